```python
import jax, jax.numpy as jnp
from jax import lax
import numpy as np

D_MODEL = 1024
BATCH = 1
SEQ = 16384
DEPTH = 4

HEAD_DIM = 64
ROPE_THETA = 10000.0
LN_EPS = 1e-5
NEG_INF = -1e30
Q_BLOCK = 128

NSA_HEADS = 8
NSA_GROUPS = 2
NSA_HPG = NSA_HEADS // NSA_GROUPS
NSA_Q_W = NSA_HEADS * HEAD_DIM
NSA_KV_W = NSA_GROUPS * HEAD_DIM
CMP_LEN = 32
CMP_STRIDE = 16
CMP_HIDDEN = 128
SLC_BLOCK = 64
SLC_RATIO = SLC_BLOCK // CMP_STRIDE
SLC_TOPK = 16
NSA_WINDOW = 512
FORCE_SCORE = 1e6

DIL_PAIRS = ((128, 1), (512, 4), (2048, 16))
DIL_HPG = 4
DIL_HEADS = DIL_HPG * 3
DIL_W = DIL_HEADS * HEAD_DIM
DIL_OUT_W = DIL_HPG * HEAD_DIM

SGU_CHUNK = 128
SGU_GROUPS = 4
SGU_GROUP_CH = 128
SGU_WIDTH = SGU_GROUPS * SGU_GROUP_CH

FFN_DIM = 2816

N_BRANCH = 3
IN_COLS = NSA_Q_W + 6 * NSA_KV_W + N_BRANCH * NSA_HEADS + 3 * DIL_W + 2 * SGU_WIDTH + N_BRANCH * D_MODEL

DEEPNORM_ALPHA = (2 * DEPTH) ** 0.25
DEEPNORM_BETA = (8 * DEPTH) ** -0.25

kernel_name = 'hybrid_nsa_dilated_sgu_macaron_deepnorm'


def layer_norm(x, g, b):
    xf = x.astype(jnp.float32)
    mu = jnp.mean(xf, axis=-1, keepdims=True)
    var = jnp.mean(jnp.square(xf - mu), axis=-1, keepdims=True)
    return ((xf - mu) * lax.rsqrt(var + LN_EPS) * g + b).astype(x.dtype)


def swiglu(x, w_gate, w_up, w_down):
    return (jax.nn.silu(x @ w_gate) * (x @ w_up)) @ w_down


def rope_tables(seq):
    inv = 1.0 / (ROPE_THETA ** (jnp.arange(0, HEAD_DIM, 2, dtype=jnp.float32) / HEAD_DIM))
    ang = jnp.arange(seq, dtype=jnp.float32)[:, None] * inv[None, :]
    return jnp.cos(ang), jnp.sin(ang)


def apply_rope(x, cos, sin):
    half = HEAD_DIM // 2
    shape = (1, x.shape[1]) + (1,) * (x.ndim - 3) + (half,)
    c = cos.reshape(shape).astype(x.dtype)
    s = sin.reshape(shape).astype(x.dtype)
    x1, x2 = x[..., :half], x[..., half:]
    return jnp.concatenate([x1 * c - x2 * s, x2 * c + x1 * s], axis=-1)


def masked_softmax(scores, mask):
    s = jnp.where(mask, scores.astype(jnp.float32), NEG_INF)
    m = jnp.max(s, axis=-1, keepdims=True)
    e = jnp.exp(s - m) * mask
    den = jnp.sum(e, axis=-1, keepdims=True)
    p = e / jnp.maximum(den, 1e-30)
    lse = m[..., 0] + jnp.log(jnp.maximum(den[..., 0], 1e-30))
    return p, lse


def compress_blocks(x, pos, w1, w2):
    B, S, G, Dh = x.shape
    n_cmp = S // CMP_STRIDE
    xp = jnp.pad(x, ((0, 0), (0, CMP_STRIDE), (0, 0), (0, 0)))
    sub = xp.reshape(B, n_cmp + 1, CMP_STRIDE, G, Dh)
    blk = jnp.concatenate([sub[:, :-1], sub[:, 1:]], axis=2) + pos[None, None, :, None, :]
    flat = blk.transpose(0, 1, 3, 2, 4).reshape(B, n_cmp, G, CMP_LEN * Dh)
    return jax.nn.gelu(flat @ w1) @ w2


def nsa_mixer(q, k_c, v_c, k_s, v_s, k_w, v_w, gate, pk_pos, pk_w1, pk_w2, pv_pos, pv_w1, pv_w2, cos, sin):
    B, S = q.shape[:2]
    G, HPG, Dh = NSA_GROUPS, NSA_HPG, HEAD_DIM
    scale = Dh ** -0.5
    n_cmp = S // CMP_STRIDE
    n_slc = S // SLC_BLOCK
    n_blk = S // Q_BLOCK
    top_k = min(SLC_TOPK, n_slc)
    q = q.reshape(B, S, G, HPG, Dh)
    q_rot = apply_rope(q, cos, sin)
    k_s = apply_rope(k_s, cos, sin)
    k_w = apply_rope(k_w, cos, sin)
    kc = compress_blocks(k_c, pk_pos, pk_w1, pk_w2)
    vc = compress_blocks(v_c, pv_pos, pv_w1, pv_w2)
    c_end = jnp.arange(n_cmp) * CMP_STRIDE + (CMP_LEN - 1)
    ks_b = k_s.reshape(B, n_slc, SLC_BLOCK, G, Dh).transpose(0, 3, 1, 2, 4)
    vs_b = v_s.reshape(B, n_slc, SLC_BLOCK, G, Dh).transpose(0, 3, 1, 2, 4)
    kw_p = jnp.pad(k_w, ((0, 0), (NSA_WINDOW, 0), (0, 0), (0, 0)))
    vw_p = jnp.pad(v_w, ((0, 0), (NSA_WINDOW, 0), (0, 0), (0, 0)))
    slc_idx = jnp.arange(n_slc)
    bi = jnp.arange(B)[:, None, None, None]
    gi = jnp.arange(G)[None, :, None, None]

    def block_fn(args):
        blk, qr, qp = args
        t = blk * Q_BLOCK + jnp.arange(Q_BLOCK)
        s_c = jnp.einsum('bqghd,bngd->bghqn', qr, kc) * scale
        p_c, _ = masked_softmax(s_c, c_end[None, :] <= t[:, None])
        o_c = jnp.einsum('bghqn,bngd->bqghd', p_c.astype(vc.dtype), vc)
        imp = jnp.pad(jnp.sum(p_c, axis=2), ((0, 0), (0, 0), (0, 0), (1, SLC_RATIO - 1)))
        p_slc = imp[..., :n_cmp].reshape(B, G, Q_BLOCK, n_slc, SLC_RATIO).sum(-1) + imp[..., SLC_RATIO::SLC_RATIO]
        cur = (t // SLC_BLOCK)[:, None]
        forced = (slc_idx[None] == 0) | (slc_idx[None] == cur) | (slc_idx[None] == cur - 1)
        future = slc_idx[None] * SLC_BLOCK > t[:, None]
        sel_score = jnp.where(forced, FORCE_SCORE, jnp.where(future, -1.0, p_slc))
        _, idx = lax.top_k(sel_score, top_k)
        kg = ks_b[bi, gi, idx].reshape(B, G, Q_BLOCK, top_k * SLC_BLOCK, Dh)
        vg = vs_b[bi, gi, idx].reshape(B, G, Q_BLOCK, top_k * SLC_BLOCK, Dh)
        kpos = (idx[..., None] * SLC_BLOCK + jnp.arange(SLC_BLOCK)).reshape(B, G, Q_BLOCK, top_k * SLC_BLOCK)
        s_s = jnp.einsum('bqghd,bgqkd->bghqk', qp, kg) * scale
        p_s, _ = masked_softmax(s_s, (kpos <= t[None, None, :, None])[:, :, None])
        o_s = jnp.einsum('bghqk,bgqkd->bqghd', p_s.astype(vg.dtype), vg)
        kw = lax.dynamic_slice_in_dim(kw_p, blk * Q_BLOCK, Q_BLOCK + NSA_WINDOW, axis=1)
        vw = lax.dynamic_slice_in_dim(vw_p, blk * Q_BLOCK, Q_BLOCK + NSA_WINDOW, axis=1)
        kpos_w = blk * Q_BLOCK - NSA_WINDOW + jnp.arange(Q_BLOCK + NSA_WINDOW)
        diff = t[:, None] - kpos_w[None, :]
        m_w = (diff >= 0) & (diff < NSA_WINDOW) & (kpos_w[None, :] >= 0)
        s_w = jnp.einsum('bqghd,bkgd->bghqk', qp, kw) * scale
        p_w, _ = masked_softmax(s_w, m_w)
        o_w = jnp.einsum('bghqk,bkgd->bqghd', p_w.astype(vw.dtype), vw)
        return o_c, o_s, o_w

    def to_blocks(x):
        return jnp.moveaxis(x.reshape(B, n_blk, Q_BLOCK, G, HPG, Dh), 1, 0)

    o_c, o_s, o_w = lax.map(block_fn, (jnp.arange(n_blk), to_blocks(q), to_blocks(q_rot)))
    back = lambda o: jnp.moveaxis(o, 0, 1).reshape(B, S, G, HPG, Dh)
    g = jax.nn.sigmoid(gate.reshape(B, S, G, HPG, N_BRANCH))[..., None]
    o = g[..., 0, :] * back(o_c) + g[..., 1, :] * back(o_s) + g[..., 2, :] * back(o_w)
    return o.reshape(B, S, NSA_Q_W)


def dilated_group(q, k, v, span, dil):
    B, S, Hg, Dh = q.shape
    unit = dil * Q_BLOCK
    s_pad = -(-S // unit) * unit
    m_len = s_pad // dil
    n_sub = m_len // Q_BLOCK

    def to_sub(x):
        x = jnp.pad(x, ((0, 0), (0, s_pad - S), (0, 0), (0, 0)))
        x = x.reshape(B, m_len, dil, Hg, Dh).transpose(0, 2, 1, 3, 4)
        return x.reshape(B, dil, n_sub, Q_BLOCK, Hg, Dh)

    def with_prev(x):
        prev = jnp.pad(x, ((0, 0), (0, 0), (1, 0), (0, 0), (0, 0), (0, 0)))[:, :, :-1]
        return jnp.concatenate([prev, x], axis=3)

    qs = to_sub(q)
    ks = with_prev(to_sub(k))
    vs = with_prev(to_sub(v))
    scores = jnp.einsum('brnqhd,brnkhd->brnhqk', qs, ks) * (Dh ** -0.5)
    qi = jnp.arange(Q_BLOCK)[:, None] + Q_BLOCK
    ki = jnp.arange(2 * Q_BLOCK)[None, :]
    delta = qi - ki
    band = (delta >= 0) & (delta <= span)
    first = (jnp.arange(n_sub) == 0)[:, None, None] & (ki < Q_BLOCK)[None]
    mask = (band[None] & ~first)[:, None]
    p, lse = masked_softmax(scores, mask)
    o = jnp.einsum('brnhqk,brnkhd->brnqhd', p.astype(vs.dtype), vs)
    o = o.reshape(B, dil, m_len, Hg, Dh).transpose(0, 2, 1, 3, 4).reshape(B, s_pad, Hg, Dh)[:, :S]
    lse = lse.transpose(0, 1, 2, 4, 3).reshape(B, dil, m_len, Hg).transpose(0, 2, 1, 3).reshape(B, s_pad, Hg)[:, :S]
    return o, lse


def dilated_mixer(q, k, v, cos, sin):
    B, S = q.shape[:2]
    q = apply_rope(q.reshape(B, S, DIL_HEADS, HEAD_DIM), cos, sin)
    k = apply_rope(k.reshape(B, S, DIL_HEADS, HEAD_DIM), cos, sin)
    v = v.reshape(B, S, DIL_HEADS, HEAD_DIM)
    outs, lses = [], []
    for gidx, (win, dil) in enumerate(DIL_PAIRS):
        hs = slice(gidx * DIL_HPG, (gidx + 1) * DIL_HPG)
        o, lse = dilated_group(q[:, :, hs], k[:, :, hs], v[:, :, hs], win // dil, dil)
        outs.append(o)
        lses.append(lse)
    w = jax.nn.softmax(jnp.stack(lses, axis=0), axis=0)
    o = jnp.sum(w[..., None].astype(v.dtype) * jnp.stack(outs, axis=0), axis=0)
    return o.reshape(B, S, DIL_OUT_W)


def sgu_mixer(uv, ln_g, ln_b, w_s, b_s):
    B, S, _ = uv.shape
    uv = jax.nn.gelu(uv)
    u, v = uv[..., :SGU_WIDTH], uv[..., SGU_WIDTH:]
    v = layer_norm(v, ln_g, ln_b)
    vc = v.reshape(B, S // SGU_CHUNK, SGU_CHUNK, SGU_GROUPS, SGU_GROUP_CH)
    causal = jnp.tril(jnp.ones((SGU_CHUNK, SGU_CHUNK), dtype=bool))
    ws = jnp.where(causal[None], w_s, 0)
    sv = jnp.einsum('gts,bnsgc->bntgc', ws, vc) + b_s.T[None, None, :, :, None]
    return u * sv.reshape(B, S, SGU_WIDTH)


def token_mixing(h, w_in, pk_pos, pk_w1, pk_w2, pv_pos, pv_w1, pv_w2, sgu_ln_g, sgu_ln_b, sgu_w, sgu_b,
                 w_branch_a, w_branch_b, w_branch_c, w_out, cos, sin):
    sizes = [NSA_Q_W] + [NSA_KV_W] * 6 + [N_BRANCH * NSA_HEADS, DIL_W, DIL_W, DIL_W, 2 * SGU_WIDTH, N_BRANCH * D_MODEL]
    z = h @ w_in
    parts = jnp.split(z, np.cumsum(sizes)[:-1].tolist(), axis=-1)
    q_a, kc, vc, ks, vs, kw, vw, g_a, q_b, k_b, v_b, uv, g_m = parts
    B, S, _ = h.shape
    kv = lambda t: t.reshape(B, S, NSA_GROUPS, HEAD_DIM)
    y_a = nsa_mixer(q_a, kv(kc), kv(vc), kv(ks), kv(vs), kv(kw), kv(vw), g_a,
                    pk_pos, pk_w1, pk_w2, pv_pos, pv_w1, pv_w2, cos, sin) @ w_branch_a
    y_b = dilated_mixer(q_b, k_b, v_b, cos, sin) @ w_branch_b
    y_c = sgu_mixer(uv, sgu_ln_g, sgu_ln_b, sgu_w, sgu_b) @ w_branch_c
    g = jax.nn.sigmoid(g_m)
    merged = g[..., :D_MODEL] * y_a + g[..., D_MODEL:2 * D_MODEL] * y_b + g[..., 2 * D_MODEL:] * y_c
    return merged @ w_out


def setup_inputs(seed: int = 0) -> dict:
    key = jax.random.key(seed)
    ks = jax.random.split(key, 24)
    L, D = DEPTH, D_MODEL

    def nrm(k, shape, scale):
        return jax.random.normal(k, shape, jnp.float32) * scale

    return {
        'x': nrm(ks[0], (BATCH, SEQ, D), 1.0),
        'ln_g': 1.0 + nrm(ks[1], (L, 3, D), 0.02),
        'ln_b': nrm(ks[2], (L, 3, D), 0.02),
        'ffn1_gate': nrm(ks[3], (L, D, FFN_DIM), D ** -0.5),
        'ffn1_up': nrm(ks[4], (L, D, FFN_DIM), D ** -0.5),
        'ffn1_down': nrm(ks[5], (L, FFN_DIM, D), FFN_DIM ** -0.5 * DEEPNORM_BETA),
        'ffn2_gate': nrm(ks[6], (L, D, FFN_DIM), D ** -0.5),
        'ffn2_up': nrm(ks[7], (L, D, FFN_DIM), D ** -0.5),
        'ffn2_down': nrm(ks[8], (L, FFN_DIM, D), FFN_DIM ** -0.5 * DEEPNORM_BETA),
        'w_in': nrm(ks[9], (L, D, IN_COLS), D ** -0.5),
        'phi_k_pos': nrm(ks[10], (L, CMP_LEN, HEAD_DIM), 0.5),
        'phi_k_w1': nrm(ks[11], (L, CMP_LEN * HEAD_DIM, CMP_HIDDEN), (CMP_LEN * HEAD_DIM) ** -0.5),
        'phi_k_w2': nrm(ks[12], (L, CMP_HIDDEN, HEAD_DIM), CMP_HIDDEN ** -0.5),
        'phi_v_pos': nrm(ks[13], (L, CMP_LEN, HEAD_DIM), 0.5),
        'phi_v_w1': nrm(ks[14], (L, CMP_LEN * HEAD_DIM, CMP_HIDDEN), (CMP_LEN * HEAD_DIM) ** -0.5),
        'phi_v_w2': nrm(ks[15], (L, CMP_HIDDEN, HEAD_DIM), CMP_HIDDEN ** -0.5),
        'sgu_ln_g': 1.0 + nrm(ks[16], (L, SGU_WIDTH), 0.02),
        'sgu_ln_b': nrm(ks[17], (L, SGU_WIDTH), 0.02),
        'sgu_w': nrm(ks[18], (L, SGU_GROUPS, SGU_CHUNK, SGU_CHUNK), 0.5 * SGU_CHUNK ** -0.5),
        'sgu_b': 1.0 + nrm(ks[19], (L, SGU_GROUPS, SGU_CHUNK), 0.02),
        'w_branch_a': nrm(ks[20], (L, NSA_Q_W, D), NSA_Q_W ** -0.5),
        'w_branch_b': nrm(ks[21], (L, DIL_OUT_W, D), DIL_OUT_W ** -0.5),
        'w_branch_c': nrm(ks[22], (L, SGU_WIDTH, D), SGU_WIDTH ** -0.5),
        'w_out': nrm(ks[23], (L, D, D), D ** -0.5 * DEEPNORM_BETA),
    }


def reference(x, ln_g, ln_b, ffn1_gate, ffn1_up, ffn1_down, ffn2_gate, ffn2_up, ffn2_down, w_in,
              phi_k_pos, phi_k_w1, phi_k_w2, phi_v_pos, phi_v_w1, phi_v_w2,
              sgu_ln_g, sgu_ln_b, sgu_w, sgu_b, w_branch_a, w_branch_b, w_branch_c, w_out):
    cos, sin = rope_tables(x.shape[1])
    for l in range(DEPTH):
        x = layer_norm(DEEPNORM_ALPHA * x + 0.5 * swiglu(x, ffn1_gate[l], ffn1_up[l], ffn1_down[l]),
                       ln_g[l, 0], ln_b[l, 0])
        mix = token_mixing(x, w_in[l], phi_k_pos[l], phi_k_w1[l], phi_k_w2[l], phi_v_pos[l], phi_v_w1[l], phi_v_w2[l],
                           sgu_ln_g[l], sgu_ln_b[l], sgu_w[l], sgu_b[l],
                           w_branch_a[l], w_branch_b[l], w_branch_c[l], w_out[l], cos, sin)
        x = layer_norm(DEEPNORM_ALPHA * x + mix, ln_g[l, 1], ln_b[l, 1])
        x = layer_norm(DEEPNORM_ALPHA * x + 0.5 * swiglu(x, ffn2_gate[l], ffn2_up[l], ffn2_down[l]),
                       ln_g[l, 2], ln_b[l, 2])
    return x
```

```python
import functools

import numpy as np
import jax
import jax.numpy as jnp
from jax import lax
from jax.experimental import pallas as pl
from jax.experimental.pallas import tpu as pltpu

F32 = jnp.float32
BF16 = jnp.bfloat16

HEAD_DIM = 64
ROPE_THETA = 10000.0
LN_EPS = 1e-5
Q_BLOCK = 128
NSA_HEADS = 8
NSA_GROUPS = 2
NSA_HPG = NSA_HEADS // NSA_GROUPS
NSA_Q_W = NSA_HEADS * HEAD_DIM
NSA_KV_W = NSA_GROUPS * HEAD_DIM
CMP_LEN = 32
CMP_STRIDE = 16
SLC_BLOCK = 64
SLC_RATIO = SLC_BLOCK // CMP_STRIDE
SLC_TOPK = 16
NSA_WINDOW = 512
FORCE_SCORE = 1e6
DIL_PAIRS = ((128, 1), (512, 4), (2048, 16))
DIL_HPG = 4
DIL_HEADS = DIL_HPG * 3
DIL_W = DIL_HEADS * HEAD_DIM
DIL_OUT_W = DIL_HPG * HEAD_DIM
SGU_CHUNK = 128
SGU_GROUPS = 4
SGU_GROUP_CH = 128
SGU_WIDTH = SGU_GROUPS * SGU_GROUP_CH
N_BRANCH = 3

LANES = 128
V7X_VMEM_BYTES = 64 * 1024 * 1024
VMEM_CAP = 56 * 1024 * 1024

M_INIT = -1e30
MASKED = -(2.0 ** 101)


def _cparams(sem, vmem_bytes):
    return pltpu.CompilerParams(dimension_semantics=sem,
                                vmem_limit_bytes=int(min(max(vmem_bytes, 16 * 1024 * 1024), VMEM_CAP)))


def _nbytes(shape, dtype):
    return int(np.prod(shape)) * jnp.dtype(dtype).itemsize


def _layer_norm(y, g, b):
    mu = jnp.mean(y, axis=-1, keepdims=True)
    d = y - mu
    var = jnp.mean(d * d, axis=-1, keepdims=True)
    return d * lax.rsqrt(var + LN_EPS) * g + b


def _lane_iota(shape):
    return lax.broadcasted_iota(jnp.int32, shape, len(shape) - 1)


def _row_iota(shape):
    return lax.broadcasted_iota(jnp.int32, shape, 0)


def _ffn_kernel(x_ref, wg_ref, wu_ref, wd_ref, g_ref, b_ref, o_ref, ob_ref, *, alpha, fc):
    x = x_ref[...]
    xb = x.astype(BF16)
    n_f = wg_ref.shape[1]
    acc = jnp.zeros(x.shape, F32)
    for c in range(n_f // fc):
        sl = slice(c * fc, (c + 1) * fc)
        gate = jnp.dot(xb, wg_ref[:, sl], preferred_element_type=F32)
        up = jnp.dot(xb, wu_ref[:, sl], preferred_element_type=F32)
        h = (gate * jax.nn.sigmoid(gate)) * up
        acc = acc + jnp.dot(h.astype(BF16), wd_ref[sl, :], preferred_element_type=F32)
    out = _layer_norm(alpha * x + 0.5 * acc, g_ref[...], b_ref[...])
    o_ref[...] = out
    ob_ref[...] = out.astype(BF16)


def _ffn(x, wg, wu, wd, g, b, alpha):
    s, d = x.shape
    n_f = wg.shape[1]
    tm = min(512, s)
    fc = 256
    assert s % tm == 0 and n_f % fc == 0
    const = lambda i: (0, 0)
    row = lambda i: (i, 0)
    vmem = (2 * 2 * _nbytes((tm, d), F32) + 2 * _nbytes((tm, d), BF16)
            + 2 * 3 * _nbytes((d, n_f), BF16) + 6 * _nbytes((tm, d), F32))
    return pl.pallas_call(
        functools.partial(_ffn_kernel, alpha=alpha, fc=fc),
        grid=(s // tm,),
        in_specs=[pl.BlockSpec((tm, d), row), pl.BlockSpec((d, n_f), const), pl.BlockSpec((d, n_f), const),
                  pl.BlockSpec((n_f, d), const), pl.BlockSpec((1, d), const), pl.BlockSpec((1, d), const)],
        out_specs=[pl.BlockSpec((tm, d), row), pl.BlockSpec((tm, d), row)],
        out_shape=[jax.ShapeDtypeStruct((s, d), F32), jax.ShapeDtypeStruct((s, d), BF16)],
        compiler_params=_cparams(("parallel",), vmem),
        name="ffn_deepnorm",
    )(x, wg, wu, wd, g.reshape(1, d), b.reshape(1, d))


def _swap_halves_64(a):
    half = HEAD_DIM // 2
    first = (_lane_iota(a.shape) % HEAD_DIM) < half
    return jnp.where(first, pltpu.roll(a, LANES - half, 1), pltpu.roll(a, half, 1))


def _mm_kernel(*refs, rope, scale):
    if rope:
        x_ref, w_ref, c_ref, s_ref, o_ref = refs
    else:
        x_ref, w_ref, o_ref = refs
    acc = jnp.dot(x_ref[...], w_ref[...], preferred_element_type=F32)
    if rope:
        cos = c_ref[...]
        sin = s_ref[...]
        for c in range(acc.shape[1] // LANES):
            a = acc[:, c * LANES:(c + 1) * LANES]
            r = a * cos + _swap_halves_64(a) * sin
            if scale != 1.0:
                r = r * scale
            o_ref[:, c * LANES:(c + 1) * LANES] = r.astype(o_ref.dtype)
    else:
        if scale != 1.0:
            acc = acc * scale
        o_ref[...] = acc.astype(o_ref.dtype)


def _mm(xb, w, out_dtype, *, rope_tabs=None, scale=1.0):
    s, k = xb.shape
    n = w.shape[1]
    tm = min(1024, s)
    tn = 512 if n % 512 == 0 else (256 if n % 256 == 0 else 128)
    assert s % tm == 0 and n % tn == 0
    in_specs = [pl.BlockSpec((tm, k), lambda i, j: (i, 0)), pl.BlockSpec((k, tn), lambda i, j: (0, j))]
    args = [xb, w]
    if rope_tabs is not None:
        in_specs += [pl.BlockSpec((tm, LANES), lambda i, j: (i, 0))] * 2
        args += list(rope_tabs)
    vmem = 2 * (_nbytes((tm, k), BF16) + _nbytes((k, tn), BF16) + _nbytes((tm, tn), out_dtype)
                + 2 * _nbytes((tm, LANES), F32)) + 3 * _nbytes((tm, tn), F32)
    return pl.pallas_call(
        functools.partial(_mm_kernel, rope=rope_tabs is not None, scale=scale),
        grid=(s // tm, n // tn),
        in_specs=in_specs,
        out_specs=pl.BlockSpec((tm, tn), lambda i, j: (i, j)),
        out_shape=jax.ShapeDtypeStruct((s, n), out_dtype),
        compiler_params=_cparams(("parallel", "parallel"), vmem),
        name="proj_rope" if rope_tabs is not None else "proj",
    )(*args)


def _compress_kernel(a_ref, pa_ref, pb_ref, w1a_ref, w1b_ref, w2_ref, o_ref):
    n_cmp = a_ref.shape[1]
    pb = pb_ref[...]
    tail = jnp.dot(jnp.broadcast_to(pb, (8, pb.shape[1])).astype(BF16), w1b_ref[...],
                   preferred_element_type=F32)[0:1, :]
    last = _row_iota((n_cmp, 1)) == n_cmp - 1
    outs = []
    for g in range(NSA_GROUPS):
        a = a_ref[g]
        ha = jnp.dot((a + pa_ref[...]).astype(BF16), w1a_ref[...], preferred_element_type=F32)
        hb = jnp.dot((a + pb).astype(BF16), w1b_ref[...], preferred_element_type=F32)
        hb_next = jnp.where(last, tail, pltpu.roll(hb, n_cmp - 1, 0))
        h = jax.nn.gelu(ha + hb_next)
        outs.append(jnp.dot(h.astype(BF16), w2_ref[...], preferred_element_type=F32))
    lo = _lane_iota(outs[0].shape) < HEAD_DIM
    o_ref[...] = jnp.where(lo, outs[0], outs[1]).astype(o_ref.dtype)


def _compress(x, pos, w1, w2):
    s = x.shape[0]
    n_cmp = s // CMP_STRIDE
    half_w = CMP_STRIDE * HEAD_DIM
    a = x.reshape(n_cmp, CMP_STRIDE, NSA_GROUPS, HEAD_DIM).transpose(2, 0, 1, 3).reshape(NSA_GROUPS, n_cmp, half_w)
    posf = pos.reshape(1, CMP_LEN * HEAD_DIM)
    w1b16 = w1.astype(BF16)
    w2d = jnp.concatenate([w2, w2], axis=1).astype(BF16)
    hidden = w1.shape[1]
    vmem = 2 * (_nbytes(a.shape, F32) + 2 * _nbytes((half_w, hidden), BF16)) + 6 * _nbytes((n_cmp, half_w), F32)
    return pl.pallas_call(
        _compress_kernel,
        out_shape=jax.ShapeDtypeStruct((n_cmp, NSA_KV_W), BF16),
        compiler_params=pltpu.CompilerParams(vmem_limit_bytes=int(min(max(vmem, 16 << 20), VMEM_CAP))),
        name="nsa_compress",
    )(a, posf[:, :half_w], posf[:, half_w:], w1b16[:half_w], w1b16[half_w:], w2d)


def _stack_group_queries(q_ref, col0, kv_lane_group):
    parts = []
    for h in range(NSA_HPG):
        c = col0 + (h // 2) * LANES
        qc = q_ref[:, c:c + LANES].astype(F32)
        e = h % 2
        lane = _lane_iota(qc.shape)
        mine = (lane >= e * HEAD_DIM) & (lane < (e + 1) * HEAD_DIM)
        qh = jnp.where(mine, qc, 0.0)
        if e != kv_lane_group:
            qh = pltpu.roll(qh, HEAD_DIM, 1)
        parts.append(qh.astype(BF16))
    return jnp.concatenate(parts, axis=0)


def _unstack_group_outputs(o, kv_lane_group):
    chunks = []
    for c in range(NSA_HPG // 2):
        halves = []
        for e in range(2):
            h = 2 * c + e
            oh = o[h * Q_BLOCK:(h + 1) * Q_BLOCK, :]
            if e != kv_lane_group:
                oh = pltpu.roll(oh, HEAD_DIM, 1)
            halves.append(oh)
        lo = _lane_iota(halves[0].shape) < HEAD_DIM
        chunks.append(jnp.where(lo, halves[0], halves[1]))
    return jnp.concatenate(chunks, axis=1)


def _qk(q, k):
    return lax.dot_general(q, k, (((1,), (1,)), ((), ())), preferred_element_type=F32)


def _softmax_rows(s, mask):
    sm = jnp.where(mask, s, MASKED)
    m = jnp.maximum(jnp.max(sm, axis=-1, keepdims=True), M_INIT)
    e = jnp.exp(sm - m)
    return e, jnp.sum(e, axis=-1, keepdims=True), m


def _cmp_select_kernel(q_ref, kc_ref, vc_ref, mmap_ref, oc_ref, nsel_ref, *, top_k):
    b = pl.program_id(0)
    n_cmp = kc_ref.shape[0]
    n_slc = mmap_ref.shape[1]
    rows = NSA_HPG * Q_BLOCK
    t4 = b * Q_BLOCK + (_row_iota((rows, 1)) % Q_BLOCK)
    c_end = _lane_iota((1, n_cmp)) * CMP_STRIDE + (CMP_LEN - 1)
    valid = c_end <= t4
    t = b * Q_BLOCK + _row_iota((Q_BLOCK, 1))
    blk = _lane_iota((1, n_slc))
    cur = t // SLC_BLOCK
    forced = (blk == 0) | (blk == cur) | (blk == cur - 1)
    future = blk * SLC_BLOCK > t
    blk_f = blk.astype(F32)
    kc = kc_ref[...]
    vc = vc_ref[...]
    mmap = mmap_ref[...]
    for g in range(NSA_GROUPS):
        qst = _stack_group_queries(q_ref, g * NSA_HPG * HEAD_DIM, g)
        e, den, _ = _softmax_rows(_qk(qst, kc), valid)
        p = e / jnp.maximum(den, 1e-30)
        o = jnp.dot(p.astype(BF16), vc, preferred_element_type=F32)
        oc_ref[:, g * NSA_HPG * HEAD_DIM:(g + 1) * NSA_HPG * HEAD_DIM] = _unstack_group_outputs(o, g)
        psum = p[0:Q_BLOCK]
        for h in range(1, NSA_HPG):
            psum = psum + p[h * Q_BLOCK:(h + 1) * Q_BLOCK]
        hi = psum.astype(BF16)
        r1 = psum - hi.astype(F32)
        mid = r1.astype(BF16)
        low = (r1 - mid.astype(F32)).astype(BF16)
        p_slc = (jnp.dot(hi, mmap, preferred_element_type=F32) + jnp.dot(mid, mmap, preferred_element_type=F32)
                 + jnp.dot(low, mmap, preferred_element_type=F32))
        score = jnp.where(forced, FORCE_SCORE, jnp.where(future, -1.0, p_slc))
        not_picked = jnp.ones(score.shape, F32)
        for _ in range(top_k):
            mx = jnp.max(score, axis=-1, keepdims=True)
            first = jnp.min(jnp.where(score == mx, blk_f, float(n_slc)), axis=-1, keepdims=True)
            hit = blk_f == first
            not_picked = jnp.where(hit, 0.0, not_picked)
            score = jnp.where(hit, -2.0, score)
        nsel_ref[:, g * n_slc:(g + 1) * n_slc] = not_picked.astype(nsel_ref.dtype)


def _cmp_select(q_u, kc, vc):
    s = q_u.shape[0]
    n_cmp, n_slc = s // CMP_STRIDE, s // SLC_BLOCK
    top_k = min(SLC_TOPK, n_slc)
    m = np.arange(n_cmp)[:, None]
    j = np.arange(n_slc)[None, :]
    mmap = jnp.asarray(((m >= SLC_RATIO * j - 1) & (m <= SLC_RATIO * j + SLC_RATIO - 1)).astype(np.float32), BF16)
    const = lambda b: (0, 0)
    rows = NSA_HPG * Q_BLOCK
    vmem = 2 * (2 * _nbytes((n_cmp, NSA_KV_W), BF16) + _nbytes((n_cmp, n_slc), BF16)
                + _nbytes((Q_BLOCK, NSA_Q_W), F32) * 2) + 8 * _nbytes((rows, n_cmp), F32)
    return pl.pallas_call(
        functools.partial(_cmp_select_kernel, top_k=top_k),
        grid=(s // Q_BLOCK,),
        in_specs=[pl.BlockSpec((Q_BLOCK, NSA_Q_W), lambda b: (b, 0)), pl.BlockSpec((n_cmp, NSA_KV_W), const),
                  pl.BlockSpec((n_cmp, NSA_KV_W), const), pl.BlockSpec((n_cmp, n_slc), const)],
        out_specs=[pl.BlockSpec((Q_BLOCK, NSA_Q_W), lambda b: (b, 0)),
                   pl.BlockSpec((Q_BLOCK, NSA_GROUPS * n_slc), lambda b: (b, 0))],
        out_shape=[jax.ShapeDtypeStruct((s, NSA_Q_W), F32), jax.ShapeDtypeStruct((s, NSA_GROUPS * n_slc), BF16)],
        compiler_params=_cparams(("parallel",), vmem),
        name="nsa_compressed_select",
    )(q_u, kc, vc, mmap)


def _sel_attn_kernel(q_ref, k_ref, v_ref, nsel_ref, e_ref, o_ref, m_ref, l_ref, acc_ref, *, win):
    b = pl.program_id(0)
    n_slc = nsel_ref.shape[1] // NSA_GROUPS
    n_win = n_slc // win
    tiles_per_win = win // 2
    gw = NSA_HPG * HEAD_DIM

    for g in range(NSA_GROUPS):
        qst = _stack_group_queries(q_ref, g * gw, g)
        m_ref[...] = jnp.full(m_ref.shape, M_INIT, F32)
        l_ref[...] = jnp.zeros(l_ref.shape, F32)
        acc_ref[...] = jnp.zeros(acc_ref.shape, F32)

        def tile(kt, causal, g=g, qst=qst):
            off = pl.multiple_of(kt * Q_BLOCK, Q_BLOCK)
            k = k_ref[pl.ds(off, Q_BLOCK), :]
            v = v_ref[pl.ds(off, Q_BLOCK), :]
            s = _qk(qst, k)
            w = kt // tiles_per_win
            base = g * n_slc
            nsel = nsel_ref[:, base:base + win]
            for i in range(1, n_win):
                nsel = jnp.where(w == i, nsel_ref[:, base + i * win:base + (i + 1) * win], nsel)
            bias = jnp.dot(nsel, e_ref[kt % tiles_per_win], preferred_element_type=F32)
            if causal:
                keep = _lane_iota((Q_BLOCK, Q_BLOCK)) <= _row_iota((Q_BLOCK, Q_BLOCK))
                bias = jnp.where(keep, bias, MASKED)
            s = s + jnp.concatenate([bias] * NSA_HPG, axis=0)
            m_old = m_ref[...]
            m_new = jnp.maximum(m_old, jnp.max(s, axis=-1, keepdims=True))
            alpha = jnp.exp(m_old - m_new)
            p = jnp.exp(s - m_new)
            l_ref[...] = alpha * l_ref[...] + jnp.sum(p, axis=-1, keepdims=True)
            acc_ref[...] = alpha * acc_ref[...] + jnp.dot(p.astype(BF16), v, preferred_element_type=F32)
            m_ref[...] = m_new

        def body(kt, carry, tile=tile):
            tile(kt, False)
            return carry

        lax.fori_loop(0, b, body, 0)
        tile(b, True)
        o = acc_ref[...] / jnp.maximum(l_ref[...], 1e-30)
        o_ref[:, g * gw:(g + 1) * gw] = _unstack_group_outputs(o, g)


def _sel_attn(rot_a, v_plain, nsel, ks_col, vs_col):
    s = rot_a.shape[0]
    n_slc = s // SLC_BLOCK
    win = min(n_slc, LANES)
    r = np.arange(win // 2)[:, None, None]
    jj = np.arange(win)[None, :, None]
    cc = np.arange(Q_BLOCK)[None, None, :]
    e_tab = jnp.asarray(np.where(jj == 2 * r + (cc >= SLC_BLOCK), MASKED, 0.0).astype(np.float32), BF16)
    rows = NSA_HPG * Q_BLOCK
    gw = NSA_HPG * HEAD_DIM
    vmem = (2 * (2 * _nbytes((s, NSA_KV_W), BF16) + _nbytes(e_tab.shape, BF16) + _nbytes((Q_BLOCK, NSA_Q_W), BF16)
                 + _nbytes((Q_BLOCK, NSA_GROUPS * n_slc), BF16) + _nbytes((Q_BLOCK, NSA_Q_W), F32))
            + 12 * _nbytes((rows, LANES), F32))
    return pl.pallas_call(
        functools.partial(_sel_attn_kernel, win=win),
        grid=(s // Q_BLOCK,),
        in_specs=[pl.BlockSpec((Q_BLOCK, NSA_Q_W), lambda b: (b, 0)),
                  pl.BlockSpec((s, NSA_KV_W), lambda b: (0, ks_col)),
                  pl.BlockSpec((s, NSA_KV_W), lambda b: (0, vs_col)),
                  pl.BlockSpec((Q_BLOCK, NSA_GROUPS * n_slc), lambda b: (b, 0)),
                  pl.BlockSpec(e_tab.shape, lambda b: (0, 0, 0))],
        out_specs=pl.BlockSpec((Q_BLOCK, NSA_Q_W), lambda b: (b, 0)),
        out_shape=jax.ShapeDtypeStruct((s, NSA_Q_W), F32),
        scratch_shapes=[pltpu.VMEM((rows, 1), F32), pltpu.VMEM((rows, 1), F32), pltpu.VMEM((rows, LANES), F32)],
        compiler_params=_cparams(("parallel",), vmem),
        name="nsa_selected",
    )(rot_a, rot_a, v_plain, nsel, e_tab)


def _win_attn_kernel(q_ref, k_ref, v_ref, o_ref, *, span):
    b = pl.program_id(0)
    rows = NSA_HPG * Q_BLOCK
    n_back = NSA_WINDOW // Q_BLOCK
    start = pl.multiple_of(jnp.maximum(b - n_back, 0) * Q_BLOCK, Q_BLOCK)
    k = k_ref[pl.ds(start, span), :]
    v = v_ref[pl.ds(start, span), :]
    t4 = b * Q_BLOCK + (_row_iota((rows, 1)) % Q_BLOCK)
    diff = t4 - (start + _lane_iota((1, span)))
    mask = (diff >= 0) & (diff < NSA_WINDOW)
    for g in range(NSA_GROUPS):
        qst = _stack_group_queries(q_ref, g * NSA_HPG * HEAD_DIM, g)
        e, den, _ = _softmax_rows(_qk(qst, k), mask)
        o = jnp.dot(e.astype(BF16), v, preferred_element_type=F32) / jnp.maximum(den, 1e-30)
        o_ref[:, g * NSA_HPG * HEAD_DIM:(g + 1) * NSA_HPG * HEAD_DIM] = _unstack_group_outputs(o, g)


def _win_attn(rot_a, v_plain, kw_col, vw_col):
    s = rot_a.shape[0]
    span = min(NSA_WINDOW + Q_BLOCK, s)
    rows = NSA_HPG * Q_BLOCK
    vmem = 2 * (2 * _nbytes((s, NSA_KV_W), BF16) + _nbytes((Q_BLOCK, NSA_Q_W), BF16)
                + _nbytes((Q_BLOCK, NSA_Q_W), F32)) + 8 * _nbytes((rows, span), F32)
    return pl.pallas_call(
        functools.partial(_win_attn_kernel, span=span),
        grid=(s // Q_BLOCK,),
        in_specs=[pl.BlockSpec((Q_BLOCK, NSA_Q_W), lambda b: (b, 0)),
                  pl.BlockSpec((s, NSA_KV_W), lambda b: (0, kw_col)),
                  pl.BlockSpec((s, NSA_KV_W), lambda b: (0, vw_col))],
        out_specs=pl.BlockSpec((Q_BLOCK, NSA_Q_W), lambda b: (b, 0)),
        out_shape=jax.ShapeDtypeStruct((s, NSA_Q_W), F32),
        compiler_params=_cparams(("parallel",), vmem),
        name="nsa_window",
    )(rot_a, rot_a, v_plain)


def _dil_attn_kernel(q_ref, kp_ref, kc_ref, vp_ref, vc_ref, o_ref, lse_ref, *, span):
    n = pl.program_id(1)
    qi = _row_iota((Q_BLOCK, 1)) + Q_BLOCK
    ki = _lane_iota((1, 2 * Q_BLOCK))
    delta = qi - ki
    mask = (delta >= 0) & (delta <= span) & ((n > 0) | (ki >= Q_BLOCK))
    for c in range(DIL_HPG // 2):
        sl = slice(c * LANES, (c + 1) * LANES)
        q = q_ref[:, sl].astype(F32)
        k = jnp.concatenate([kp_ref[:, sl], kc_ref[:, sl]], axis=0)
        v = jnp.concatenate([vp_ref[:, sl], vc_ref[:, sl]], axis=0)
        lo = _lane_iota(q.shape) < HEAD_DIM
        outs, lses = [], []
        for e in range(2):
            qh = jnp.where(lo if e == 0 else ~lo, q, 0.0).astype(BF16)
            ex, den, m = _softmax_rows(_qk(qh, k), mask)
            den = jnp.maximum(den, 1e-30)
            outs.append(jnp.dot(ex.astype(BF16), v, preferred_element_type=F32) / den)
            lses.append(jnp.broadcast_to(m + jnp.log(den), (Q_BLOCK, LANES)))
        o_ref[:, sl] = jnp.where(lo, outs[0], outs[1])
        lse_ref[:, sl] = jnp.where(lo, lses[0], lses[1])


def _dil_attn(qb, kb, vb, gidx, win, dil):
    s = qb.shape[0]
    assert s % (dil * Q_BLOCK) == 0
    m_len = s // dil
    n_sub = m_len // Q_BLOCK
    span = win // dil
    n_grp = DIL_HEADS // DIL_HPG
    view = lambda a: a.reshape(m_len, dil * DIL_W)
    cur = lambda r, n: (n, r * n_grp + gidx)
    prev = lambda r, n: (jnp.maximum(n - 1, 0), r * n_grp + gidx)
    blk = (Q_BLOCK, DIL_OUT_W)
    vmem = 2 * (5 * _nbytes(blk, BF16) + 2 * _nbytes(blk, F32)) + 16 * _nbytes((Q_BLOCK, 2 * Q_BLOCK), F32)
    o, lse = pl.pallas_call(
        functools.partial(_dil_attn_kernel, span=span),
        grid=(dil, n_sub),
        in_specs=[pl.BlockSpec(blk, cur), pl.BlockSpec(blk, prev), pl.BlockSpec(blk, cur),
                  pl.BlockSpec(blk, prev), pl.BlockSpec(blk, cur)],
        out_specs=[pl.BlockSpec(blk, lambda r, n: (n, r)), pl.BlockSpec(blk, lambda r, n: (n, r))],
        out_shape=[jax.ShapeDtypeStruct((m_len, dil * DIL_OUT_W), F32)] * 2,
        compiler_params=_cparams(("parallel", "parallel"), vmem),
        name=f"dilated_w{win}_d{dil}",
    )(view(qb), view(kb), view(kb), view(vb), view(vb))
    return o.reshape(s, DIL_OUT_W), lse.reshape(s, DIL_OUT_W)


def _sgu_kernel(uv_ref, g_ref, b_ref, ws_ref, bs_ref, o_ref):
    uv = jax.nn.gelu(uv_ref[...])
    u = uv[:, :SGU_WIDTH]
    v = _layer_norm(uv[:, SGU_WIDTH:], g_ref[...], b_ref[...]).astype(BF16)
    causal = _lane_iota((SGU_CHUNK, SGU_CHUNK)) <= _row_iota((SGU_CHUNK, SGU_CHUNK))
    bs = bs_ref[...]
    for n in range(uv.shape[0] // SGU_CHUNK):
        rs = slice(n * SGU_CHUNK, (n + 1) * SGU_CHUNK)
        for g in range(SGU_GROUPS):
            cs = slice(g * SGU_GROUP_CH, (g + 1) * SGU_GROUP_CH)
            ws = jnp.where(causal, ws_ref[g], 0.0).astype(BF16)
            sv = jnp.dot(ws, v[rs, cs], preferred_element_type=F32) + bs[:, cs]
            o_ref[rs, cs] = (u[rs, cs] * sv).astype(o_ref.dtype)


def _sgu(uv, ln_g, ln_b, w_s, b_s):
    s = uv.shape[0]
    tm = min(4 * SGU_CHUNK, s)
    assert s % tm == 0
    bs = jnp.repeat(b_s.T, SGU_GROUP_CH, axis=1)
    c2 = lambda i: (0, 0)
    vmem = 2 * (_nbytes((tm, 2 * SGU_WIDTH), F32) + _nbytes((tm, SGU_WIDTH), F32)) + 6 * _nbytes((tm, 2 * SGU_WIDTH), F32)
    return pl.pallas_call(
        _sgu_kernel,
        grid=(s // tm,),
        in_specs=[pl.BlockSpec((tm, 2 * SGU_WIDTH), lambda i: (i, 0)), pl.BlockSpec((1, SGU_WIDTH), c2),
                  pl.BlockSpec((1, SGU_WIDTH), c2), pl.BlockSpec(w_s.shape, lambda i: (0, 0, 0)),
                  pl.BlockSpec((SGU_CHUNK, SGU_WIDTH), c2)],
        out_specs=pl.BlockSpec((tm, SGU_WIDTH), lambda i: (i, 0)),
        out_shape=jax.ShapeDtypeStruct((s, SGU_WIDTH), F32),
        compiler_params=_cparams(("parallel",), vmem),
        name="sgu",
    )(uv, ln_g.reshape(1, -1), ln_b.reshape(1, -1), w_s, bs)


def _split_dot(x, w):
    hi = x.astype(BF16)
    lo = (x - hi.astype(F32)).astype(BF16)
    return jnp.dot(hi, w, preferred_element_type=F32) + jnp.dot(lo, w, preferred_element_type=F32)


def _mix_kernel(x_ref, oc_ref, os_ref, ow_ref, ga_ref, eg_ref, d0_ref, d1_ref, d2_ref, l0_ref, l1_ref, l2_ref,
                sg_ref, gm_ref, wa_ref, wb_ref, wc_ref, wo_ref, g_ref, b_ref, o_ref, *, alpha):
    d = x_ref.shape[1]
    gates = _split_dot(jax.nn.sigmoid(ga_ref[...]), eg_ref[...])
    o_a = (gates[:, 0:NSA_Q_W] * oc_ref[...] + gates[:, NSA_Q_W:2 * NSA_Q_W] * os_ref[...]
           + gates[:, 2 * NSA_Q_W:3 * NSA_Q_W] * ow_ref[...])
    y_a = jnp.dot(o_a.astype(BF16), wa_ref[...], preferred_element_type=F32)
    l0, l1, l2 = l0_ref[...], l1_ref[...], l2_ref[...]
    lm = jnp.maximum(jnp.maximum(l0, l1), l2)
    e0, e1, e2 = jnp.exp(l0 - lm), jnp.exp(l1 - lm), jnp.exp(l2 - lm)
    den = e0 + e1 + e2
    o_b = (e0 / den) * d0_ref[...] + (e1 / den) * d1_ref[...] + (e2 / den) * d2_ref[...]
    y_b = jnp.dot(o_b.astype(BF16), wb_ref[...], preferred_element_type=F32)
    y_c = jnp.dot(sg_ref[...].astype(BF16), wc_ref[...], preferred_element_type=F32)
    gm = jax.nn.sigmoid(gm_ref[...])
    merged = gm[:, 0:d] * y_a + gm[:, d:2 * d] * y_b + gm[:, 2 * d:3 * d] * y_c
    mix = jnp.dot(merged.astype(BF16), wo_ref[...], preferred_element_type=F32)
    o_ref[...] = _layer_norm(alpha * x_ref[...] + mix, g_ref[...], b_ref[...])


def _mix(x, o_c, o_s, o_w, g_a, dil_o, dil_lse, sgu, g_m, wa, wb, wc, wo, g, b, alpha):
    s, d = x.shape
    tm = min(256, s)
    assert s % tm == 0
    col = np.arange(N_BRANCH * NSA_Q_W)
    head, br = (col % NSA_Q_W) // HEAD_DIM, col // NSA_Q_W
    eg = jnp.asarray((np.arange(LANES)[:, None] == (head * N_BRANCH + br)[None, :]).astype(np.float32), BF16)
    row = lambda i: (i, 0)
    const = lambda i: (0, 0)
    rspec = lambda w: pl.BlockSpec((tm, w), row)
    cspec = lambda a: pl.BlockSpec(a.shape, const)
    widths = [d, NSA_Q_W, NSA_Q_W, NSA_Q_W, LANES] + [DIL_OUT_W] * 6 + [SGU_WIDTH, N_BRANCH * d]
    vmem = (2 * sum(_nbytes((tm, w), F32) for w in widths) + 2 * _nbytes((tm, d), F32)
            + 2 * sum(_nbytes(a.shape, BF16) for a in (eg, wa, wb, wc, wo)) + 10 * _nbytes((tm, N_BRANCH * d), F32))
    return pl.pallas_call(
        functools.partial(_mix_kernel, alpha=alpha),
        grid=(s // tm,),
        in_specs=[rspec(d), rspec(NSA_Q_W), rspec(NSA_Q_W), rspec(NSA_Q_W), rspec(LANES), cspec(eg)]
                 + [rspec(DIL_OUT_W)] * 6 + [rspec(SGU_WIDTH), rspec(N_BRANCH * d)]
                 + [cspec(wa), cspec(wb), cspec(wc), cspec(wo), pl.BlockSpec((1, d), const), pl.BlockSpec((1, d), const)],
        out_specs=rspec(d),
        out_shape=jax.ShapeDtypeStruct((s, d), F32),
        compiler_params=_cparams(("parallel",), vmem),
        name="mixer_merge_deepnorm",
    )(x, o_c, o_s, o_w, g_a, eg, *dil_o, *dil_lse, sgu, g_m, wa, wb, wc, wo, g.reshape(1, d), b.reshape(1, d))


def _rope_tables(s):
    inv = 1.0 / (ROPE_THETA ** (jnp.arange(0, HEAD_DIM, 2, dtype=F32) / HEAD_DIM))
    ang = jnp.arange(s, dtype=F32)[:, None] * inv[None, :]
    cos, sin = jnp.cos(ang), jnp.sin(ang)
    reps = LANES // HEAD_DIM
    return jnp.tile(jnp.concatenate([cos, cos], axis=1), (1, reps)), jnp.tile(jnp.concatenate([-sin, sin], axis=1), (1, reps))


def kernel(x, ln_g, ln_b, ffn1_gate, ffn1_up, ffn1_down, ffn2_gate, ffn2_up, ffn2_down, w_in, phi_k_pos, phi_k_w1, phi_k_w2, phi_v_pos, phi_v_w1, phi_v_w2, sgu_ln_g, sgu_ln_b, sgu_w, sgu_b, w_branch_a, w_branch_b, w_branch_c, w_out):
    bsz, s, d = x.shape
    depth = ln_g.shape[0]
    alpha = float((2 * depth) ** 0.25)
    q_scale = float(HEAD_DIM ** -0.5)
    tabs = _rope_tables(s)
    sizes = [NSA_Q_W] + [NSA_KV_W] * 6 + [N_BRANCH * NSA_HEADS, DIL_W, DIL_W, DIL_W, 2 * SGU_WIDTH, N_BRANCH * d]
    offs = np.concatenate([[0], np.cumsum(sizes)]).tolist()
    outs = []
    for bi in range(bsz):
        h = x[bi]
        for l in range(depth):
            wb16 = w_in[l].astype(BF16)
            (w_qa, w_kc, w_vc, w_ks, w_vs, w_kw, w_vw, w_ga, w_qb, w_kb, w_vb, w_uv, w_gm) = [
                wb16[:, offs[i]:offs[i + 1]] for i in range(len(sizes))]
            w_ga = jnp.pad(w_ga, ((0, 0), (0, LANES - w_ga.shape[1])))

            h, hb = _ffn(h, ffn1_gate[l].astype(BF16), ffn1_up[l].astype(BF16), ffn1_down[l].astype(BF16),
                         ln_g[l, 0], ln_b[l, 0], alpha)

            q_u = _mm(hb, w_qa, BF16, scale=q_scale)
            rot_a = _mm(hb, jnp.concatenate([w_qa * q_scale, w_ks, w_kw], axis=1), BF16, rope_tabs=tabs)
            qb = _mm(hb, w_qb, BF16, rope_tabs=tabs, scale=q_scale)
            kb = _mm(hb, w_kb, BF16, rope_tabs=tabs)
            kv_c = _mm(hb, jnp.concatenate([w_kc, w_vc], axis=1), F32)
            v_plain = _mm(hb, jnp.concatenate([w_vs, w_vw], axis=1), BF16)
            vb = _mm(hb, w_vb, BF16)
            g_a = _mm(hb, w_ga, F32)
            uv = _mm(hb, w_uv, F32)
            g_m = _mm(hb, w_gm, F32)

            kc = _compress(kv_c[:, :NSA_KV_W], phi_k_pos[l], phi_k_w1[l], phi_k_w2[l])
            vc = _compress(kv_c[:, NSA_KV_W:], phi_v_pos[l], phi_v_w1[l], phi_v_w2[l])
            o_c, nsel = _cmp_select(q_u, kc, vc)
            ks_col = NSA_Q_W // NSA_KV_W
            o_s = _sel_attn(rot_a, v_plain, nsel, ks_col, 0)
            o_w = _win_attn(rot_a, v_plain, ks_col + 1, 1)
            dil = [_dil_attn(qb, kb, vb, gi, win, dl) for gi, (win, dl) in enumerate(DIL_PAIRS)]
            sg = _sgu(uv, sgu_ln_g[l], sgu_ln_b[l], sgu_w[l], sgu_b[l])
            h = _mix(h, o_c, o_s, o_w, g_a, [o for o, _ in dil], [e for _, e in dil], sg, g_m,
                     w_branch_a[l].astype(BF16), w_branch_b[l].astype(BF16), w_branch_c[l].astype(BF16),
                     w_out[l].astype(BF16), ln_g[l, 1], ln_b[l, 1], alpha)

            h, _ = _ffn(h, ffn2_gate[l].astype(BF16), ffn2_up[l].astype(BF16), ffn2_down[l].astype(BF16),
                        ln_g[l, 2], ln_b[l, 2], alpha)
        outs.append(h)
    return jnp.stack(outs, axis=0)
```

```python
import functools

import numpy as np
import jax
import jax.numpy as jnp
from jax import lax
from jax.experimental import pallas as pl
from jax.experimental.pallas import tpu as pltpu

F32 = jnp.float32
BF16 = jnp.bfloat16

HEAD_DIM = 64
ROPE_THETA = 10000.0
LN_EPS = 1e-5
Q_BLOCK = 128
NSA_HEADS = 8
NSA_GROUPS = 2
NSA_HPG = NSA_HEADS // NSA_GROUPS
NSA_Q_W = NSA_HEADS * HEAD_DIM
NSA_KV_W = NSA_GROUPS * HEAD_DIM
CMP_LEN = 32
CMP_STRIDE = 16
SLC_BLOCK = 64
SLC_RATIO = SLC_BLOCK // CMP_STRIDE
SLC_TOPK = 16
SEL_TILE = 512
NSA_WINDOW = 512
FORCE_SCORE = 1e6
DIL_PAIRS = ((128, 1), (512, 4), (2048, 16))
DIL_HPG = 4
DIL_HEADS = DIL_HPG * 3
DIL_W = DIL_HEADS * HEAD_DIM
DIL_OUT_W = DIL_HPG * HEAD_DIM
SGU_CHUNK = 128
SGU_GROUPS = 4
SGU_GROUP_CH = 128
SGU_WIDTH = SGU_GROUPS * SGU_GROUP_CH
N_BRANCH = 3

LANES = 128
V7X_VMEM_BYTES = 64 * 1024 * 1024
VMEM_CAP = 56 * 1024 * 1024

M_INIT = -1e30
MASKED = -(2.0 ** 101)
LOG2E = 1.4426950408889634


def _cparams(sem, vmem_bytes):
    return pltpu.CompilerParams(dimension_semantics=sem,
                                vmem_limit_bytes=int(min(max(vmem_bytes, 16 * 1024 * 1024), VMEM_CAP)))


def _nbytes(shape, dtype):
    return int(np.prod(shape)) * jnp.dtype(dtype).itemsize


def _layer_norm(y, g, b):
    mu = jnp.mean(y, axis=-1, keepdims=True)
    d = y - mu
    var = jnp.mean(d * d, axis=-1, keepdims=True)
    return d * lax.rsqrt(var + LN_EPS) * g + b


def _lane_iota(shape):
    return lax.broadcasted_iota(jnp.int32, shape, len(shape) - 1)


def _row_iota(shape):
    return lax.broadcasted_iota(jnp.int32, shape, 0)


def _ffn_kernel(x_ref, wg_ref, wu_ref, wd_ref, g_ref, b_ref, o_ref, ob_ref, *, alpha, fc):
    x = x_ref[...]
    xb = x.astype(BF16)
    n_f = wg_ref.shape[1]
    acc = jnp.zeros(x.shape, F32)
    for c in range(n_f // fc):
        sl = slice(c * fc, (c + 1) * fc)
        gate = jnp.dot(xb, wg_ref[:, sl], preferred_element_type=F32)
        up = jnp.dot(xb, wu_ref[:, sl], preferred_element_type=F32)
        h = (gate * jax.nn.sigmoid(gate)) * up
        acc = acc + jnp.dot(h.astype(BF16), wd_ref[sl, :], preferred_element_type=F32)
    out = _layer_norm(alpha * x + 0.5 * acc, g_ref[...], b_ref[...])
    o_ref[...] = out
    ob_ref[...] = out.astype(BF16)


def _ffn(x, wg, wu, wd, g, b, alpha):
    s, d = x.shape
    n_f = wg.shape[1]
    tm = min(512, s)
    fc = 256
    assert s % tm == 0 and n_f % fc == 0
    const = lambda i: (0, 0)
    row = lambda i: (i, 0)
    vmem = (2 * 2 * _nbytes((tm, d), F32) + 2 * _nbytes((tm, d), BF16)
            + 2 * 3 * _nbytes((d, n_f), BF16) + 6 * _nbytes((tm, d), F32))
    return pl.pallas_call(
        functools.partial(_ffn_kernel, alpha=alpha, fc=fc),
        grid=(s // tm,),
        in_specs=[pl.BlockSpec((tm, d), row), pl.BlockSpec((d, n_f), const), pl.BlockSpec((d, n_f), const),
                  pl.BlockSpec((n_f, d), const), pl.BlockSpec((1, d), const), pl.BlockSpec((1, d), const)],
        out_specs=[pl.BlockSpec((tm, d), row), pl.BlockSpec((tm, d), row)],
        out_shape=[jax.ShapeDtypeStruct((s, d), F32), jax.ShapeDtypeStruct((s, d), BF16)],
        compiler_params=_cparams(("parallel",), vmem),
        name="ffn_deepnorm",
    )(x, wg, wu, wd, g.reshape(1, d), b.reshape(1, d))


def _swap_halves_64(a):
    half = HEAD_DIM // 2
    first = (_lane_iota(a.shape) % HEAD_DIM) < half
    return jnp.where(first, pltpu.roll(a, LANES - half, 1), pltpu.roll(a, half, 1))


def _mm_kernel(*refs, rope, scale):
    if rope:
        x_ref, w_ref, c_ref, s_ref, o_ref = refs
    else:
        x_ref, w_ref, o_ref = refs
    acc = jnp.dot(x_ref[...], w_ref[...], preferred_element_type=F32)
    if rope:
        cos = c_ref[...]
        sin = s_ref[...]
        for c in range(acc.shape[1] // LANES):
            a = acc[:, c * LANES:(c + 1) * LANES]
            r = a * cos + _swap_halves_64(a) * sin
            if scale != 1.0:
                r = r * scale
            o_ref[:, c * LANES:(c + 1) * LANES] = r.astype(o_ref.dtype)
    else:
        if scale != 1.0:
            acc = acc * scale
        o_ref[...] = acc.astype(o_ref.dtype)


def _mm(xb, w, out_dtype, *, rope_tabs=None, scale=1.0):
    s, k = xb.shape
    n = w.shape[1]
    tm = min(1024, s)
    tn = 512 if n % 512 == 0 else (256 if n % 256 == 0 else 128)
    assert s % tm == 0 and n % tn == 0
    in_specs = [pl.BlockSpec((tm, k), lambda i, j: (i, 0)), pl.BlockSpec((k, tn), lambda i, j: (0, j))]
    args = [xb, w]
    if rope_tabs is not None:
        in_specs += [pl.BlockSpec((tm, LANES), lambda i, j: (i, 0))] * 2
        args += list(rope_tabs)
    vmem = 2 * (_nbytes((tm, k), BF16) + _nbytes((k, tn), BF16) + _nbytes((tm, tn), out_dtype)
                + 2 * _nbytes((tm, LANES), F32)) + 3 * _nbytes((tm, tn), F32)
    return pl.pallas_call(
        functools.partial(_mm_kernel, rope=rope_tabs is not None, scale=scale),
        grid=(s // tm, n // tn),
        in_specs=in_specs,
        out_specs=pl.BlockSpec((tm, tn), lambda i, j: (i, j)),
        out_shape=jax.ShapeDtypeStruct((s, n), out_dtype),
        compiler_params=_cparams(("parallel", "parallel"), vmem),
        name="proj_rope" if rope_tabs is not None else "proj",
    )(*args)


def _compress_kernel(a_ref, pa_ref, pb_ref, w1a_ref, w1b_ref, w2_ref, o_ref):
    n_cmp = a_ref.shape[1]
    pb = pb_ref[...]
    tail = jnp.dot(jnp.broadcast_to(pb, (8, pb.shape[1])).astype(BF16), w1b_ref[...],
                   preferred_element_type=F32)[0:1, :]
    last = _row_iota((n_cmp, 1)) == n_cmp - 1
    outs = []
    for g in range(NSA_GROUPS):
        a = a_ref[g]
        ha = jnp.dot((a + pa_ref[...]).astype(BF16), w1a_ref[...], preferred_element_type=F32)
        hb = jnp.dot((a + pb).astype(BF16), w1b_ref[...], preferred_element_type=F32)
        hb_next = jnp.where(last, tail, pltpu.roll(hb, n_cmp - 1, 0))
        h = jax.nn.gelu(ha + hb_next)
        outs.append(jnp.dot(h.astype(BF16), w2_ref[...], preferred_element_type=F32))
    lo = _lane_iota(outs[0].shape) < HEAD_DIM
    o_ref[...] = jnp.where(lo, outs[0], outs[1]).astype(o_ref.dtype)


def _compress(x, pos, w1, w2):
    s = x.shape[0]
    n_cmp = s // CMP_STRIDE
    half_w = CMP_STRIDE * HEAD_DIM
    a = x.reshape(n_cmp, CMP_STRIDE, NSA_GROUPS, HEAD_DIM).transpose(2, 0, 1, 3).reshape(NSA_GROUPS, n_cmp, half_w)
    posf = pos.reshape(1, CMP_LEN * HEAD_DIM)
    w1b16 = w1.astype(BF16)
    w2d = jnp.concatenate([w2, w2], axis=1).astype(BF16)
    hidden = w1.shape[1]
    vmem = 2 * (_nbytes(a.shape, F32) + 2 * _nbytes((half_w, hidden), BF16)) + 6 * _nbytes((n_cmp, half_w), F32)
    return pl.pallas_call(
        _compress_kernel,
        out_shape=jax.ShapeDtypeStruct((n_cmp, NSA_KV_W), BF16),
        compiler_params=pltpu.CompilerParams(vmem_limit_bytes=int(min(max(vmem, 16 << 20), VMEM_CAP))),
        name="nsa_compress",
    )(a, posf[:, :half_w], posf[:, half_w:], w1b16[:half_w], w1b16[half_w:], w2d)


def _stack_group_queries(q_ref, col0, kv_lane_group):
    parts = []
    for h in range(NSA_HPG):
        c = col0 + (h // 2) * LANES
        qc = q_ref[:, c:c + LANES].astype(F32)
        e = h % 2
        lane = _lane_iota(qc.shape)
        mine = (lane >= e * HEAD_DIM) & (lane < (e + 1) * HEAD_DIM)
        qh = jnp.where(mine, qc, 0.0)
        if e != kv_lane_group:
            qh = pltpu.roll(qh, HEAD_DIM, 1)
        parts.append(qh.astype(BF16))
    return jnp.concatenate(parts, axis=0)


def _unstack_group_outputs(o, kv_lane_group):
    chunks = []
    for c in range(NSA_HPG // 2):
        halves = []
        for e in range(2):
            h = 2 * c + e
            oh = o[h * Q_BLOCK:(h + 1) * Q_BLOCK, :]
            if e != kv_lane_group:
                oh = pltpu.roll(oh, HEAD_DIM, 1)
            halves.append(oh)
        lo = _lane_iota(halves[0].shape) < HEAD_DIM
        chunks.append(jnp.where(lo, halves[0], halves[1]))
    return jnp.concatenate(chunks, axis=1)


def _qk(q, k):
    return lax.dot_general(q, k, (((1,), (1,)), ((), ())), preferred_element_type=F32)


def _softmax_rows(s, mask, base2=False):
    sm = jnp.where(mask, s, MASKED)
    m = jnp.maximum(jnp.max(sm, axis=-1, keepdims=True), M_INIT)
    e = jnp.exp2(sm - m) if base2 else jnp.exp(sm - m)
    return e, jnp.sum(e, axis=-1, keepdims=True), m


def _cmp_select_kernel(q_ref, kc_ref, vc_ref, mmap_ref, oc_ref, nsel_ref, *, top_k):
    b = pl.program_id(0)
    n_cmp = kc_ref.shape[0]
    n_slc = mmap_ref.shape[1]
    rows = NSA_HPG * Q_BLOCK
    t4 = b * Q_BLOCK + (_row_iota((rows, 1)) % Q_BLOCK)
    c_end = _lane_iota((1, n_cmp)) * CMP_STRIDE + (CMP_LEN - 1)
    valid = c_end <= t4
    t = b * Q_BLOCK + _row_iota((Q_BLOCK, 1))
    blk = _lane_iota((1, n_slc))
    cur = t // SLC_BLOCK
    forced = (blk == 0) | (blk == cur) | (blk == cur - 1)
    future = blk * SLC_BLOCK > t
    blk_f = blk.astype(F32)
    kc = kc_ref[...]
    vc = vc_ref[...]
    mmap = mmap_ref[...]
    for g in range(NSA_GROUPS):
        qst = _stack_group_queries(q_ref, g * NSA_HPG * HEAD_DIM, g)
        e, den, _ = _softmax_rows(_qk(qst, kc), valid)
        p = e / jnp.maximum(den, 1e-30)
        o = jnp.dot(p.astype(BF16), vc, preferred_element_type=F32)
        oc_ref[:, g * NSA_HPG * HEAD_DIM:(g + 1) * NSA_HPG * HEAD_DIM] = _unstack_group_outputs(o, g)
        psum = p[0:Q_BLOCK]
        for h in range(1, NSA_HPG):
            psum = psum + p[h * Q_BLOCK:(h + 1) * Q_BLOCK]
        hi = psum.astype(BF16)
        r1 = psum - hi.astype(F32)
        mid = r1.astype(BF16)
        low = (r1 - mid.astype(F32)).astype(BF16)
        p_slc = (jnp.dot(hi, mmap, preferred_element_type=F32) + jnp.dot(mid, mmap, preferred_element_type=F32)
                 + jnp.dot(low, mmap, preferred_element_type=F32))
        score = jnp.where(forced, FORCE_SCORE, jnp.where(future, -1.0, p_slc))
        not_picked = jnp.ones(score.shape, F32)
        for _ in range(top_k):
            mx = jnp.max(score, axis=-1, keepdims=True)
            first = jnp.min(jnp.where(score == mx, blk_f, float(n_slc)), axis=-1, keepdims=True)
            hit = blk_f == first
            not_picked = jnp.where(hit, 0.0, not_picked)
            score = jnp.where(hit, -2.0, score)
        nsel_ref[:, g * n_slc:(g + 1) * n_slc] = not_picked.astype(nsel_ref.dtype)


def _cmp_select(q_u, kc, vc):
    s = q_u.shape[0]
    n_cmp, n_slc = s // CMP_STRIDE, s // SLC_BLOCK
    top_k = min(SLC_TOPK, n_slc)
    m = np.arange(n_cmp)[:, None]
    j = np.arange(n_slc)[None, :]
    mmap = jnp.asarray(((m >= SLC_RATIO * j - 1) & (m <= SLC_RATIO * j + SLC_RATIO - 1)).astype(np.float32), BF16)
    const = lambda b: (0, 0)
    rows = NSA_HPG * Q_BLOCK
    vmem = 2 * (2 * _nbytes((n_cmp, NSA_KV_W), BF16) + _nbytes((n_cmp, n_slc), BF16)
                + _nbytes((Q_BLOCK, NSA_Q_W), F32) * 2) + 8 * _nbytes((rows, n_cmp), F32)
    return pl.pallas_call(
        functools.partial(_cmp_select_kernel, top_k=top_k),
        grid=(s // Q_BLOCK,),
        in_specs=[pl.BlockSpec((Q_BLOCK, NSA_Q_W), lambda b: (b, 0)), pl.BlockSpec((n_cmp, NSA_KV_W), const),
                  pl.BlockSpec((n_cmp, NSA_KV_W), const), pl.BlockSpec((n_cmp, n_slc), const)],
        out_specs=[pl.BlockSpec((Q_BLOCK, NSA_Q_W), lambda b: (b, 0)),
                   pl.BlockSpec((Q_BLOCK, NSA_GROUPS * n_slc), lambda b: (b, 0))],
        out_shape=[jax.ShapeDtypeStruct((s, NSA_Q_W), F32), jax.ShapeDtypeStruct((s, NSA_GROUPS * n_slc), BF16)],
        compiler_params=_cparams(("parallel",), vmem),
        name="nsa_compressed_select",
    )(q_u, kc, vc, mmap)


def _sel_attn_kernel(q_ref, k_ref, v_ref, nsel_ref, o_ref, qa_ref, m_ref, acc_ref, *, win):
    b = pl.program_id(0)
    n_slc = nsel_ref.shape[1] // NSA_GROUPS
    n_win = n_slc // win
    tiles_per_win = win * SLC_BLOCK // SEL_TILE
    gw = NSA_HPG * HEAD_DIM
    n_chunks = SEL_TILE // LANES
    rows = NSA_HPG * Q_BLOCK
    last = (b * Q_BLOCK) // SEL_TILE
    t = b * Q_BLOCK + _row_iota((Q_BLOCK, 1))
    lane = _lane_iota((Q_BLOCK, LANES))
    head_lanes = lane < HEAD_DIM
    flag_lanes = (lane >= HEAD_DIM) & (lane < HEAD_DIM + win)

    nsel_all = nsel_ref[...].astype(F32)
    if nsel_all.shape[1] < LANES:
        nsel_all = jnp.concatenate([nsel_all, jnp.zeros((Q_BLOCK, LANES - nsel_all.shape[1]), F32)], axis=1)
    for g in range(NSA_GROUPS):
        heads = []
        for h in range(NSA_HPG):
            c = g * gw + (h // 2) * LANES
            qc = q_ref[:, c:c + LANES].astype(F32)
            heads.append(qc if h % 2 == 0 else pltpu.roll(qc, HEAD_DIM, 1))
        for w in range(n_win):
            a = g * n_slc + w * win
            fl = nsel_all[:, (a // LANES) * LANES:(a // LANES + 1) * LANES]
            shift = (HEAD_DIM - a % LANES) % LANES
            if shift:
                fl = pltpu.roll(fl, shift, 1)
            fl = jnp.where(flag_lanes, fl, 0.0)
            for h in range(NSA_HPG):
                qa_ref[g, w, h * Q_BLOCK:(h + 1) * Q_BLOCK, :] = jnp.where(head_lanes, heads[h], fl).astype(BF16)
    m_ref[...] = jnp.full(m_ref.shape, M_INIT, F32)
    acc_ref[...] = jnp.zeros(acc_ref.shape, F32)

    def tile(g, kt, causal):
        off = pl.multiple_of(kt * SEL_TILE, SEL_TILE)
        k = k_ref[g, pl.ds(off, SEL_TILE), :]
        v = v_ref[g, pl.ds(off, SEL_TILE), :]
        s = _qk(qa_ref[g, kt // tiles_per_win], k)
        if causal:
            keep = off + _lane_iota((1, SEL_TILE)) <= t
            s = jnp.where(jnp.concatenate([keep] * NSA_HPG, axis=0), s, MASKED)
        m_old = m_ref[g]
        m_new = jnp.maximum(m_old, jnp.max(s, axis=-1, keepdims=True))
        alpha = jnp.exp2(m_old - m_new)
        p = jnp.concatenate([jnp.exp2(s[:, c * LANES:(c + 1) * LANES] - m_new).astype(BF16)
                             for c in range(n_chunks)], axis=1)
        acc_ref[g] = alpha * acc_ref[g] + jnp.dot(p, v, preferred_element_type=F32)
        m_ref[g] = m_new

    def body(kt, carry):
        for g in range(NSA_GROUPS):
            tile(g, kt, False)
        return carry

    lax.fori_loop(0, last, body, 0)
    for g in range(NSA_GROUPS):
        tile(g, last, True)
        acc = acc_ref[g]
        o = acc / jnp.maximum(acc[:, HEAD_DIM:HEAD_DIM + 1], 1e-30)
        for c in range(NSA_HPG // 2):
            even = o[(2 * c) * Q_BLOCK:(2 * c + 1) * Q_BLOCK, :]
            odd = pltpu.roll(o[(2 * c + 1) * Q_BLOCK:(2 * c + 2) * Q_BLOCK, :], HEAD_DIM, 1)
            o_ref[:, g * gw + c * LANES:g * gw + (c + 1) * LANES] = jnp.where(head_lanes, even, odd)


def _sel_attn(rot_a, k_s, v_s, nsel):
    s = rot_a.shape[0]
    n_slc = s // SLC_BLOCK
    win = min(n_slc, LANES - HEAD_DIM)
    assert s % SEL_TILE == 0 and n_slc % win == 0 and (win * SLC_BLOCK) % SEL_TILE == 0
    n_win = n_slc // win
    spare = LANES - HEAD_DIM
    blk = (np.arange(s) // SLC_BLOCK) % win
    marks = jnp.asarray(np.where(np.arange(spare)[None, :] == blk[:, None], MASKED, 0.0).astype(np.float32), BF16)
    ones = jnp.asarray((np.arange(spare)[None, :] == 0).astype(np.float32) * np.ones((s, 1), np.float32), BF16)
    grp = lambda a, g: a[:, g * HEAD_DIM:(g + 1) * HEAD_DIM]
    k_aug = jnp.stack([jnp.concatenate([grp(k_s, g), marks], axis=1) for g in range(NSA_GROUPS)])
    v_aug = jnp.stack([jnp.concatenate([grp(v_s, g), ones], axis=1) for g in range(NSA_GROUPS)])
    rows = NSA_HPG * Q_BLOCK
    vmem = (2 * (2 * _nbytes(k_aug.shape, BF16) + _nbytes((Q_BLOCK, NSA_Q_W), BF16)
                 + _nbytes((Q_BLOCK, NSA_GROUPS * n_slc), BF16) + _nbytes((Q_BLOCK, NSA_Q_W), F32))
            + NSA_GROUPS * (n_win * _nbytes((rows, LANES), BF16) + 2 * _nbytes((rows, LANES), F32))
            + 2 * 4 * _nbytes((rows, SEL_TILE), F32))
    return pl.pallas_call(
        functools.partial(_sel_attn_kernel, win=win),
        grid=(s // Q_BLOCK,),
        in_specs=[pl.BlockSpec((Q_BLOCK, NSA_Q_W), lambda b: (b, 0)),
                  pl.BlockSpec(k_aug.shape, lambda b: (0, 0, 0)),
                  pl.BlockSpec(v_aug.shape, lambda b: (0, 0, 0)),
                  pl.BlockSpec((Q_BLOCK, NSA_GROUPS * n_slc), lambda b: (b, 0))],
        out_specs=pl.BlockSpec((Q_BLOCK, NSA_Q_W), lambda b: (b, 0)),
        out_shape=jax.ShapeDtypeStruct((s, NSA_Q_W), F32),
        scratch_shapes=[pltpu.VMEM((NSA_GROUPS, n_win, rows, LANES), BF16),
                        pltpu.VMEM((NSA_GROUPS, rows, LANES), F32), pltpu.VMEM((NSA_GROUPS, rows, LANES), F32)],
        compiler_params=_cparams(("parallel",), vmem),
        name="nsa_selected",
    )(rot_a, k_aug, v_aug, nsel)


def _win_attn_kernel(q_ref, k_ref, v_ref, o_ref, *, span):
    b = pl.program_id(0)
    rows = NSA_HPG * Q_BLOCK
    n_back = NSA_WINDOW // Q_BLOCK
    start = pl.multiple_of(jnp.maximum(b - n_back, 0) * Q_BLOCK, Q_BLOCK)
    k = k_ref[pl.ds(start, span), :]
    v = v_ref[pl.ds(start, span), :]
    t4 = b * Q_BLOCK + (_row_iota((rows, 1)) % Q_BLOCK)
    diff = t4 - (start + _lane_iota((1, span)))
    mask = (diff >= 0) & (diff < NSA_WINDOW)
    for g in range(NSA_GROUPS):
        qst = _stack_group_queries(q_ref, g * NSA_HPG * HEAD_DIM, g)
        e, den, _ = _softmax_rows(_qk(qst, k), mask, base2=True)
        o = jnp.dot(e.astype(BF16), v, preferred_element_type=F32) / jnp.maximum(den, 1e-30)
        o_ref[:, g * NSA_HPG * HEAD_DIM:(g + 1) * NSA_HPG * HEAD_DIM] = _unstack_group_outputs(o, g)


def _win_attn(q_rot, k_rot, v_plain, kw_col, vw_col):
    s = q_rot.shape[0]
    span = min(NSA_WINDOW + Q_BLOCK, s)
    rows = NSA_HPG * Q_BLOCK
    vmem = 2 * (2 * _nbytes((s, NSA_KV_W), BF16) + _nbytes((Q_BLOCK, NSA_Q_W), BF16)
                + _nbytes((Q_BLOCK, NSA_Q_W), F32)) + 8 * _nbytes((rows, span), F32)
    return pl.pallas_call(
        functools.partial(_win_attn_kernel, span=span),
        grid=(s // Q_BLOCK,),
        in_specs=[pl.BlockSpec((Q_BLOCK, NSA_Q_W), lambda b: (b, 0)),
                  pl.BlockSpec((s, NSA_KV_W), lambda b: (0, kw_col)),
                  pl.BlockSpec((s, NSA_KV_W), lambda b: (0, vw_col))],
        out_specs=pl.BlockSpec((Q_BLOCK, NSA_Q_W), lambda b: (b, 0)),
        out_shape=jax.ShapeDtypeStruct((s, NSA_Q_W), F32),
        compiler_params=_cparams(("parallel",), vmem),
        name="nsa_window",
    )(q_rot, k_rot, v_plain)


def _dil_attn_kernel(q_ref, kp_ref, kc_ref, vp_ref, vc_ref, o_ref, lse_ref, *, span, dil):
    n = pl.program_id(0)
    qi = _row_iota((Q_BLOCK, 1)) + Q_BLOCK
    ki = _lane_iota((1, 2 * Q_BLOCK))
    delta = qi - ki
    mask = (delta >= 0) & (delta <= span) & ((n > 0) | (ki >= Q_BLOCK))
    lo = _lane_iota((Q_BLOCK, LANES)) < HEAD_DIM

    def residue(r, carry):
        rows = pl.ds(r, Q_BLOCK, stride=dil) if dil > 1 else pl.ds(0, Q_BLOCK)
        q = q_ref[rows, :]
        k = jnp.concatenate([kp_ref[rows, :], kc_ref[rows, :]], axis=0).astype(BF16)
        v = jnp.concatenate([vp_ref[rows, :], vc_ref[rows, :]], axis=0).astype(BF16)
        outs, lses = [], []
        for e in range(2):
            qh = jnp.where(lo if e == 0 else ~lo, q, 0.0).astype(BF16)
            ex, den, m = _softmax_rows(_qk(qh, k), mask)
            den = jnp.maximum(den, 1e-30)
            outs.append(jnp.dot(ex.astype(BF16), v, preferred_element_type=F32) / den)
            lses.append(jnp.broadcast_to(m + jnp.log(den), (Q_BLOCK, LANES)))
        o_ref[rows, :] = jnp.where(lo, outs[0], outs[1])
        lse_ref[rows, :] = jnp.where(lo, lses[0], lses[1])
        return carry

    if dil == 1:
        residue(0, 0)
    else:
        lax.fori_loop(0, dil, residue, 0)


def _dil_attn(qb, kb, vb, gidx, win, dil):
    s = qb.shape[0]
    tb = dil * Q_BLOCK
    assert s % tb == 0
    span = win // dil
    n_chunks = DIL_OUT_W // LANES
    cur = lambda n, c: (n, gidx * n_chunks + c)
    prev = lambda n, c: (jnp.maximum(n - 1, 0), gidx * n_chunks + c)
    blk = (tb, LANES)
    vmem = 2 * 7 * _nbytes(blk, F32) + 16 * _nbytes((Q_BLOCK, 2 * Q_BLOCK), F32)
    return pl.pallas_call(
        functools.partial(_dil_attn_kernel, span=span, dil=dil),
        grid=(s // tb, n_chunks),
        in_specs=[pl.BlockSpec(blk, cur), pl.BlockSpec(blk, prev), pl.BlockSpec(blk, cur),
                  pl.BlockSpec(blk, prev), pl.BlockSpec(blk, cur)],
        out_specs=[pl.BlockSpec(blk, lambda n, c: (n, c)), pl.BlockSpec(blk, lambda n, c: (n, c))],
        out_shape=[jax.ShapeDtypeStruct((s, DIL_OUT_W), F32)] * 2,
        compiler_params=_cparams(("parallel", "parallel"), vmem),
        name=f"dilated_w{win}_d{dil}",
    )(qb, kb, kb, vb, vb)


def _sgu_kernel(uv_ref, g_ref, b_ref, ws_ref, bs_ref, o_ref):
    uv = jax.nn.gelu(uv_ref[...])
    u = uv[:, :SGU_WIDTH]
    v = _layer_norm(uv[:, SGU_WIDTH:], g_ref[...], b_ref[...]).astype(BF16)
    causal = _lane_iota((SGU_CHUNK, SGU_CHUNK)) <= _row_iota((SGU_CHUNK, SGU_CHUNK))
    bs = bs_ref[...]
    for n in range(uv.shape[0] // SGU_CHUNK):
        rs = slice(n * SGU_CHUNK, (n + 1) * SGU_CHUNK)
        for g in range(SGU_GROUPS):
            cs = slice(g * SGU_GROUP_CH, (g + 1) * SGU_GROUP_CH)
            ws = jnp.where(causal, ws_ref[g], 0.0).astype(BF16)
            sv = jnp.dot(ws, v[rs, cs], preferred_element_type=F32) + bs[:, cs]
            o_ref[rs, cs] = (u[rs, cs] * sv).astype(o_ref.dtype)


def _sgu(uv, ln_g, ln_b, w_s, b_s):
    s = uv.shape[0]
    tm = min(4 * SGU_CHUNK, s)
    assert s % tm == 0
    bs = jnp.repeat(b_s.T, SGU_GROUP_CH, axis=1)
    c2 = lambda i: (0, 0)
    vmem = 2 * (_nbytes((tm, 2 * SGU_WIDTH), F32) + _nbytes((tm, SGU_WIDTH), F32)) + 6 * _nbytes((tm, 2 * SGU_WIDTH), F32)
    return pl.pallas_call(
        _sgu_kernel,
        grid=(s // tm,),
        in_specs=[pl.BlockSpec((tm, 2 * SGU_WIDTH), lambda i: (i, 0)), pl.BlockSpec((1, SGU_WIDTH), c2),
                  pl.BlockSpec((1, SGU_WIDTH), c2), pl.BlockSpec(w_s.shape, lambda i: (0, 0, 0)),
                  pl.BlockSpec((SGU_CHUNK, SGU_WIDTH), c2)],
        out_specs=pl.BlockSpec((tm, SGU_WIDTH), lambda i: (i, 0)),
        out_shape=jax.ShapeDtypeStruct((s, SGU_WIDTH), F32),
        compiler_params=_cparams(("parallel",), vmem),
        name="sgu",
    )(uv, ln_g.reshape(1, -1), ln_b.reshape(1, -1), w_s, bs)


def _split_dot(x, w):
    hi = x.astype(BF16)
    lo = (x - hi.astype(F32)).astype(BF16)
    return jnp.dot(hi, w, preferred_element_type=F32) + jnp.dot(lo, w, preferred_element_type=F32)


def _mix_kernel(x_ref, oc_ref, os_ref, ow_ref, ga_ref, eg_ref, d0_ref, d1_ref, d2_ref, l0_ref, l1_ref, l2_ref,
                sg_ref, gm_ref, wa_ref, wb_ref, wc_ref, wo_ref, g_ref, b_ref, o_ref, *, alpha):
    d = x_ref.shape[1]
    gates = _split_dot(jax.nn.sigmoid(ga_ref[...]), eg_ref[...])
    o_a = (gates[:, 0:NSA_Q_W] * oc_ref[...] + gates[:, NSA_Q_W:2 * NSA_Q_W] * os_ref[...]
           + gates[:, 2 * NSA_Q_W:3 * NSA_Q_W] * ow_ref[...])
    y_a = jnp.dot(o_a.astype(BF16), wa_ref[...], preferred_element_type=F32)
    l0, l1, l2 = l0_ref[...], l1_ref[...], l2_ref[...]
    lm = jnp.maximum(jnp.maximum(l0, l1), l2)
    e0, e1, e2 = jnp.exp(l0 - lm), jnp.exp(l1 - lm), jnp.exp(l2 - lm)
    den = e0 + e1 + e2
    o_b = (e0 / den) * d0_ref[...] + (e1 / den) * d1_ref[...] + (e2 / den) * d2_ref[...]
    y_b = jnp.dot(o_b.astype(BF16), wb_ref[...], preferred_element_type=F32)
    y_c = jnp.dot(sg_ref[...].astype(BF16), wc_ref[...], preferred_element_type=F32)
    gm = jax.nn.sigmoid(gm_ref[...])
    merged = gm[:, 0:d] * y_a + gm[:, d:2 * d] * y_b + gm[:, 2 * d:3 * d] * y_c
    mix = jnp.dot(merged.astype(BF16), wo_ref[...], preferred_element_type=F32)
    o_ref[...] = _layer_norm(alpha * x_ref[...] + mix, g_ref[...], b_ref[...])


def _mix(x, o_c, o_s, o_w, g_a, dil_o, dil_lse, sgu, g_m, wa, wb, wc, wo, g, b, alpha):
    s, d = x.shape
    tm = min(256, s)
    assert s % tm == 0
    col = np.arange(N_BRANCH * NSA_Q_W)
    head, br = (col % NSA_Q_W) // HEAD_DIM, col // NSA_Q_W
    eg = jnp.asarray((np.arange(LANES)[:, None] == (head * N_BRANCH + br)[None, :]).astype(np.float32), BF16)
    row = lambda i: (i, 0)
    const = lambda i: (0, 0)
    rspec = lambda w: pl.BlockSpec((tm, w), row)
    cspec = lambda a: pl.BlockSpec(a.shape, const)
    widths = [d, NSA_Q_W, NSA_Q_W, NSA_Q_W, LANES] + [DIL_OUT_W] * 6 + [SGU_WIDTH, N_BRANCH * d]
    vmem = (2 * sum(_nbytes((tm, w), F32) for w in widths) + 2 * _nbytes((tm, d), F32)
            + 2 * sum(_nbytes(a.shape, BF16) for a in (eg, wa, wb, wc, wo)) + 10 * _nbytes((tm, N_BRANCH * d), F32))
    return pl.pallas_call(
        functools.partial(_mix_kernel, alpha=alpha),
        grid=(s // tm,),
        in_specs=[rspec(d), rspec(NSA_Q_W), rspec(NSA_Q_W), rspec(NSA_Q_W), rspec(LANES), cspec(eg)]
                 + [rspec(DIL_OUT_W)] * 6 + [rspec(SGU_WIDTH), rspec(N_BRANCH * d)]
                 + [cspec(wa), cspec(wb), cspec(wc), cspec(wo), pl.BlockSpec((1, d), const), pl.BlockSpec((1, d), const)],
        out_specs=rspec(d),
        out_shape=jax.ShapeDtypeStruct((s, d), F32),
        compiler_params=_cparams(("parallel",), vmem),
        name="mixer_merge_deepnorm",
    )(x, o_c, o_s, o_w, g_a, eg, *dil_o, *dil_lse, sgu, g_m, wa, wb, wc, wo, g.reshape(1, d), b.reshape(1, d))


def _rope_tables(s):
    inv = 1.0 / (ROPE_THETA ** (jnp.arange(0, HEAD_DIM, 2, dtype=F32) / HEAD_DIM))
    ang = jnp.arange(s, dtype=F32)[:, None] * inv[None, :]
    cos, sin = jnp.cos(ang), jnp.sin(ang)
    reps = LANES // HEAD_DIM
    return jnp.tile(jnp.concatenate([cos, cos], axis=1), (1, reps)), jnp.tile(jnp.concatenate([-sin, sin], axis=1), (1, reps))


def kernel(x, ln_g, ln_b, ffn1_gate, ffn1_up, ffn1_down, ffn2_gate, ffn2_up, ffn2_down, w_in, phi_k_pos, phi_k_w1, phi_k_w2, phi_v_pos, phi_v_w1, phi_v_w2, sgu_ln_g, sgu_ln_b, sgu_w, sgu_b, w_branch_a, w_branch_b, w_branch_c, w_out):
    bsz, s, d = x.shape
    depth = ln_g.shape[0]
    alpha = float((2 * depth) ** 0.25)
    q_scale = float(HEAD_DIM ** -0.5)
    tabs = _rope_tables(s)
    sizes = [NSA_Q_W] + [NSA_KV_W] * 6 + [N_BRANCH * NSA_HEADS, DIL_W, DIL_W, DIL_W, 2 * SGU_WIDTH, N_BRANCH * d]
    offs = np.concatenate([[0], np.cumsum(sizes)]).tolist()
    outs = []
    for bi in range(bsz):
        h = x[bi]
        for l in range(depth):
            wb16 = w_in[l].astype(BF16)
            (w_qa, w_kc, w_vc, w_ks, w_vs, w_kw, w_vw, w_ga, w_qb, w_kb, w_vb, w_uv, w_gm) = [
                wb16[:, offs[i]:offs[i + 1]] for i in range(len(sizes))]
            w_ga = jnp.pad(w_ga, ((0, 0), (0, LANES - w_ga.shape[1])))

            h, hb = _ffn(h, ffn1_gate[l].astype(BF16), ffn1_up[l].astype(BF16), ffn1_down[l].astype(BF16),
                         ln_g[l, 0], ln_b[l, 0], alpha)

            q_u = _mm(hb, w_qa, BF16, scale=q_scale)
            q_rot = _mm(hb, w_qa, BF16, rope_tabs=tabs, scale=q_scale * LOG2E)
            k_rot = _mm(hb, jnp.concatenate([w_ks, w_kw], axis=1), BF16, rope_tabs=tabs)
            qb = _mm(hb, w_qb, F32, rope_tabs=tabs, scale=q_scale)
            kb = _mm(hb, w_kb, F32, rope_tabs=tabs)
            kv_c = _mm(hb, jnp.concatenate([w_kc, w_vc], axis=1), F32)
            v_plain = _mm(hb, jnp.concatenate([w_vs, w_vw], axis=1), BF16)
            vb = _mm(hb, w_vb, F32)
            g_a = _mm(hb, w_ga, F32)
            uv = _mm(hb, w_uv, F32)
            g_m = _mm(hb, w_gm, F32)

            kc = _compress(kv_c[:, :NSA_KV_W], phi_k_pos[l], phi_k_w1[l], phi_k_w2[l])
            vc = _compress(kv_c[:, NSA_KV_W:], phi_v_pos[l], phi_v_w1[l], phi_v_w2[l])
            o_c, nsel = _cmp_select(q_u, kc, vc)
            o_s = _sel_attn(q_rot, k_rot[:, :NSA_KV_W], v_plain[:, :NSA_KV_W], nsel)
            o_w = _win_attn(q_rot, k_rot, v_plain, 1, 1)
            dil = [_dil_attn(qb, kb, vb, gi, win, dl) for gi, (win, dl) in enumerate(DIL_PAIRS)]
            sg = _sgu(uv, sgu_ln_g[l], sgu_ln_b[l], sgu_w[l], sgu_b[l])
            h = _mix(h, o_c, o_s, o_w, g_a, [o for o, _ in dil], [e for _, e in dil], sg, g_m,
                     w_branch_a[l].astype(BF16), w_branch_b[l].astype(BF16), w_branch_c[l].astype(BF16),
                     w_out[l].astype(BF16), ln_g[l, 1], ln_b[l, 1], alpha)

            h, _ = _ffn(h, ffn2_gate[l].astype(BF16), ffn2_up[l].astype(BF16), ffn2_down[l].astype(BF16),
                        ln_g[l, 2], ln_b[l, 2], alpha)
        outs.append(h)
    return jnp.stack(outs, axis=0)
```

```python
import functools

import numpy as np
import jax
import jax.numpy as jnp
from jax import lax
from jax.experimental import pallas as pl
from jax.experimental.pallas import tpu as pltpu

F32 = jnp.float32
BF16 = jnp.bfloat16

HEAD_DIM = 64
ROPE_THETA = 10000.0
LN_EPS = 1e-5
Q_BLOCK = 128
NSA_HEADS = 8
NSA_GROUPS = 2
NSA_HPG = NSA_HEADS // NSA_GROUPS
NSA_Q_W = NSA_HEADS * HEAD_DIM
NSA_KV_W = NSA_GROUPS * HEAD_DIM
CMP_LEN = 32
CMP_STRIDE = 16
SLC_BLOCK = 64
SLC_RATIO = SLC_BLOCK // CMP_STRIDE
SLC_TOPK = 16
DIL_STEP_TOKENS = 2048
SEL_TILE = 1024
NSA_WINDOW = 512
FORCE_SCORE = 1e6
DIL_PAIRS = ((128, 1), (512, 4), (2048, 16))
DIL_HPG = 4
DIL_HEADS = DIL_HPG * 3
DIL_W = DIL_HEADS * HEAD_DIM
DIL_OUT_W = DIL_HPG * HEAD_DIM
SGU_CHUNK = 128
SGU_GROUPS = 4
SGU_GROUP_CH = 128
SGU_WIDTH = SGU_GROUPS * SGU_GROUP_CH
N_BRANCH = 3

LANES = 128
V7X_VMEM_BYTES = 64 * 1024 * 1024
VMEM_CAP = 56 * 1024 * 1024

M_INIT = -1e30
MASKED = -(2.0 ** 101)
LOG2E = 1.4426950408889634


def _cparams(sem, vmem_bytes):
    return pltpu.CompilerParams(dimension_semantics=sem,
                                vmem_limit_bytes=int(min(max(vmem_bytes, 16 * 1024 * 1024), VMEM_CAP)))


def _nbytes(shape, dtype):
    return int(np.prod(shape)) * jnp.dtype(dtype).itemsize


def _layer_norm(y, g, b):
    mu = jnp.mean(y, axis=-1, keepdims=True)
    d = y - mu
    var = jnp.mean(d * d, axis=-1, keepdims=True)
    return d * lax.rsqrt(var + LN_EPS) * g + b


def _lane_iota(shape):
    return lax.broadcasted_iota(jnp.int32, shape, len(shape) - 1)


def _row_iota(shape):
    return lax.broadcasted_iota(jnp.int32, shape, 0)


def _ffn_kernel(x_ref, wg_ref, wu_ref, wd_ref, g_ref, b_ref, o_ref, ob_ref, *, alpha, fc):
    x = x_ref[...]
    xb = x.astype(BF16)
    n_f = wg_ref.shape[1]
    acc = jnp.zeros(x.shape, F32)
    for c in range(n_f // fc):
        sl = slice(c * fc, (c + 1) * fc)
        gate = jnp.dot(xb, wg_ref[:, sl], preferred_element_type=F32)
        up = jnp.dot(xb, wu_ref[:, sl], preferred_element_type=F32)
        h = (gate * jax.nn.sigmoid(gate)) * up
        acc = acc + jnp.dot(h.astype(BF16), wd_ref[sl, :], preferred_element_type=F32)
    out = _layer_norm(alpha * x + 0.5 * acc, g_ref[...], b_ref[...])
    o_ref[...] = out
    ob_ref[...] = out.astype(BF16)


def _ffn(x, wg, wu, wd, g, b, alpha):
    s, d = x.shape
    n_f = wg.shape[1]
    tm = min(512, s)
    fc = 256
    assert s % tm == 0 and n_f % fc == 0
    const = lambda i: (0, 0)
    row = lambda i: (i, 0)
    vmem = (2 * 2 * _nbytes((tm, d), F32) + 2 * _nbytes((tm, d), BF16)
            + 2 * 3 * _nbytes((d, n_f), BF16) + 6 * _nbytes((tm, d), F32))
    return pl.pallas_call(
        functools.partial(_ffn_kernel, alpha=alpha, fc=fc),
        grid=(s // tm,),
        in_specs=[pl.BlockSpec((tm, d), row), pl.BlockSpec((d, n_f), const), pl.BlockSpec((d, n_f), const),
                  pl.BlockSpec((n_f, d), const), pl.BlockSpec((1, d), const), pl.BlockSpec((1, d), const)],
        out_specs=[pl.BlockSpec((tm, d), row), pl.BlockSpec((tm, d), row)],
        out_shape=[jax.ShapeDtypeStruct((s, d), F32), jax.ShapeDtypeStruct((s, d), BF16)],
        compiler_params=_cparams(("parallel",), vmem),
        name="ffn_deepnorm",
    )(x, wg, wu, wd, g.reshape(1, d), b.reshape(1, d))


def _swap_halves_64(a):
    half = HEAD_DIM // 2
    first = (_lane_iota(a.shape) % HEAD_DIM) < half
    return jnp.where(first, pltpu.roll(a, LANES - half, 1), pltpu.roll(a, half, 1))


def _mm_kernel(*refs, rope, scale):
    if rope:
        x_ref, w_ref, c_ref, s_ref, o_ref = refs
    else:
        x_ref, w_ref, o_ref = refs
    acc = jnp.dot(x_ref[...], w_ref[...], preferred_element_type=F32)
    if rope:
        cos = c_ref[...]
        sin = s_ref[...]
        for c in range(acc.shape[1] // LANES):
            a = acc[:, c * LANES:(c + 1) * LANES]
            r = a * cos + _swap_halves_64(a) * sin
            if scale != 1.0:
                r = r * scale
            o_ref[:, c * LANES:(c + 1) * LANES] = r.astype(o_ref.dtype)
    else:
        if scale != 1.0:
            acc = acc * scale
        o_ref[...] = acc.astype(o_ref.dtype)


def _mm(xb, w, cols, out_dtype, *, rope_tabs=None, scale=1.0):
    s, k = xb.shape
    col0, n = cols[0], cols[1] - cols[0]
    tm = min(1024, s)
    tn = next(c for c in (512, 256, LANES) if n % c == 0 and col0 % c == 0)
    assert s % tm == 0
    jb = col0 // tn
    in_specs = [pl.BlockSpec((tm, k), lambda i, j: (i, 0)), pl.BlockSpec((k, tn), lambda i, j: (0, jb + j))]
    args = [xb, w]
    if rope_tabs is not None:
        in_specs += [pl.BlockSpec((tm, LANES), lambda i, j: (i, 0))] * 2
        args += list(rope_tabs)
    vmem = 2 * (_nbytes((tm, k), BF16) + _nbytes((k, tn), BF16) + _nbytes((tm, tn), out_dtype)
                + 2 * _nbytes((tm, LANES), F32)) + 3 * _nbytes((tm, tn), F32)
    return pl.pallas_call(
        functools.partial(_mm_kernel, rope=rope_tabs is not None, scale=scale),
        grid=(s // tm, n // tn),
        in_specs=in_specs,
        out_specs=pl.BlockSpec((tm, tn), lambda i, j: (i, j)),
        out_shape=jax.ShapeDtypeStruct((s, n), out_dtype),
        compiler_params=_cparams(("parallel", "parallel"), vmem),
        name="proj_rope" if rope_tabs is not None else "proj",
    )(*args)


def _compress_kernel(x_ref, pos_ref, w1_ref, w2_ref, o_ref):
    n_cmp = o_ref.shape[0]
    hidden = w1_ref.shape[3]
    ha = [jnp.zeros((n_cmp, hidden), F32) for _ in range(NSA_GROUPS)]
    hb = [jnp.zeros((n_cmp, hidden), F32) for _ in range(NSA_GROUPS)]
    tail = [jnp.zeros((8, hidden), F32) for _ in range(NSA_GROUPS)]
    for j in range(CMP_STRIDE):
        xj = x_ref[pl.ds(j, n_cmp, stride=CMP_STRIDE), :]
        pa = pos_ref[j:j + 1, :]
        pb = pos_ref[CMP_STRIDE + j:CMP_STRIDE + j + 1, :]
        xa = (xj + pa).astype(BF16)
        xb = (xj + pb).astype(BF16)
        pb8 = jnp.broadcast_to(pb, (8, pb.shape[1])).astype(BF16)
        for g in range(NSA_GROUPS):
            ha[g] = ha[g] + jnp.dot(xa, w1_ref[g, j], preferred_element_type=F32)
            hb[g] = hb[g] + jnp.dot(xb, w1_ref[g, CMP_STRIDE + j], preferred_element_type=F32)
            tail[g] = tail[g] + jnp.dot(pb8, w1_ref[g, CMP_STRIDE + j], preferred_element_type=F32)
    last = _row_iota((n_cmp, 1)) == n_cmp - 1
    outs = []
    for g in range(NSA_GROUPS):
        hb_next = jnp.where(last, tail[g][0:1, :], pltpu.roll(hb[g], n_cmp - 1, 0))
        h = jax.nn.gelu(ha[g] + hb_next)
        outs.append(jnp.dot(h.astype(BF16), w2_ref[...], preferred_element_type=F32))
    lo = _lane_iota(outs[0].shape) < HEAD_DIM
    o_ref[...] = jnp.where(lo, outs[0], outs[1]).astype(o_ref.dtype)


def _compress(x, col, pos, w1, w2):
    s = x.shape[0]
    n_cmp = s // CMP_STRIDE
    hidden = w1.shape[1]
    pos2 = jnp.concatenate([pos] * NSA_GROUPS, axis=1)
    w1r = w1.reshape(CMP_LEN, HEAD_DIM, hidden).astype(BF16)
    zero = jnp.zeros_like(w1r)
    w1g = jnp.stack([jnp.concatenate([w1r if gg == g else zero for gg in range(NSA_GROUPS)], axis=1)
                     for g in range(NSA_GROUPS)])
    w2d = jnp.concatenate([w2, w2], axis=1).astype(BF16)
    vmem = 2 * (_nbytes((s, NSA_KV_W), F32) + _nbytes(w1g.shape, BF16)) + 16 * _nbytes((n_cmp, hidden), F32)
    return pl.pallas_call(
        _compress_kernel,
        grid=(1,),
        in_specs=[pl.BlockSpec((s, NSA_KV_W), lambda i: (0, col)), pl.BlockSpec(pos2.shape, lambda i: (0, 0)),
                  pl.BlockSpec(w1g.shape, lambda i: (0, 0, 0, 0)), pl.BlockSpec(w2d.shape, lambda i: (0, 0))],
        out_specs=pl.BlockSpec((n_cmp, NSA_KV_W), lambda i: (0, 0)),
        out_shape=jax.ShapeDtypeStruct((n_cmp, NSA_KV_W), BF16),
        compiler_params=_cparams(("arbitrary",), vmem),
        name="nsa_compress",
    )(x, pos2, w1g, w2d)


def _stack_group_queries(q_ref, col0, kv_lane_group):
    parts = []
    for h in range(NSA_HPG):
        c = col0 + (h // 2) * LANES
        qc = q_ref[:, c:c + LANES].astype(F32)
        e = h % 2
        lane = _lane_iota(qc.shape)
        mine = (lane >= e * HEAD_DIM) & (lane < (e + 1) * HEAD_DIM)
        qh = jnp.where(mine, qc, 0.0)
        if e != kv_lane_group:
            qh = pltpu.roll(qh, HEAD_DIM, 1)
        parts.append(qh.astype(BF16))
    return jnp.concatenate(parts, axis=0)


def _unstack_group_outputs(o, kv_lane_group):
    chunks = []
    for c in range(NSA_HPG // 2):
        halves = []
        for e in range(2):
            h = 2 * c + e
            oh = o[h * Q_BLOCK:(h + 1) * Q_BLOCK, :]
            if e != kv_lane_group:
                oh = pltpu.roll(oh, HEAD_DIM, 1)
            halves.append(oh)
        lo = _lane_iota(halves[0].shape) < HEAD_DIM
        chunks.append(jnp.where(lo, halves[0], halves[1]))
    return jnp.concatenate(chunks, axis=1)


def _qk(q, k):
    return lax.dot_general(q, k, (((1,), (1,)), ((), ())), preferred_element_type=F32)


def _softmax_rows(s, mask, base2=False):
    sm = jnp.where(mask, s, MASKED)
    m = jnp.maximum(jnp.max(sm, axis=-1, keepdims=True), M_INIT)
    e = jnp.exp2(sm - m) if base2 else jnp.exp(sm - m)
    return e, jnp.sum(e, axis=-1, keepdims=True), m


def _cmp_select_kernel(q_ref, kc_ref, vc_ref, mmap_ref, oc_ref, nsel_ref, *, top_k, n_cls):
    b = pl.program_id(0)
    n_cmp = kc_ref.shape[0]
    n_slc = mmap_ref.shape[1]
    rows = NSA_HPG * Q_BLOCK
    gw = NSA_HPG * HEAD_DIM
    t4 = b * Q_BLOCK + (_row_iota((rows, 1)) % Q_BLOCK)
    t = b * Q_BLOCK + _row_iota((Q_BLOCK, 1))

    def variant(nc, nb):
        valid = _lane_iota((1, nc)) * CMP_STRIDE + (CMP_LEN - 1) <= t4
        blk = _lane_iota((1, nb))
        cur = t // SLC_BLOCK
        forced = (blk == 0) | (blk == cur) | (blk == cur - 1)
        future = blk * SLC_BLOCK > t
        blk_f = blk.astype(F32)
        kc = kc_ref[0:nc, :]
        vc = vc_ref[0:nc, :]
        mmap = mmap_ref[0:nc, 0:nb]
        for g in range(NSA_GROUPS):
            qst = _stack_group_queries(q_ref, g * gw, g)
            e, den, _ = _softmax_rows(_qk(qst, kc), valid)
            p = e / jnp.maximum(den, 1e-30)
            o = jnp.dot(p.astype(BF16), vc, preferred_element_type=F32)
            oc_ref[:, g * gw:(g + 1) * gw] = _unstack_group_outputs(o, g)
            psum = p[0:Q_BLOCK]
            for h in range(1, NSA_HPG):
                psum = psum + p[h * Q_BLOCK:(h + 1) * Q_BLOCK]
            hi = psum.astype(BF16)
            r1 = psum - hi.astype(F32)
            mid = r1.astype(BF16)
            low = (r1 - mid.astype(F32)).astype(BF16)
            p_slc = (jnp.dot(hi, mmap, preferred_element_type=F32) + jnp.dot(mid, mmap, preferred_element_type=F32)
                     + jnp.dot(low, mmap, preferred_element_type=F32))
            score = jnp.where(forced, FORCE_SCORE, jnp.where(future, -1.0, p_slc))
            for _ in range(top_k):
                mx = jnp.max(score, axis=-1, keepdims=True)
                first = jnp.min(jnp.where(score == mx, blk_f, float(nb)), axis=-1, keepdims=True)
                score = jnp.where(blk_f == first, -2.0, score)
            nsel_ref[:, g * n_slc:g * n_slc + nb] = jnp.where(score == -2.0, 0.0, 1.0).astype(nsel_ref.dtype)
            if nb < n_slc:
                nsel_ref[:, g * n_slc + nb:(g + 1) * n_slc] = jnp.ones((Q_BLOCK, n_slc - nb), nsel_ref.dtype)

    for c in range(n_cls):
        nc = (c + 1) * (n_cmp // n_cls)
        nb = min(n_slc, -(-(nc // SLC_RATIO) // LANES) * LANES)
        lo_b = c * (nc // (c + 1)) // (Q_BLOCK // CMP_STRIDE)
        hi_b = nc // (Q_BLOCK // CMP_STRIDE)
        pl.when((b >= lo_b) & (b < hi_b))(functools.partial(variant, nc, nb))


def _cmp_select(q_u, kc, vc):
    s = q_u.shape[0]
    n_cmp, n_slc = s // CMP_STRIDE, s // SLC_BLOCK
    top_k = min(SLC_TOPK, n_slc)
    n_cls = max(n_cmp // (2 * LANES), 1)
    assert n_cmp % n_cls == 0 and (n_cmp // n_cls) % (Q_BLOCK // CMP_STRIDE) == 0
    m = np.arange(n_cmp)[:, None]
    j = np.arange(n_slc)[None, :]
    mmap = jnp.asarray(((m >= SLC_RATIO * j - 1) & (m <= SLC_RATIO * j + SLC_RATIO - 1)).astype(np.float32), BF16)
    const = lambda b: (0, 0)
    rows = NSA_HPG * Q_BLOCK
    vmem = 2 * (2 * _nbytes((n_cmp, NSA_KV_W), BF16) + _nbytes((n_cmp, n_slc), BF16)
                + _nbytes((Q_BLOCK, NSA_Q_W), F32) * 2) + 8 * _nbytes((rows, n_cmp), F32)
    return pl.pallas_call(
        functools.partial(_cmp_select_kernel, top_k=top_k, n_cls=n_cls),
        grid=(s // Q_BLOCK,),
        in_specs=[pl.BlockSpec((Q_BLOCK, NSA_Q_W), lambda b: (b, 0)), pl.BlockSpec((n_cmp, NSA_KV_W), const),
                  pl.BlockSpec((n_cmp, NSA_KV_W), const), pl.BlockSpec((n_cmp, n_slc), const)],
        out_specs=[pl.BlockSpec((Q_BLOCK, NSA_Q_W), lambda b: (b, 0)),
                   pl.BlockSpec((Q_BLOCK, NSA_GROUPS * n_slc), lambda b: (b, 0))],
        out_shape=[jax.ShapeDtypeStruct((s, NSA_Q_W), F32), jax.ShapeDtypeStruct((s, NSA_GROUPS * n_slc), BF16)],
        compiler_params=_cparams(("parallel",), vmem),
        name="nsa_compressed_select",
    )(q_u, kc, vc, mmap)


def _sel_attn_kernel(q_ref, k_ref, v_ref, nsel_ref, o_ref, qa_ref, m_ref, acc_ref, *, win):
    b = pl.program_id(0)
    n_slc = nsel_ref.shape[1] // NSA_GROUPS
    n_win = n_slc // win
    tiles_per_win = win * SLC_BLOCK // SEL_TILE
    gw = NSA_HPG * HEAD_DIM
    n_chunks = SEL_TILE // LANES
    rows = NSA_HPG * Q_BLOCK
    last = (b * Q_BLOCK) // SEL_TILE
    t = b * Q_BLOCK + _row_iota((Q_BLOCK, 1))
    lane = _lane_iota((Q_BLOCK, LANES))
    head_lanes = lane < HEAD_DIM
    flag_lanes = (lane >= HEAD_DIM) & (lane < HEAD_DIM + win)

    nsel_all = nsel_ref[...].astype(F32)
    if nsel_all.shape[1] < LANES:
        nsel_all = jnp.concatenate([nsel_all, jnp.zeros((Q_BLOCK, LANES - nsel_all.shape[1]), F32)], axis=1)
    for g in range(NSA_GROUPS):
        heads = []
        for h in range(NSA_HPG):
            c = g * gw + (h // 2) * LANES
            qc = q_ref[:, c:c + LANES].astype(F32)
            heads.append(qc if h % 2 == 0 else pltpu.roll(qc, HEAD_DIM, 1))
        for w in range(n_win):
            a = g * n_slc + w * win
            fl = nsel_all[:, (a // LANES) * LANES:(a // LANES + 1) * LANES]
            shift = (HEAD_DIM - a % LANES) % LANES
            if shift:
                fl = pltpu.roll(fl, shift, 1)
            fl = jnp.where(flag_lanes, fl, 0.0)
            for h in range(NSA_HPG):
                qa_ref[g, w, h * Q_BLOCK:(h + 1) * Q_BLOCK, :] = jnp.where(head_lanes, heads[h], fl).astype(BF16)
    m_ref[...] = jnp.full(m_ref.shape, M_INIT, F32)
    acc_ref[...] = jnp.zeros(acc_ref.shape, F32)

    def tile(g, kt, causal):
        off = pl.multiple_of(kt * SEL_TILE, SEL_TILE)
        k = k_ref[g, pl.ds(off, SEL_TILE), :]
        v = v_ref[g, pl.ds(off, SEL_TILE), :]
        s = _qk(qa_ref[g, kt // tiles_per_win], k)
        if causal:
            keep = off + _lane_iota((1, SEL_TILE)) <= t
            s = jnp.where(jnp.concatenate([keep] * NSA_HPG, axis=0), s, MASKED)
        m_old = m_ref[g]
        m_new = jnp.maximum(m_old, jnp.max(s, axis=-1, keepdims=True))
        alpha = jnp.exp2(m_old - m_new)
        p = jnp.concatenate([jnp.exp2(s[:, c * LANES:(c + 1) * LANES] - m_new).astype(BF16)
                             for c in range(n_chunks)], axis=1)
        acc_ref[g] = alpha * acc_ref[g] + jnp.dot(p, v, preferred_element_type=F32)
        m_ref[g] = m_new

    def body(kt, carry):
        for g in range(NSA_GROUPS):
            tile(g, kt, False)
        return carry

    lax.fori_loop(0, last, body, 0)
    for g in range(NSA_GROUPS):
        tile(g, last, True)
        acc = acc_ref[g]
        o = acc / jnp.maximum(acc[:, HEAD_DIM:HEAD_DIM + 1], 1e-30)
        for c in range(NSA_HPG // 2):
            even = o[(2 * c) * Q_BLOCK:(2 * c + 1) * Q_BLOCK, :]
            odd = pltpu.roll(o[(2 * c + 1) * Q_BLOCK:(2 * c + 2) * Q_BLOCK, :], HEAD_DIM, 1)
            o_ref[:, g * gw + c * LANES:g * gw + (c + 1) * LANES] = jnp.where(head_lanes, even, odd)


def _sel_attn(rot_a, k_s, v_s, nsel):
    s = rot_a.shape[0]
    n_slc = s // SLC_BLOCK
    win = min(n_slc, LANES - HEAD_DIM)
    assert s % SEL_TILE == 0 and n_slc % win == 0 and (win * SLC_BLOCK) % SEL_TILE == 0
    n_win = n_slc // win
    spare = LANES - HEAD_DIM
    blk = (np.arange(s) // SLC_BLOCK) % win
    marks = jnp.asarray(np.where(np.arange(spare)[None, :] == blk[:, None], MASKED, 0.0).astype(np.float32), BF16)
    ones = jnp.asarray((np.arange(spare)[None, :] == 0).astype(np.float32) * np.ones((s, 1), np.float32), BF16)
    grp = lambda a, g: a[:, g * HEAD_DIM:(g + 1) * HEAD_DIM]
    k_aug = jnp.stack([jnp.concatenate([grp(k_s, g), marks], axis=1) for g in range(NSA_GROUPS)])
    v_aug = jnp.stack([jnp.concatenate([grp(v_s, g), ones], axis=1) for g in range(NSA_GROUPS)])
    rows = NSA_HPG * Q_BLOCK
    vmem = (2 * (2 * _nbytes(k_aug.shape, BF16) + _nbytes((Q_BLOCK, NSA_Q_W), BF16)
                 + _nbytes((Q_BLOCK, NSA_GROUPS * n_slc), BF16) + _nbytes((Q_BLOCK, NSA_Q_W), F32))
            + NSA_GROUPS * (n_win * _nbytes((rows, LANES), BF16) + 2 * _nbytes((rows, LANES), F32))
            + 2 * 4 * _nbytes((rows, SEL_TILE), F32))
    return pl.pallas_call(
        functools.partial(_sel_attn_kernel, win=win),
        grid=(s // Q_BLOCK,),
        in_specs=[pl.BlockSpec((Q_BLOCK, NSA_Q_W), lambda b: (b, 0)),
                  pl.BlockSpec(k_aug.shape, lambda b: (0, 0, 0)),
                  pl.BlockSpec(v_aug.shape, lambda b: (0, 0, 0)),
                  pl.BlockSpec((Q_BLOCK, NSA_GROUPS * n_slc), lambda b: (b, 0))],
        out_specs=pl.BlockSpec((Q_BLOCK, NSA_Q_W), lambda b: (b, 0)),
        out_shape=jax.ShapeDtypeStruct((s, NSA_Q_W), F32),
        scratch_shapes=[pltpu.VMEM((NSA_GROUPS, n_win, rows, LANES), BF16),
                        pltpu.VMEM((NSA_GROUPS, rows, LANES), F32), pltpu.VMEM((NSA_GROUPS, rows, LANES), F32)],
        compiler_params=_cparams(("parallel",), vmem),
        name="nsa_selected",
    )(rot_a, k_aug, v_aug, nsel)


def _win_attn_kernel(q_ref, k_ref, v_ref, o_ref, *, span):
    b = pl.program_id(0)
    rows = NSA_HPG * Q_BLOCK
    n_back = NSA_WINDOW // Q_BLOCK
    start = pl.multiple_of(jnp.maximum(b - n_back, 0) * Q_BLOCK, Q_BLOCK)
    k = k_ref[pl.ds(start, span), :]
    v = v_ref[pl.ds(start, span), :]
    t4 = b * Q_BLOCK + (_row_iota((rows, 1)) % Q_BLOCK)
    diff = t4 - (start + _lane_iota((1, span)))
    mask = (diff >= 0) & (diff < NSA_WINDOW)
    for g in range(NSA_GROUPS):
        qst = _stack_group_queries(q_ref, g * NSA_HPG * HEAD_DIM, g)
        e, den, _ = _softmax_rows(_qk(qst, k), mask, base2=True)
        o = jnp.dot(e.astype(BF16), v, preferred_element_type=F32) / jnp.maximum(den, 1e-30)
        o_ref[:, g * NSA_HPG * HEAD_DIM:(g + 1) * NSA_HPG * HEAD_DIM] = _unstack_group_outputs(o, g)


def _win_attn(q_rot, k_rot, v_plain, kw_col, vw_col):
    s = q_rot.shape[0]
    span = min(NSA_WINDOW + Q_BLOCK, s)
    rows = NSA_HPG * Q_BLOCK
    vmem = 2 * (2 * _nbytes((s, NSA_KV_W), BF16) + _nbytes((Q_BLOCK, NSA_Q_W), BF16)
                + _nbytes((Q_BLOCK, NSA_Q_W), F32)) + 8 * _nbytes((rows, span), F32)
    return pl.pallas_call(
        functools.partial(_win_attn_kernel, span=span),
        grid=(s // Q_BLOCK,),
        in_specs=[pl.BlockSpec((Q_BLOCK, NSA_Q_W), lambda b: (b, 0)),
                  pl.BlockSpec((s, NSA_KV_W), lambda b: (0, kw_col)),
                  pl.BlockSpec((s, NSA_KV_W), lambda b: (0, vw_col))],
        out_specs=pl.BlockSpec((Q_BLOCK, NSA_Q_W), lambda b: (b, 0)),
        out_shape=jax.ShapeDtypeStruct((s, NSA_Q_W), F32),
        compiler_params=_cparams(("parallel",), vmem),
        name="nsa_window",
    )(q_rot, k_rot, v_plain)


def _dil_attn_kernel(q_ref, kp_ref, kc_ref, vp_ref, vc_ref, o_ref, lse_ref, kbuf, vbuf, *, span, dil, nb):
    n = pl.program_id(0)
    unit = dil * Q_BLOCK
    kbuf[0:unit, :] = kp_ref[...]
    kbuf[unit:, :] = kc_ref[...]
    vbuf[0:unit, :] = vp_ref[...]
    vbuf[unit:, :] = vc_ref[...]
    qi = _row_iota((Q_BLOCK, 1)) + Q_BLOCK
    ki = _lane_iota((1, 2 * Q_BLOCK))
    delta = qi - ki
    band = (delta >= 0) & (delta <= span)
    lo = _lane_iota((Q_BLOCK, LANES)) < HEAD_DIM

    def item(idx, carry):
        j = idx // dil
        base = j * unit + idx % dil
        if dil > 1:
            q_rows, kv_rows = pl.ds(base, Q_BLOCK, stride=dil), pl.ds(base, 2 * Q_BLOCK, stride=dil)
        else:
            base = pl.multiple_of(base, Q_BLOCK)
            q_rows, kv_rows = pl.ds(base, Q_BLOCK), pl.ds(base, 2 * Q_BLOCK)
        mask = band & ((n * nb + j > 0) | (ki >= Q_BLOCK))
        q = q_ref[q_rows, :]
        k = kbuf[kv_rows, :].astype(BF16)
        v = vbuf[kv_rows, :].astype(BF16)
        outs, lses = [], []
        for e in range(2):
            qh = jnp.where(lo if e == 0 else ~lo, q, 0.0).astype(BF16)
            ex, den, m = _softmax_rows(_qk(qh, k), mask)
            den = jnp.maximum(den, 1e-30)
            outs.append(jnp.dot(ex.astype(BF16), v, preferred_element_type=F32) / den)
            lses.append(jnp.broadcast_to(m + jnp.log(den), (Q_BLOCK, LANES)))
        o_ref[q_rows, :] = jnp.where(lo, outs[0], outs[1])
        lse_ref[q_rows, :] = jnp.where(lo, lses[0], lses[1])
        return carry

    lax.fori_loop(0, nb * dil, item, 0, unroll=2)


def _dil_attn(qb, kb, vb, gidx, win, dil):
    s = qb.shape[0]
    unit = dil * Q_BLOCK
    tb = min(max(DIL_STEP_TOKENS, unit), s)
    assert s % tb == 0 and tb % unit == 0
    nb = tb // unit
    span = win // dil
    n_chunks = DIL_OUT_W // LANES
    cur = lambda n, c: (n, gidx * n_chunks + c)
    prev = lambda n, c: (jnp.maximum(n * nb - 1, 0), gidx * n_chunks + c)
    blk, pblk = (tb, LANES), (unit, LANES)
    vmem = (2 * (5 * _nbytes(blk, F32) + 2 * _nbytes(pblk, F32)) + 2 * _nbytes((tb + unit, LANES), F32)
            + 32 * _nbytes((Q_BLOCK, 2 * Q_BLOCK), F32))
    return pl.pallas_call(
        functools.partial(_dil_attn_kernel, span=span, dil=dil, nb=nb),
        grid=(s // tb, n_chunks),
        in_specs=[pl.BlockSpec(blk, cur), pl.BlockSpec(pblk, prev), pl.BlockSpec(blk, cur),
                  pl.BlockSpec(pblk, prev), pl.BlockSpec(blk, cur)],
        out_specs=[pl.BlockSpec(blk, lambda n, c: (n, c)), pl.BlockSpec(blk, lambda n, c: (n, c))],
        out_shape=[jax.ShapeDtypeStruct((s, DIL_OUT_W), F32)] * 2,
        scratch_shapes=[pltpu.VMEM((tb + unit, LANES), F32)] * 2,
        compiler_params=_cparams(("parallel", "parallel"), vmem),
        name=f"dilated_w{win}_d{dil}",
    )(qb, kb, kb, vb, vb)


def _sgu_kernel(uv_ref, g_ref, b_ref, ws_ref, bs_ref, o_ref):
    uv = jax.nn.gelu(uv_ref[...])
    u = uv[:, :SGU_WIDTH]
    v = _layer_norm(uv[:, SGU_WIDTH:], g_ref[...], b_ref[...]).astype(BF16)
    causal = _lane_iota((SGU_CHUNK, SGU_CHUNK)) <= _row_iota((SGU_CHUNK, SGU_CHUNK))
    bs = bs_ref[...]
    for n in range(uv.shape[0] // SGU_CHUNK):
        rs = slice(n * SGU_CHUNK, (n + 1) * SGU_CHUNK)
        for g in range(SGU_GROUPS):
            cs = slice(g * SGU_GROUP_CH, (g + 1) * SGU_GROUP_CH)
            ws = jnp.where(causal, ws_ref[g], 0.0).astype(BF16)
            sv = jnp.dot(ws, v[rs, cs], preferred_element_type=F32) + bs[:, cs]
            o_ref[rs, cs] = (u[rs, cs] * sv).astype(o_ref.dtype)


def _sgu(uv, ln_g, ln_b, w_s, b_s):
    s = uv.shape[0]
    tm = min(4 * SGU_CHUNK, s)
    assert s % tm == 0
    bs = jnp.repeat(b_s.T, SGU_GROUP_CH, axis=1)
    c2 = lambda i: (0, 0)
    vmem = 2 * (_nbytes((tm, 2 * SGU_WIDTH), F32) + _nbytes((tm, SGU_WIDTH), F32)) + 6 * _nbytes((tm, 2 * SGU_WIDTH), F32)
    return pl.pallas_call(
        _sgu_kernel,
        grid=(s // tm,),
        in_specs=[pl.BlockSpec((tm, 2 * SGU_WIDTH), lambda i: (i, 0)), pl.BlockSpec((1, SGU_WIDTH), c2),
                  pl.BlockSpec((1, SGU_WIDTH), c2), pl.BlockSpec(w_s.shape, lambda i: (0, 0, 0)),
                  pl.BlockSpec((SGU_CHUNK, SGU_WIDTH), c2)],
        out_specs=pl.BlockSpec((tm, SGU_WIDTH), lambda i: (i, 0)),
        out_shape=jax.ShapeDtypeStruct((s, SGU_WIDTH), F32),
        compiler_params=_cparams(("parallel",), vmem),
        name="sgu",
    )(uv, ln_g.reshape(1, -1), ln_b.reshape(1, -1), w_s, bs)


def _split_dot(x, w):
    hi = x.astype(BF16)
    lo = (x - hi.astype(F32)).astype(BF16)
    return jnp.dot(hi, w, preferred_element_type=F32) + jnp.dot(lo, w, preferred_element_type=F32)


def _mix_kernel(x_ref, oc_ref, os_ref, ow_ref, ga_ref, eg_ref, d0_ref, d1_ref, d2_ref, l0_ref, l1_ref, l2_ref,
                sg_ref, gm_ref, wa_ref, wb_ref, wc_ref, wo_ref, g_ref, b_ref, o_ref, *, alpha):
    d = x_ref.shape[1]
    gates = _split_dot(jax.nn.sigmoid(ga_ref[...]), eg_ref[...])
    o_a = (gates[:, 0:NSA_Q_W] * oc_ref[...] + gates[:, NSA_Q_W:2 * NSA_Q_W] * os_ref[...]
           + gates[:, 2 * NSA_Q_W:3 * NSA_Q_W] * ow_ref[...])
    y_a = jnp.dot(o_a.astype(BF16), wa_ref[...], preferred_element_type=F32)
    l0, l1, l2 = l0_ref[...], l1_ref[...], l2_ref[...]
    lm = jnp.maximum(jnp.maximum(l0, l1), l2)
    e0, e1, e2 = jnp.exp(l0 - lm), jnp.exp(l1 - lm), jnp.exp(l2 - lm)
    den = e0 + e1 + e2
    o_b = (e0 / den) * d0_ref[...] + (e1 / den) * d1_ref[...] + (e2 / den) * d2_ref[...]
    y_b = jnp.dot(o_b.astype(BF16), wb_ref[...], preferred_element_type=F32)
    y_c = jnp.dot(sg_ref[...].astype(BF16), wc_ref[...], preferred_element_type=F32)
    gm = jax.nn.sigmoid(gm_ref[...])
    merged = gm[:, 0:d] * y_a + gm[:, d:2 * d] * y_b + gm[:, 2 * d:3 * d] * y_c
    mix = jnp.dot(merged.astype(BF16), wo_ref[...], preferred_element_type=F32)
    o_ref[...] = _layer_norm(alpha * x_ref[...] + mix, g_ref[...], b_ref[...])


def _mix(x, o_c, o_s, o_w, g_a, dil_o, dil_lse, sgu, g_m, wa, wb, wc, wo, g, b, alpha):
    s, d = x.shape
    tm = min(256, s)
    assert s % tm == 0
    col = np.arange(N_BRANCH * NSA_Q_W)
    head, br = (col % NSA_Q_W) // HEAD_DIM, col // NSA_Q_W
    eg = jnp.asarray((np.arange(LANES)[:, None] == (head * N_BRANCH + br)[None, :]).astype(np.float32), BF16)
    row = lambda i: (i, 0)
    const = lambda i: (0, 0)
    rspec = lambda w: pl.BlockSpec((tm, w), row)
    cspec = lambda a: pl.BlockSpec(a.shape, const)
    widths = [d, NSA_Q_W, NSA_Q_W, NSA_Q_W, LANES] + [DIL_OUT_W] * 6 + [SGU_WIDTH, N_BRANCH * d]
    vmem = (2 * sum(_nbytes((tm, w), F32) for w in widths) + 2 * _nbytes((tm, d), F32)
            + 2 * sum(_nbytes(a.shape, BF16) for a in (eg, wa, wb, wc, wo)) + 10 * _nbytes((tm, N_BRANCH * d), F32))
    return pl.pallas_call(
        functools.partial(_mix_kernel, alpha=alpha),
        grid=(s // tm,),
        in_specs=[rspec(d), rspec(NSA_Q_W), rspec(NSA_Q_W), rspec(NSA_Q_W), rspec(LANES), cspec(eg)]
                 + [rspec(DIL_OUT_W)] * 6 + [rspec(SGU_WIDTH), rspec(N_BRANCH * d)]
                 + [cspec(wa), cspec(wb), cspec(wc), cspec(wo), pl.BlockSpec((1, d), const), pl.BlockSpec((1, d), const)],
        out_specs=rspec(d),
        out_shape=jax.ShapeDtypeStruct((s, d), F32),
        compiler_params=_cparams(("parallel",), vmem),
        name="mixer_merge_deepnorm",
    )(x, o_c, o_s, o_w, g_a, eg, *dil_o, *dil_lse, sgu, g_m, wa, wb, wc, wo, g.reshape(1, d), b.reshape(1, d))


def _rope_tables(s):
    inv = 1.0 / (ROPE_THETA ** (jnp.arange(0, HEAD_DIM, 2, dtype=F32) / HEAD_DIM))
    ang = jnp.arange(s, dtype=F32)[:, None] * inv[None, :]
    cos, sin = jnp.cos(ang), jnp.sin(ang)
    reps = LANES // HEAD_DIM
    return jnp.tile(jnp.concatenate([cos, cos], axis=1), (1, reps)), jnp.tile(jnp.concatenate([-sin, sin], axis=1), (1, reps))


def kernel(x, ln_g, ln_b, ffn1_gate, ffn1_up, ffn1_down, ffn2_gate, ffn2_up, ffn2_down, w_in, phi_k_pos, phi_k_w1, phi_k_w2, phi_v_pos, phi_v_w1, phi_v_w2, sgu_ln_g, sgu_ln_b, sgu_w, sgu_b, w_branch_a, w_branch_b, w_branch_c, w_out):
    bsz, s, d = x.shape
    depth = ln_g.shape[0]
    alpha = float((2 * depth) ** 0.25)
    q_scale = float(HEAD_DIM ** -0.5)
    tabs = _rope_tables(s)
    sizes = [NSA_Q_W] + [NSA_KV_W] * 6 + [N_BRANCH * NSA_HEADS, DIL_W, DIL_W, DIL_W, 2 * SGU_WIDTH, N_BRANCH * d]
    offs = np.concatenate([[0], np.cumsum(sizes)]).tolist()
    outs = []
    for bi in range(bsz):
        h = x[bi]
        for l in range(depth):
            (w_qa, w_kc, w_vc, w_ks, w_vs, w_kw, w_vw, w_ga, w_qb, w_kb, w_vb, w_uv, w_gm) = [
                w_in[l][:, offs[i]:offs[i + 1]] for i in range(len(sizes))]
            w_ga = jnp.pad(w_ga, ((0, 0), (0, LANES - w_ga.shape[1])))
            groups = dict(qa=w_qa, k_rot=jnp.concatenate([w_ks, w_kw], axis=1), qb=w_qb, kb=w_kb, vb=w_vb,
                          kv_c=jnp.concatenate([w_kc, w_vc], axis=1), v_plain=jnp.concatenate([w_vs, w_vw], axis=1),
                          uv=w_uv, gm=w_gm, ga=w_ga)
            wp = jnp.concatenate(list(groups.values()), axis=1).astype(BF16)
            ends = np.cumsum([g.shape[1] for g in groups.values()]).tolist()
            cols = {name: (e - g.shape[1], e) for (name, g), e in zip(groups.items(), ends)}

            h, hb = _ffn(h, ffn1_gate[l].astype(BF16), ffn1_up[l].astype(BF16), ffn1_down[l].astype(BF16),
                         ln_g[l, 0], ln_b[l, 0], alpha)

            q_u = _mm(hb, wp, cols["qa"], BF16, scale=q_scale)
            q_rot = _mm(hb, wp, cols["qa"], BF16, rope_tabs=tabs, scale=q_scale * LOG2E)
            k_rot = _mm(hb, wp, cols["k_rot"], BF16, rope_tabs=tabs)
            qb = _mm(hb, wp, cols["qb"], F32, rope_tabs=tabs, scale=q_scale)
            kb = _mm(hb, wp, cols["kb"], F32, rope_tabs=tabs)
            vb = _mm(hb, wp, cols["vb"], F32)
            kv_c = _mm(hb, wp, cols["kv_c"], F32)
            v_plain = _mm(hb, wp, cols["v_plain"], BF16)
            uv = _mm(hb, wp, cols["uv"], F32)
            g_m = _mm(hb, wp, cols["gm"], F32)
            g_a = _mm(hb, wp, cols["ga"], F32)

            kc = _compress(kv_c, 0, phi_k_pos[l], phi_k_w1[l], phi_k_w2[l])
            vc = _compress(kv_c, 1, phi_v_pos[l], phi_v_w1[l], phi_v_w2[l])
            o_c, nsel = _cmp_select(q_u, kc, vc)
            o_s = _sel_attn(q_rot, k_rot[:, :NSA_KV_W], v_plain[:, :NSA_KV_W], nsel)
            o_w = _win_attn(q_rot, k_rot, v_plain, 1, 1)
            dil = [_dil_attn(qb, kb, vb, gi, win, dl) for gi, (win, dl) in enumerate(DIL_PAIRS)]
            sg = _sgu(uv, sgu_ln_g[l], sgu_ln_b[l], sgu_w[l], sgu_b[l])
            h = _mix(h, o_c, o_s, o_w, g_a, [o for o, _ in dil], [e for _, e in dil], sg, g_m,
                     w_branch_a[l].astype(BF16), w_branch_b[l].astype(BF16), w_branch_c[l].astype(BF16),
                     w_out[l].astype(BF16), ln_g[l, 1], ln_b[l, 1], alpha)

            h, _ = _ffn(h, ffn2_gate[l].astype(BF16), ffn2_up[l].astype(BF16), ffn2_down[l].astype(BF16),
                        ln_g[l, 2], ln_b[l, 2], alpha)
        outs.append(h)
    return jnp.stack(outs, axis=0)
```

```python
import functools

import numpy as np
import jax
import jax.numpy as jnp
from jax import lax
from jax.experimental import pallas as pl
from jax.experimental.pallas import tpu as pltpu

F32 = jnp.float32
BF16 = jnp.bfloat16

HEAD_DIM = 64
ROPE_THETA = 10000.0
LN_EPS = 1e-5
Q_BLOCK = 128
NSA_HEADS = 8
NSA_GROUPS = 2
NSA_HPG = NSA_HEADS // NSA_GROUPS
NSA_Q_W = NSA_HEADS * HEAD_DIM
NSA_KV_W = NSA_GROUPS * HEAD_DIM
CMP_LEN = 32
CMP_STRIDE = 16
SLC_BLOCK = 64
SLC_RATIO = SLC_BLOCK // CMP_STRIDE
SLC_TOPK = 16
CMP_Q_BLOCK = 256
DIL_STEP_TOKENS = 2048
SEL_TILE = 1024
NSA_WINDOW = 512
FORCE_SCORE = 1e6
DIL_PAIRS = ((128, 1), (512, 4), (2048, 16))
DIL_HPG = 4
DIL_HEADS = DIL_HPG * 3
DIL_W = DIL_HEADS * HEAD_DIM
DIL_OUT_W = DIL_HPG * HEAD_DIM
SGU_CHUNK = 128
SGU_GROUPS = 4
SGU_GROUP_CH = 128
SGU_WIDTH = SGU_GROUPS * SGU_GROUP_CH
N_BRANCH = 3

LANES = 128
V7X_VMEM_BYTES = 64 * 1024 * 1024
VMEM_CAP = 56 * 1024 * 1024

M_INIT = -1e30
MASKED = -(2.0 ** 101)
LOG2E = 1.4426950408889634


def _cparams(sem, vmem_bytes):
    return pltpu.CompilerParams(dimension_semantics=sem,
                                vmem_limit_bytes=int(min(max(vmem_bytes, 16 * 1024 * 1024), VMEM_CAP)))


def _nbytes(shape, dtype):
    return int(np.prod(shape)) * jnp.dtype(dtype).itemsize


def _layer_norm(y, g, b):
    mu = jnp.mean(y, axis=-1, keepdims=True)
    d = y - mu
    var = jnp.mean(d * d, axis=-1, keepdims=True)
    return d * lax.rsqrt(var + LN_EPS) * g + b


def _lane_iota(shape):
    return lax.broadcasted_iota(jnp.int32, shape, len(shape) - 1)


def _row_iota(shape):
    return lax.broadcasted_iota(jnp.int32, shape, 0)


def _ffn_kernel(x_ref, wg_ref, wu_ref, wd_ref, g_ref, b_ref, o_ref, ob_ref, *, alpha, fc):
    x = x_ref[...]
    xb = x.astype(BF16)
    n_f = wg_ref.shape[1]
    acc = jnp.zeros(x.shape, F32)
    for c in range(n_f // fc):
        sl = slice(c * fc, (c + 1) * fc)
        gate = jnp.dot(xb, wg_ref[:, sl], preferred_element_type=F32)
        up = jnp.dot(xb, wu_ref[:, sl], preferred_element_type=F32)
        h = (gate * jax.nn.sigmoid(gate)) * up
        acc = acc + jnp.dot(h.astype(BF16), wd_ref[sl, :], preferred_element_type=F32)
    out = _layer_norm(alpha * x + 0.5 * acc, g_ref[...], b_ref[...])
    o_ref[...] = out
    ob_ref[...] = out.astype(BF16)


def _ffn(x, wg, wu, wd, g, b, alpha):
    s, d = x.shape
    n_f = wg.shape[1]
    tm = min(512, s)
    fc = 256
    assert s % tm == 0 and n_f % fc == 0
    const = lambda i: (0, 0)
    row = lambda i: (i, 0)
    vmem = (2 * 2 * _nbytes((tm, d), F32) + 2 * _nbytes((tm, d), BF16)
            + 2 * 3 * _nbytes((d, n_f), BF16) + 6 * _nbytes((tm, d), F32))
    return pl.pallas_call(
        functools.partial(_ffn_kernel, alpha=alpha, fc=fc),
        grid=(s // tm,),
        in_specs=[pl.BlockSpec((tm, d), row), pl.BlockSpec((d, n_f), const), pl.BlockSpec((d, n_f), const),
                  pl.BlockSpec((n_f, d), const), pl.BlockSpec((1, d), const), pl.BlockSpec((1, d), const)],
        out_specs=[pl.BlockSpec((tm, d), row), pl.BlockSpec((tm, d), row)],
        out_shape=[jax.ShapeDtypeStruct((s, d), F32), jax.ShapeDtypeStruct((s, d), BF16)],
        compiler_params=_cparams(("parallel",), vmem),
        name="ffn_deepnorm",
    )(x, wg, wu, wd, g.reshape(1, d), b.reshape(1, d))


def _swap_halves_64(a):
    half = HEAD_DIM // 2
    first = (_lane_iota(a.shape) % HEAD_DIM) < half
    return jnp.where(first, pltpu.roll(a, LANES - half, 1), pltpu.roll(a, half, 1))


def _mm_kernel(*refs, rope, scale, plain_scale):
    if rope:
        x_ref, w_ref, c_ref, s_ref, o_ref = refs[:5]
    else:
        x_ref, w_ref, o_ref = refs
    acc = jnp.dot(x_ref[...], w_ref[...], preferred_element_type=F32)
    if plain_scale is not None:
        refs[5][...] = (acc * plain_scale).astype(refs[5].dtype)
    if rope:
        cos = c_ref[...]
        sin = s_ref[...]
        for c in range(acc.shape[1] // LANES):
            a = acc[:, c * LANES:(c + 1) * LANES]
            r = a * cos + _swap_halves_64(a) * sin
            if scale != 1.0:
                r = r * scale
            o_ref[:, c * LANES:(c + 1) * LANES] = r.astype(o_ref.dtype)
    else:
        if scale != 1.0:
            acc = acc * scale
        o_ref[...] = acc.astype(o_ref.dtype)


def _mm(xb, w, cols, out_dtype, *, rope_tabs=None, scale=1.0, plain_scale=None):
    s, k = xb.shape
    col0, n = cols[0], cols[1] - cols[0]
    tm = min(1024, s)
    tn = next(c for c in (1024, 768, 512, 256, LANES) if n % c == 0 and col0 % c == 0)
    assert s % tm == 0 and (plain_scale is None or rope_tabs is not None)
    jb = col0 // tn
    n_out = 1 if plain_scale is None else 2
    in_specs = [pl.BlockSpec((tm, k), lambda i, j: (i, 0)), pl.BlockSpec((k, tn), lambda i, j: (0, jb + j))]
    args = [xb, w]
    if rope_tabs is not None:
        in_specs += [pl.BlockSpec((tm, LANES), lambda i, j: (i, 0))] * 2
        args += list(rope_tabs)
    vmem = 2 * (_nbytes((tm, k), BF16) + _nbytes((k, tn), BF16) + n_out * _nbytes((tm, tn), out_dtype)
                + 2 * _nbytes((tm, LANES), F32)) + 3 * _nbytes((tm, tn), F32)
    out = pl.pallas_call(
        functools.partial(_mm_kernel, rope=rope_tabs is not None, scale=scale, plain_scale=plain_scale),
        grid=(s // tm, n // tn),
        in_specs=in_specs,
        out_specs=[pl.BlockSpec((tm, tn), lambda i, j: (i, j))] * n_out,
        out_shape=[jax.ShapeDtypeStruct((s, n), out_dtype)] * n_out,
        compiler_params=_cparams(("parallel", "parallel"), vmem),
        name="proj_rope" if rope_tabs is not None else "proj",
    )(*args)
    return out[0] if n_out == 1 else out


def _compress_kernel(x_ref, pos_ref, w1_ref, w2_ref, o_ref):
    n_cmp = o_ref.shape[0]
    hidden = w1_ref.shape[3]
    ha = [jnp.zeros((n_cmp, hidden), F32) for _ in range(NSA_GROUPS)]
    hb = [jnp.zeros((n_cmp, hidden), F32) for _ in range(NSA_GROUPS)]
    tail = [jnp.zeros((8, hidden), F32) for _ in range(NSA_GROUPS)]
    for j in range(CMP_STRIDE):
        xj = x_ref[pl.ds(j, n_cmp, stride=CMP_STRIDE), :]
        pa = pos_ref[j:j + 1, :]
        pb = pos_ref[CMP_STRIDE + j:CMP_STRIDE + j + 1, :]
        xa = (xj + pa).astype(BF16)
        xb = (xj + pb).astype(BF16)
        pb8 = jnp.broadcast_to(pb, (8, pb.shape[1])).astype(BF16)
        for g in range(NSA_GROUPS):
            ha[g] = ha[g] + jnp.dot(xa, w1_ref[g, j], preferred_element_type=F32)
            hb[g] = hb[g] + jnp.dot(xb, w1_ref[g, CMP_STRIDE + j], preferred_element_type=F32)
            tail[g] = tail[g] + jnp.dot(pb8, w1_ref[g, CMP_STRIDE + j], preferred_element_type=F32)
    last = _row_iota((n_cmp, 1)) == n_cmp - 1
    outs = []
    for g in range(NSA_GROUPS):
        hb_next = jnp.where(last, tail[g][0:1, :], pltpu.roll(hb[g], n_cmp - 1, 0))
        h = jax.nn.gelu(ha[g] + hb_next)
        outs.append(jnp.dot(h.astype(BF16), w2_ref[...], preferred_element_type=F32))
    lo = _lane_iota(outs[0].shape) < HEAD_DIM
    o_ref[...] = jnp.where(lo, outs[0], outs[1]).astype(o_ref.dtype)


def _compress(x, col, pos, w1, w2):
    s = x.shape[0]
    n_cmp = s // CMP_STRIDE
    hidden = w1.shape[1]
    pos2 = jnp.concatenate([pos] * NSA_GROUPS, axis=1)
    w1r = w1.reshape(CMP_LEN, HEAD_DIM, hidden).astype(BF16)
    zero = jnp.zeros_like(w1r)
    w1g = jnp.stack([jnp.concatenate([w1r if gg == g else zero for gg in range(NSA_GROUPS)], axis=1)
                     for g in range(NSA_GROUPS)])
    w2d = jnp.concatenate([w2, w2], axis=1).astype(BF16)
    vmem = 2 * (_nbytes((s, NSA_KV_W), F32) + _nbytes(w1g.shape, BF16)) + 16 * _nbytes((n_cmp, hidden), F32)
    return pl.pallas_call(
        _compress_kernel,
        grid=(1,),
        in_specs=[pl.BlockSpec((s, NSA_KV_W), lambda i: (0, col)), pl.BlockSpec(pos2.shape, lambda i: (0, 0)),
                  pl.BlockSpec(w1g.shape, lambda i: (0, 0, 0, 0)), pl.BlockSpec(w2d.shape, lambda i: (0, 0))],
        out_specs=pl.BlockSpec((n_cmp, NSA_KV_W), lambda i: (0, 0)),
        out_shape=jax.ShapeDtypeStruct((n_cmp, NSA_KV_W), BF16),
        compiler_params=_cparams(("arbitrary",), vmem),
        name="nsa_compress",
    )(x, pos2, w1g, w2d)


def _stack_group_queries(q_ref, col0, kv_lane_group):
    parts = []
    for h in range(NSA_HPG):
        c = col0 + (h // 2) * LANES
        qc = q_ref[:, c:c + LANES].astype(F32)
        e = h % 2
        lane = _lane_iota(qc.shape)
        mine = (lane >= e * HEAD_DIM) & (lane < (e + 1) * HEAD_DIM)
        qh = jnp.where(mine, qc, 0.0)
        if e != kv_lane_group:
            qh = pltpu.roll(qh, HEAD_DIM, 1)
        parts.append(qh.astype(BF16))
    return jnp.concatenate(parts, axis=0)


def _unstack_group_outputs(o, kv_lane_group):
    chunks = []
    tq = o.shape[0] // NSA_HPG
    for c in range(NSA_HPG // 2):
        halves = []
        for e in range(2):
            h = 2 * c + e
            oh = o[h * tq:(h + 1) * tq, :]
            if e != kv_lane_group:
                oh = pltpu.roll(oh, HEAD_DIM, 1)
            halves.append(oh)
        lo = _lane_iota(halves[0].shape) < HEAD_DIM
        chunks.append(jnp.where(lo, halves[0], halves[1]))
    return jnp.concatenate(chunks, axis=1)


def _qk(q, k):
    return lax.dot_general(q, k, (((1,), (1,)), ((), ())), preferred_element_type=F32)


def _softmax_rows(s, mask, base2=False):
    sm = jnp.where(mask, s, MASKED)
    m = jnp.maximum(jnp.max(sm, axis=-1, keepdims=True), M_INIT)
    e = jnp.exp2(sm - m) if base2 else jnp.exp(sm - m)
    return e, jnp.sum(e, axis=-1, keepdims=True), m


def _cmp_select_kernel(q_ref, kc_ref, vc_ref, mmap_ref, oc_ref, nsel_ref, *, top_k, n_cls):
    b = pl.program_id(0)
    n_cmp = kc_ref.shape[0]
    n_slc = mmap_ref.shape[1]
    tq = q_ref.shape[0]
    rows = NSA_HPG * tq
    gw = NSA_HPG * HEAD_DIM
    t4 = b * tq + (_row_iota((rows, 1)) % tq)
    t = b * tq + _row_iota((tq, 1))

    def variant(nc, nb):
        valid = _lane_iota((1, nc)) * CMP_STRIDE + (CMP_LEN - 1) <= t4
        blk = _lane_iota((1, nb))
        cur = t // SLC_BLOCK
        forced = (blk == 0) | (blk == cur) | (blk == cur - 1)
        future = blk * SLC_BLOCK > t
        kc = kc_ref[0:nc, :]
        vc = vc_ref[0:nc, :]
        mmap = mmap_ref[0:nc, 0:nb]
        for g in range(NSA_GROUPS):
            qst = _stack_group_queries(q_ref, g * gw, g)
            e, den, _ = _softmax_rows(_qk(qst, kc), valid)
            p = e / jnp.maximum(den, 1e-30)
            o = jnp.dot(p.astype(BF16), vc, preferred_element_type=F32)
            oc_ref[:, g * gw:(g + 1) * gw] = _unstack_group_outputs(o, g)
            psum = p[0:tq]
            for h in range(1, NSA_HPG):
                psum = psum + p[h * tq:(h + 1) * tq]
            hi = psum.astype(BF16)
            r1 = psum - hi.astype(F32)
            mid = r1.astype(BF16)
            low = (r1 - mid.astype(F32)).astype(BF16)
            p_slc = (jnp.dot(hi, mmap, preferred_element_type=F32) + jnp.dot(mid, mmap, preferred_element_type=F32)
                     + jnp.dot(low, mmap, preferred_element_type=F32))
            score = jnp.where(forced, FORCE_SCORE, jnp.where(future, -1.0, p_slc))
            for _ in range(top_k):
                first = jnp.argmax(score, axis=-1, keepdims=True)
                score = jnp.where(blk == first, -2.0, score)
            nsel_ref[:, g * n_slc:g * n_slc + nb] = jnp.where(score == -2.0, 0.0, 1.0).astype(nsel_ref.dtype)
            if nb < n_slc:
                nsel_ref[:, g * n_slc + nb:(g + 1) * n_slc] = jnp.ones((tq, n_slc - nb), nsel_ref.dtype)

    for c in range(n_cls):
        nc = (c + 1) * (n_cmp // n_cls)
        nb = min(n_slc, -(-(nc // SLC_RATIO) // LANES) * LANES)
        lo_b = c * (nc // (c + 1)) // (tq // CMP_STRIDE)
        hi_b = nc // (tq // CMP_STRIDE)
        pl.when((b >= lo_b) & (b < hi_b))(functools.partial(variant, nc, nb))


def _cmp_select(q_u, kc, vc):
    s = q_u.shape[0]
    n_cmp, n_slc = s // CMP_STRIDE, s // SLC_BLOCK
    top_k = min(SLC_TOPK, n_slc)
    n_cls = max(n_cmp // (2 * LANES), 1)
    tq = min(CMP_Q_BLOCK, s)
    assert s % tq == 0 and n_cmp % n_cls == 0 and (n_cmp // n_cls) % (tq // CMP_STRIDE) == 0
    m = np.arange(n_cmp)[:, None]
    j = np.arange(n_slc)[None, :]
    mmap = jnp.asarray(((m >= SLC_RATIO * j - 1) & (m <= SLC_RATIO * j + SLC_RATIO - 1)).astype(np.float32), BF16)
    const = lambda b: (0, 0)
    rows = NSA_HPG * tq
    vmem = 2 * (2 * _nbytes((n_cmp, NSA_KV_W), BF16) + _nbytes((n_cmp, n_slc), BF16)
                + _nbytes((tq, NSA_Q_W), F32) * 2) + 6 * _nbytes((rows, n_cmp), F32)
    return pl.pallas_call(
        functools.partial(_cmp_select_kernel, top_k=top_k, n_cls=n_cls),
        grid=(s // tq,),
        in_specs=[pl.BlockSpec((tq, NSA_Q_W), lambda b: (b, 0)), pl.BlockSpec((n_cmp, NSA_KV_W), const),
                  pl.BlockSpec((n_cmp, NSA_KV_W), const), pl.BlockSpec((n_cmp, n_slc), const)],
        out_specs=[pl.BlockSpec((tq, NSA_Q_W), lambda b: (b, 0)),
                   pl.BlockSpec((tq, NSA_GROUPS * n_slc), lambda b: (b, 0))],
        out_shape=[jax.ShapeDtypeStruct((s, NSA_Q_W), F32), jax.ShapeDtypeStruct((s, NSA_GROUPS * n_slc), BF16)],
        compiler_params=_cparams(("parallel",), vmem),
        name="nsa_compressed_select",
    )(q_u, kc, vc, mmap)


def _sel_attn_kernel(q_ref, k_ref, v_ref, nsel_ref, o_ref, qa_ref, m_ref, acc_ref, s_ref, *, win):
    b = pl.program_id(0)
    n_slc = nsel_ref.shape[1] // NSA_GROUPS
    n_win = n_slc // win
    tiles_per_win = win * SLC_BLOCK // SEL_TILE
    gw = NSA_HPG * HEAD_DIM
    n_chunks = SEL_TILE // LANES
    rows = NSA_HPG * Q_BLOCK
    last = (b * Q_BLOCK) // SEL_TILE
    t = b * Q_BLOCK + _row_iota((Q_BLOCK, 1))
    lane = _lane_iota((Q_BLOCK, LANES))
    head_lanes = lane < HEAD_DIM
    flag_lanes = (lane >= HEAD_DIM) & (lane < HEAD_DIM + win)

    nsel_all = nsel_ref[...].astype(F32)
    if nsel_all.shape[1] < LANES:
        nsel_all = jnp.concatenate([nsel_all, jnp.zeros((Q_BLOCK, LANES - nsel_all.shape[1]), F32)], axis=1)
    for g in range(NSA_GROUPS):
        heads = []
        for h in range(NSA_HPG):
            c = g * gw + (h // 2) * LANES
            qc = q_ref[:, c:c + LANES].astype(F32)
            heads.append(qc if h % 2 == 0 else pltpu.roll(qc, HEAD_DIM, 1))
        for w in range(n_win):
            a = g * n_slc + w * win
            fl = nsel_all[:, (a // LANES) * LANES:(a // LANES + 1) * LANES]
            shift = (HEAD_DIM - a % LANES) % LANES
            if shift:
                fl = pltpu.roll(fl, shift, 1)
            fl = jnp.where(flag_lanes, fl, 0.0)
            for h in range(NSA_HPG):
                qa_ref[g, w, h * Q_BLOCK:(h + 1) * Q_BLOCK, :] = jnp.where(head_lanes, heads[h], fl).astype(BF16)
    m_ref[...] = jnp.full(m_ref.shape, M_INIT, F32)
    acc_ref[...] = jnp.zeros(acc_ref.shape, F32)

    def scores(kt, slot):
        off = pl.multiple_of(kt * SEL_TILE, SEL_TILE)
        for g in range(NSA_GROUPS):
            s_ref[g, slot] = _qk(qa_ref[g, kt // tiles_per_win], k_ref[g, pl.ds(off, SEL_TILE), :])

    def update(kt, slot, causal):
        off = pl.multiple_of(kt * SEL_TILE, SEL_TILE)
        for g in range(NSA_GROUPS):
            s = s_ref[g, slot]
            if causal:
                keep = off + _lane_iota((1, SEL_TILE)) <= t
                s = jnp.where(jnp.concatenate([keep] * NSA_HPG, axis=0), s, MASKED)
            m_old = m_ref[g]
            m_new = jnp.maximum(m_old, jnp.max(s, axis=-1, keepdims=True))
            alpha = jnp.exp2(m_old - m_new)
            p = jnp.concatenate([jnp.exp2(s[:, c * LANES:(c + 1) * LANES] - m_new).astype(BF16)
                                 for c in range(n_chunks)], axis=1)
            v = v_ref[g, pl.ds(off, SEL_TILE), :]
            acc_ref[g] = alpha * acc_ref[g] + jnp.dot(p, v, preferred_element_type=F32)
            m_ref[g] = m_new

    def finish():
        for g in range(NSA_GROUPS):
            acc = acc_ref[g]
            o = acc / jnp.maximum(acc[:, HEAD_DIM:HEAD_DIM + 1], 1e-30)
            for c in range(NSA_HPG // 2):
                even = o[(2 * c) * Q_BLOCK:(2 * c + 1) * Q_BLOCK, :]
                odd = pltpu.roll(o[(2 * c + 1) * Q_BLOCK:(2 * c + 2) * Q_BLOCK, :], HEAD_DIM, 1)
                o_ref[:, g * gw + c * LANES:g * gw + (c + 1) * LANES] = jnp.where(head_lanes, even, odd)

    scores(0, 0)

    def pair(j, carry):
        scores(2 * j + 1, 1)
        update(2 * j, 0, False)
        scores(2 * j + 2, 0)
        update(2 * j + 1, 1, False)
        return carry

    lax.fori_loop(0, last // 2, pair, 0)

    @pl.when(last % 2 == 1)
    def _():
        scores(last, 1)
        update(last - 1, 0, False)
        update(last, 1, True)
        finish()

    @pl.when(last % 2 == 0)
    def _():
        update(last, 0, True)
        finish()


def _sel_attn(rot_a, k_s, v_s, nsel):
    s = rot_a.shape[0]
    n_slc = s // SLC_BLOCK
    win = min(n_slc, LANES - HEAD_DIM)
    assert s % SEL_TILE == 0 and n_slc % win == 0 and (win * SLC_BLOCK) % SEL_TILE == 0
    n_win = n_slc // win
    spare = LANES - HEAD_DIM
    blk = (np.arange(s) // SLC_BLOCK) % win
    marks = jnp.asarray(np.where(np.arange(spare)[None, :] == blk[:, None], MASKED, 0.0).astype(np.float32), BF16)
    ones = jnp.asarray((np.arange(spare)[None, :] == 0).astype(np.float32) * np.ones((s, 1), np.float32), BF16)
    grp = lambda a, g: a[:, g * HEAD_DIM:(g + 1) * HEAD_DIM]
    k_aug = jnp.stack([jnp.concatenate([grp(k_s, g), marks], axis=1) for g in range(NSA_GROUPS)])
    v_aug = jnp.stack([jnp.concatenate([grp(v_s, g), ones], axis=1) for g in range(NSA_GROUPS)])
    rows = NSA_HPG * Q_BLOCK
    vmem = (2 * (2 * _nbytes(k_aug.shape, BF16) + _nbytes((Q_BLOCK, NSA_Q_W), BF16)
                 + _nbytes((Q_BLOCK, NSA_GROUPS * n_slc), BF16) + _nbytes((Q_BLOCK, NSA_Q_W), F32))
            + NSA_GROUPS * (n_win * _nbytes((rows, LANES), BF16) + 2 * _nbytes((rows, LANES), F32)
                            + 2 * _nbytes((rows, SEL_TILE), F32))
            + 4 * _nbytes((rows, SEL_TILE), F32))
    return pl.pallas_call(
        functools.partial(_sel_attn_kernel, win=win),
        grid=(s // Q_BLOCK,),
        in_specs=[pl.BlockSpec((Q_BLOCK, NSA_Q_W), lambda b: (b, 0)),
                  pl.BlockSpec(k_aug.shape, lambda b: (0, 0, 0)),
                  pl.BlockSpec(v_aug.shape, lambda b: (0, 0, 0)),
                  pl.BlockSpec((Q_BLOCK, NSA_GROUPS * n_slc), lambda b: (b, 0))],
        out_specs=pl.BlockSpec((Q_BLOCK, NSA_Q_W), lambda b: (b, 0)),
        out_shape=jax.ShapeDtypeStruct((s, NSA_Q_W), F32),
        scratch_shapes=[pltpu.VMEM((NSA_GROUPS, n_win, rows, LANES), BF16),
                        pltpu.VMEM((NSA_GROUPS, rows, LANES), F32), pltpu.VMEM((NSA_GROUPS, rows, LANES), F32),
                        pltpu.VMEM((NSA_GROUPS, 2, rows, SEL_TILE), F32)],
        compiler_params=_cparams(("parallel",), vmem),
        name="nsa_selected",
    )(rot_a, k_aug, v_aug, nsel)


def _win_attn_kernel(q_ref, k_ref, v_ref, o_ref, *, span):
    b = pl.program_id(0)
    rows = NSA_HPG * Q_BLOCK
    n_back = NSA_WINDOW // Q_BLOCK
    start = pl.multiple_of(jnp.maximum(b - n_back, 0) * Q_BLOCK, Q_BLOCK)
    k = k_ref[pl.ds(start, span), :]
    v = v_ref[pl.ds(start, span), :]
    t4 = b * Q_BLOCK + (_row_iota((rows, 1)) % Q_BLOCK)
    diff = t4 - (start + _lane_iota((1, span)))
    mask = (diff >= 0) & (diff < NSA_WINDOW)
    for g in range(NSA_GROUPS):
        qst = _stack_group_queries(q_ref, g * NSA_HPG * HEAD_DIM, g)
        e, den, _ = _softmax_rows(_qk(qst, k), mask, base2=True)
        o = jnp.dot(e.astype(BF16), v, preferred_element_type=F32) / jnp.maximum(den, 1e-30)
        o_ref[:, g * NSA_HPG * HEAD_DIM:(g + 1) * NSA_HPG * HEAD_DIM] = _unstack_group_outputs(o, g)


def _win_attn(q_rot, k_rot, v_plain, kw_col, vw_col):
    s = q_rot.shape[0]
    span = min(NSA_WINDOW + Q_BLOCK, s)
    rows = NSA_HPG * Q_BLOCK
    vmem = 2 * (2 * _nbytes((s, NSA_KV_W), BF16) + _nbytes((Q_BLOCK, NSA_Q_W), BF16)
                + _nbytes((Q_BLOCK, NSA_Q_W), F32)) + 8 * _nbytes((rows, span), F32)
    return pl.pallas_call(
        functools.partial(_win_attn_kernel, span=span),
        grid=(s // Q_BLOCK,),
        in_specs=[pl.BlockSpec((Q_BLOCK, NSA_Q_W), lambda b: (b, 0)),
                  pl.BlockSpec((s, NSA_KV_W), lambda b: (0, kw_col)),
                  pl.BlockSpec((s, NSA_KV_W), lambda b: (0, vw_col))],
        out_specs=pl.BlockSpec((Q_BLOCK, NSA_Q_W), lambda b: (b, 0)),
        out_shape=jax.ShapeDtypeStruct((s, NSA_Q_W), F32),
        compiler_params=_cparams(("parallel",), vmem),
        name="nsa_window",
    )(q_rot, k_rot, v_plain)


def _dil_attn_kernel(q_ref, kp_ref, kc_ref, vp_ref, vc_ref, o_ref, lse_ref, kbuf, vbuf, *, span, dil, nb):
    n = pl.program_id(0)
    unit = dil * Q_BLOCK
    kbuf[0:unit, :] = kp_ref[...]
    kbuf[unit:, :] = kc_ref[...]
    vbuf[0:unit, :] = vp_ref[...]
    vbuf[unit:, :] = vc_ref[...]
    qi = _row_iota((Q_BLOCK, 1)) + Q_BLOCK
    ki = _lane_iota((1, 2 * Q_BLOCK))
    delta = qi - ki
    band = (delta >= 0) & (delta <= span)
    lo = _lane_iota((Q_BLOCK, LANES)) < HEAD_DIM

    def item(idx, carry):
        j = idx // dil
        base = j * unit + idx % dil
        if dil > 1:
            q_rows, kv_rows = pl.ds(base, Q_BLOCK, stride=dil), pl.ds(base, 2 * Q_BLOCK, stride=dil)
        else:
            base = pl.multiple_of(base, Q_BLOCK)
            q_rows, kv_rows = pl.ds(base, Q_BLOCK), pl.ds(base, 2 * Q_BLOCK)
        mask = band & ((n * nb + j > 0) | (ki >= Q_BLOCK))
        q = q_ref[q_rows, :]
        k = kbuf[kv_rows, :].astype(BF16)
        v = vbuf[kv_rows, :].astype(BF16)
        outs, lses = [], []
        for e in range(2):
            qh = jnp.where(lo if e == 0 else ~lo, q, 0.0).astype(BF16)
            ex, den, m = _softmax_rows(_qk(qh, k), mask)
            den = jnp.maximum(den, 1e-30)
            outs.append(jnp.dot(ex.astype(BF16), v, preferred_element_type=F32) / den)
            lses.append(jnp.broadcast_to(m + jnp.log(den), (Q_BLOCK, LANES)))
        o_ref[q_rows, :] = jnp.where(lo, outs[0], outs[1])
        lse_ref[q_rows, :] = jnp.where(lo, lses[0], lses[1])
        return carry

    lax.fori_loop(0, nb * dil, item, 0, unroll=2)


def _dil_attn(qb, kb, vb, gidx, win, dil):
    s = qb.shape[0]
    unit = dil * Q_BLOCK
    tb = min(max(DIL_STEP_TOKENS, unit), s)
    assert s % tb == 0 and tb % unit == 0
    nb = tb // unit
    span = win // dil
    n_chunks = DIL_OUT_W // LANES
    cur = lambda n, c: (n, gidx * n_chunks + c)
    prev = lambda n, c: (jnp.maximum(n * nb - 1, 0), gidx * n_chunks + c)
    blk, pblk = (tb, LANES), (unit, LANES)
    vmem = (2 * (5 * _nbytes(blk, F32) + 2 * _nbytes(pblk, F32)) + 2 * _nbytes((tb + unit, LANES), F32)
            + 32 * _nbytes((Q_BLOCK, 2 * Q_BLOCK), F32))
    return pl.pallas_call(
        functools.partial(_dil_attn_kernel, span=span, dil=dil, nb=nb),
        grid=(s // tb, n_chunks),
        in_specs=[pl.BlockSpec(blk, cur), pl.BlockSpec(pblk, prev), pl.BlockSpec(blk, cur),
                  pl.BlockSpec(pblk, prev), pl.BlockSpec(blk, cur)],
        out_specs=[pl.BlockSpec(blk, lambda n, c: (n, c)), pl.BlockSpec(blk, lambda n, c: (n, c))],
        out_shape=[jax.ShapeDtypeStruct((s, DIL_OUT_W), F32)] * 2,
        scratch_shapes=[pltpu.VMEM((tb + unit, LANES), F32)] * 2,
        compiler_params=_cparams(("parallel", "parallel"), vmem),
        name=f"dilated_w{win}_d{dil}",
    )(qb, kb, kb, vb, vb)


def _sgu_kernel(uv_ref, g_ref, b_ref, ws_ref, bs_ref, o_ref):
    uv = jax.nn.gelu(uv_ref[...])
    u = uv[:, :SGU_WIDTH]
    v = _layer_norm(uv[:, SGU_WIDTH:], g_ref[...], b_ref[...]).astype(BF16)
    causal = _lane_iota((SGU_CHUNK, SGU_CHUNK)) <= _row_iota((SGU_CHUNK, SGU_CHUNK))
    bs = bs_ref[...]
    for n in range(uv.shape[0] // SGU_CHUNK):
        rs = slice(n * SGU_CHUNK, (n + 1) * SGU_CHUNK)
        for g in range(SGU_GROUPS):
            cs = slice(g * SGU_GROUP_CH, (g + 1) * SGU_GROUP_CH)
            ws = jnp.where(causal, ws_ref[g], 0.0).astype(BF16)
            sv = jnp.dot(ws, v[rs, cs], preferred_element_type=F32) + bs[:, cs]
            o_ref[rs, cs] = (u[rs, cs] * sv).astype(o_ref.dtype)


def _sgu(uv, ln_g, ln_b, w_s, b_s):
    s = uv.shape[0]
    tm = min(4 * SGU_CHUNK, s)
    assert s % tm == 0
    bs = jnp.repeat(b_s.T, SGU_GROUP_CH, axis=1)
    c2 = lambda i: (0, 0)
    vmem = 2 * (_nbytes((tm, 2 * SGU_WIDTH), F32) + _nbytes((tm, SGU_WIDTH), F32)) + 6 * _nbytes((tm, 2 * SGU_WIDTH), F32)
    return pl.pallas_call(
        _sgu_kernel,
        grid=(s // tm,),
        in_specs=[pl.BlockSpec((tm, 2 * SGU_WIDTH), lambda i: (i, 0)), pl.BlockSpec((1, SGU_WIDTH), c2),
                  pl.BlockSpec((1, SGU_WIDTH), c2), pl.BlockSpec(w_s.shape, lambda i: (0, 0, 0)),
                  pl.BlockSpec((SGU_CHUNK, SGU_WIDTH), c2)],
        out_specs=pl.BlockSpec((tm, SGU_WIDTH), lambda i: (i, 0)),
        out_shape=jax.ShapeDtypeStruct((s, SGU_WIDTH), F32),
        compiler_params=_cparams(("parallel",), vmem),
        name="sgu",
    )(uv, ln_g.reshape(1, -1), ln_b.reshape(1, -1), w_s, bs)


def _split_dot(x, w2):
    hi = x.astype(BF16)
    lo = (x - hi.astype(F32)).astype(BF16)
    return jnp.dot(jnp.concatenate([hi, lo], axis=1), w2, preferred_element_type=F32)


def _mix_kernel(x_ref, oc_ref, os_ref, ow_ref, ga_ref, eg_ref, d0_ref, d1_ref, d2_ref, l0_ref, l1_ref, l2_ref,
                sg_ref, gm_ref, wa_ref, wb_ref, wc_ref, wo_ref, g_ref, b_ref, o_ref, *, alpha):
    d = x_ref.shape[1]
    gates = _split_dot(jax.nn.sigmoid(ga_ref[...]), eg_ref[...])
    o_a = (gates[:, 0:NSA_Q_W] * oc_ref[...] + gates[:, NSA_Q_W:2 * NSA_Q_W] * os_ref[...]
           + gates[:, 2 * NSA_Q_W:3 * NSA_Q_W] * ow_ref[...])
    y_a = jnp.dot(o_a.astype(BF16), wa_ref[...], preferred_element_type=F32)
    l0, l1, l2 = l0_ref[...], l1_ref[...], l2_ref[...]
    lm = jnp.maximum(jnp.maximum(l0, l1), l2)
    e0, e1, e2 = jnp.exp(l0 - lm), jnp.exp(l1 - lm), jnp.exp(l2 - lm)
    den = e0 + e1 + e2
    o_b = (e0 / den) * d0_ref[...] + (e1 / den) * d1_ref[...] + (e2 / den) * d2_ref[...]
    y_b = jnp.dot(o_b.astype(BF16), wb_ref[...], preferred_element_type=F32)
    y_c = jnp.dot(sg_ref[...].astype(BF16), wc_ref[...], preferred_element_type=F32)
    gm = jax.nn.sigmoid(gm_ref[...])
    merged = gm[:, 0:d] * y_a + gm[:, d:2 * d] * y_b + gm[:, 2 * d:3 * d] * y_c
    mix = jnp.dot(merged.astype(BF16), wo_ref[...], preferred_element_type=F32)
    o_ref[...] = _layer_norm(alpha * x_ref[...] + mix, g_ref[...], b_ref[...])


def _mix(x, o_c, o_s, o_w, g_a, dil_o, dil_lse, sgu, g_m, wa, wb, wc, wo, g, b, alpha):
    s, d = x.shape
    tm = min(256, s)
    assert s % tm == 0
    col = np.arange(N_BRANCH * NSA_Q_W)
    head, br = (col % NSA_Q_W) // HEAD_DIM, col // NSA_Q_W
    eg1 = (np.arange(LANES)[:, None] == (head * N_BRANCH + br)[None, :]).astype(np.float32)
    eg = jnp.asarray(np.concatenate([eg1, eg1], axis=0), BF16)
    row = lambda i: (i, 0)
    const = lambda i: (0, 0)
    rspec = lambda w: pl.BlockSpec((tm, w), row)
    cspec = lambda a: pl.BlockSpec(a.shape, const)
    widths = [d, NSA_Q_W, NSA_Q_W, NSA_Q_W, LANES] + [DIL_OUT_W] * 6 + [SGU_WIDTH, N_BRANCH * d]
    vmem = (2 * sum(_nbytes((tm, w), F32) for w in widths) + 2 * _nbytes((tm, d), F32)
            + 2 * sum(_nbytes(a.shape, BF16) for a in (eg, wa, wb, wc, wo)) + 10 * _nbytes((tm, N_BRANCH * d), F32))
    return pl.pallas_call(
        functools.partial(_mix_kernel, alpha=alpha),
        grid=(s // tm,),
        in_specs=[rspec(d), rspec(NSA_Q_W), rspec(NSA_Q_W), rspec(NSA_Q_W), rspec(LANES), cspec(eg)]
                 + [rspec(DIL_OUT_W)] * 6 + [rspec(SGU_WIDTH), rspec(N_BRANCH * d)]
                 + [cspec(wa), cspec(wb), cspec(wc), cspec(wo), pl.BlockSpec((1, d), const), pl.BlockSpec((1, d), const)],
        out_specs=rspec(d),
        out_shape=jax.ShapeDtypeStruct((s, d), F32),
        compiler_params=_cparams(("parallel",), vmem),
        name="mixer_merge_deepnorm",
    )(x, o_c, o_s, o_w, g_a, eg, *dil_o, *dil_lse, sgu, g_m, wa, wb, wc, wo, g.reshape(1, d), b.reshape(1, d))


def _rope_tables(s):
    inv = 1.0 / (ROPE_THETA ** (jnp.arange(0, HEAD_DIM, 2, dtype=F32) / HEAD_DIM))
    ang = jnp.arange(s, dtype=F32)[:, None] * inv[None, :]
    cos, sin = jnp.cos(ang), jnp.sin(ang)
    reps = LANES // HEAD_DIM
    return jnp.tile(jnp.concatenate([cos, cos], axis=1), (1, reps)), jnp.tile(jnp.concatenate([-sin, sin], axis=1), (1, reps))


def kernel(x, ln_g, ln_b, ffn1_gate, ffn1_up, ffn1_down, ffn2_gate, ffn2_up, ffn2_down, w_in, phi_k_pos, phi_k_w1, phi_k_w2, phi_v_pos, phi_v_w1, phi_v_w2, sgu_ln_g, sgu_ln_b, sgu_w, sgu_b, w_branch_a, w_branch_b, w_branch_c, w_out):
    bsz, s, d = x.shape
    depth = ln_g.shape[0]
    alpha = float((2 * depth) ** 0.25)
    q_scale = float(HEAD_DIM ** -0.5)
    tabs = _rope_tables(s)
    sizes = [NSA_Q_W] + [NSA_KV_W] * 6 + [N_BRANCH * NSA_HEADS, DIL_W, DIL_W, DIL_W, 2 * SGU_WIDTH, N_BRANCH * d]
    offs = np.concatenate([[0], np.cumsum(sizes)]).tolist()
    outs = []
    for bi in range(bsz):
        h = x[bi]
        for l in range(depth):
            (w_qa, w_kc, w_vc, w_ks, w_vs, w_kw, w_vw, w_ga, w_qb, w_kb, w_vb, w_uv, w_gm) = [
                w_in[l][:, offs[i]:offs[i + 1]] for i in range(len(sizes))]
            w_ga = jnp.pad(w_ga, ((0, 0), (0, LANES - w_ga.shape[1])))
            groups = dict(qb=w_qb, kb=w_kb, vb=w_vb, k_rot=jnp.concatenate([w_ks, w_kw], axis=1),
                          kv_c=jnp.concatenate([w_kc, w_vc], axis=1), v_plain=jnp.concatenate([w_vs, w_vw], axis=1),
                          uv=w_uv, gm=w_gm, qa=w_qa, ga=w_ga)
            wp = jnp.concatenate(list(groups.values()), axis=1).astype(BF16)
            ends = np.cumsum([g.shape[1] for g in groups.values()]).tolist()
            cols = {name: (e - g.shape[1], e) for (name, g), e in zip(groups.items(), ends)}

            h, hb = _ffn(h, ffn1_gate[l].astype(BF16), ffn1_up[l].astype(BF16), ffn1_down[l].astype(BF16),
                         ln_g[l, 0], ln_b[l, 0], alpha)

            q_rot, q_u = _mm(hb, wp, cols["qa"], BF16, rope_tabs=tabs, scale=q_scale * LOG2E, plain_scale=q_scale)
            k_rot = _mm(hb, wp, cols["k_rot"], BF16, rope_tabs=tabs)
            qb = _mm(hb, wp, cols["qb"], F32, rope_tabs=tabs, scale=q_scale)
            kb = _mm(hb, wp, cols["kb"], F32, rope_tabs=tabs)
            vb = _mm(hb, wp, cols["vb"], F32)
            kv_c = _mm(hb, wp, cols["kv_c"], F32)
            v_plain = _mm(hb, wp, cols["v_plain"], BF16)
            uv = _mm(hb, wp, cols["uv"], F32)
            g_m = _mm(hb, wp, cols["gm"], F32)
            g_a = _mm(hb, wp, cols["ga"], F32)

            kc = _compress(kv_c, 0, phi_k_pos[l], phi_k_w1[l], phi_k_w2[l])
            vc = _compress(kv_c, 1, phi_v_pos[l], phi_v_w1[l], phi_v_w2[l])
            o_c, nsel = _cmp_select(q_u, kc, vc)
            o_s = _sel_attn(q_rot, k_rot[:, :NSA_KV_W], v_plain[:, :NSA_KV_W], nsel)
            o_w = _win_attn(q_rot, k_rot, v_plain, 1, 1)
            dil = [_dil_attn(qb, kb, vb, gi, win, dl) for gi, (win, dl) in enumerate(DIL_PAIRS)]
            sg = _sgu(uv, sgu_ln_g[l], sgu_ln_b[l], sgu_w[l], sgu_b[l])
            h = _mix(h, o_c, o_s, o_w, g_a, [o for o, _ in dil], [e for _, e in dil], sg, g_m,
                     w_branch_a[l].astype(BF16), w_branch_b[l].astype(BF16), w_branch_c[l].astype(BF16),
                     w_out[l].astype(BF16), ln_g[l, 1], ln_b[l, 1], alpha)

            h, _ = _ffn(h, ffn2_gate[l].astype(BF16), ffn2_up[l].astype(BF16), ffn2_down[l].astype(BF16),
                        ln_g[l, 2], ln_b[l, 2], alpha)
        outs.append(h)
    return jnp.stack(outs, axis=0)
```

```python
import functools

import numpy as np
import jax
import jax.numpy as jnp
from jax import lax
from jax.experimental import pallas as pl
from jax.experimental.pallas import tpu as pltpu

F32 = jnp.float32
BF16 = jnp.bfloat16

HEAD_DIM = 64
ROPE_THETA = 10000.0
LN_EPS = 1e-5
Q_BLOCK = 128
NSA_HEADS = 8
NSA_GROUPS = 2
NSA_HPG = NSA_HEADS // NSA_GROUPS
NSA_Q_W = NSA_HEADS * HEAD_DIM
NSA_KV_W = NSA_GROUPS * HEAD_DIM
CMP_LEN = 32
CMP_STRIDE = 16
SLC_BLOCK = 64
SLC_RATIO = SLC_BLOCK // CMP_STRIDE
SLC_TOPK = 16
CMP_Q_BLOCK = 256
DIL_STEP_TOKENS = 2048
SEL_TILE = 1024
NSA_WINDOW = 512
FORCE_SCORE = 1e6
DIL_PAIRS = ((128, 1), (512, 4), (2048, 16))
DIL_HPG = 4
DIL_HEADS = DIL_HPG * 3
DIL_W = DIL_HEADS * HEAD_DIM
DIL_OUT_W = DIL_HPG * HEAD_DIM
SGU_CHUNK = 128
SGU_GROUPS = 4
SGU_GROUP_CH = 128
SGU_WIDTH = SGU_GROUPS * SGU_GROUP_CH
N_BRANCH = 3

LANES = 128
V7X_VMEM_BYTES = 64 * 1024 * 1024
VMEM_CAP = 56 * 1024 * 1024

M_INIT = -1e30
MASKED = -(2.0 ** 101)
LOG2E = 1.4426950408889634


def _cparams(sem, vmem_bytes):
    return pltpu.CompilerParams(dimension_semantics=sem,
                                vmem_limit_bytes=int(min(max(vmem_bytes, 16 * 1024 * 1024), VMEM_CAP)))


def _nbytes(shape, dtype):
    return int(np.prod(shape)) * jnp.dtype(dtype).itemsize


def _layer_norm(y, g, b):
    mu = jnp.mean(y, axis=-1, keepdims=True)
    d = y - mu
    var = jnp.mean(d * d, axis=-1, keepdims=True)
    return d * lax.rsqrt(var + LN_EPS) * g + b


def _lane_iota(shape):
    return lax.broadcasted_iota(jnp.int32, shape, len(shape) - 1)


def _row_iota(shape):
    return lax.broadcasted_iota(jnp.int32, shape, 0)


def _ffn_kernel(x_ref, wg_ref, wu_ref, wd_ref, g_ref, b_ref, o_ref, ob_ref, *, alpha, fc):
    x = x_ref[...]
    xb = x.astype(BF16)
    n_f = wg_ref.shape[1]
    acc = jnp.zeros(x.shape, F32)
    for c in range(n_f // fc):
        sl = slice(c * fc, (c + 1) * fc)
        gate = jnp.dot(xb, wg_ref[:, sl], preferred_element_type=F32)
        up = jnp.dot(xb, wu_ref[:, sl], preferred_element_type=F32)
        h = (gate * jax.nn.sigmoid(gate)) * up
        acc = acc + jnp.dot(h.astype(BF16), wd_ref[sl, :], preferred_element_type=F32)
    out = _layer_norm(alpha * x + 0.5 * acc, g_ref[...], b_ref[...])
    o_ref[...] = out
    ob_ref[...] = out.astype(BF16)


def _ffn(x, wg, wu, wd, g, b, alpha):
    s, d = x.shape
    n_f = wg.shape[1]
    tm = min(512, s)
    fc = 256
    assert s % tm == 0 and n_f % fc == 0
    const = lambda i: (0, 0)
    row = lambda i: (i, 0)
    vmem = (2 * 2 * _nbytes((tm, d), F32) + 2 * _nbytes((tm, d), BF16)
            + 2 * 3 * _nbytes((d, n_f), BF16) + 6 * _nbytes((tm, d), F32))
    return pl.pallas_call(
        functools.partial(_ffn_kernel, alpha=alpha, fc=fc),
        grid=(s // tm,),
        in_specs=[pl.BlockSpec((tm, d), row), pl.BlockSpec((d, n_f), const), pl.BlockSpec((d, n_f), const),
                  pl.BlockSpec((n_f, d), const), pl.BlockSpec((1, d), const), pl.BlockSpec((1, d), const)],
        out_specs=[pl.BlockSpec((tm, d), row), pl.BlockSpec((tm, d), row)],
        out_shape=[jax.ShapeDtypeStruct((s, d), F32), jax.ShapeDtypeStruct((s, d), BF16)],
        compiler_params=_cparams(("parallel",), vmem),
        name="ffn_deepnorm",
    )(x, wg, wu, wd, g.reshape(1, d), b.reshape(1, d))


def _swap_halves_64(a):
    half = HEAD_DIM // 2
    first = (_lane_iota(a.shape) % HEAD_DIM) < half
    return jnp.where(first, pltpu.roll(a, LANES - half, 1), pltpu.roll(a, half, 1))


def _mm_kernel(*refs, rope, scale, plain_scale):
    if rope:
        x_ref, w_ref, c_ref, s_ref, o_ref = refs[:5]
    else:
        x_ref, w_ref, o_ref = refs
    acc = jnp.dot(x_ref[...], w_ref[...], preferred_element_type=F32)
    if plain_scale is not None:
        refs[5][...] = (acc * plain_scale).astype(refs[5].dtype)
    if rope:
        cos = c_ref[...]
        sin = s_ref[...]
        for c in range(acc.shape[1] // LANES):
            a = acc[:, c * LANES:(c + 1) * LANES]
            r = a * cos + _swap_halves_64(a) * sin
            if scale != 1.0:
                r = r * scale
            o_ref[:, c * LANES:(c + 1) * LANES] = r.astype(o_ref.dtype)
    else:
        if scale != 1.0:
            acc = acc * scale
        o_ref[...] = acc.astype(o_ref.dtype)


def _mm(xb, w, cols, out_dtype, *, rope_tabs=None, scale=1.0, plain_scale=None):
    s, k = xb.shape
    col0, n = cols[0], cols[1] - cols[0]
    tm = min(1024, s)
    tn = next(c for c in (1024, 768, 512, 256, LANES) if n % c == 0 and col0 % c == 0)
    assert s % tm == 0 and (plain_scale is None or rope_tabs is not None)
    jb = col0 // tn
    n_out = 1 if plain_scale is None else 2
    in_specs = [pl.BlockSpec((tm, k), lambda i, j: (i, 0)), pl.BlockSpec((k, tn), lambda i, j: (0, jb + j))]
    args = [xb, w]
    if rope_tabs is not None:
        in_specs += [pl.BlockSpec((tm, LANES), lambda i, j: (i, 0))] * 2
        args += list(rope_tabs)
    vmem = 2 * (_nbytes((tm, k), BF16) + _nbytes((k, tn), BF16) + n_out * _nbytes((tm, tn), out_dtype)
                + 2 * _nbytes((tm, LANES), F32)) + 3 * _nbytes((tm, tn), F32)
    out = pl.pallas_call(
        functools.partial(_mm_kernel, rope=rope_tabs is not None, scale=scale, plain_scale=plain_scale),
        grid=(s // tm, n // tn),
        in_specs=in_specs,
        out_specs=[pl.BlockSpec((tm, tn), lambda i, j: (i, j))] * n_out,
        out_shape=[jax.ShapeDtypeStruct((s, n), out_dtype)] * n_out,
        compiler_params=_cparams(("parallel", "parallel"), vmem),
        name="proj_rope" if rope_tabs is not None else "proj",
    )(*args)
    return out[0] if n_out == 1 else out


def _compress_kernel(x_ref, pos_ref, w1_ref, w2_ref, o_ref):
    n_cmp = o_ref.shape[0]
    hidden = w1_ref.shape[3]
    ha = [jnp.zeros((n_cmp, hidden), F32) for _ in range(NSA_GROUPS)]
    hb = [jnp.zeros((n_cmp, hidden), F32) for _ in range(NSA_GROUPS)]
    tail = [jnp.zeros((8, hidden), F32) for _ in range(NSA_GROUPS)]
    for j in range(CMP_STRIDE):
        xj = x_ref[pl.ds(j, n_cmp, stride=CMP_STRIDE), :]
        pa = pos_ref[j:j + 1, :]
        pb = pos_ref[CMP_STRIDE + j:CMP_STRIDE + j + 1, :]
        xa = (xj + pa).astype(BF16)
        xb = (xj + pb).astype(BF16)
        pb8 = jnp.broadcast_to(pb, (8, pb.shape[1])).astype(BF16)
        for g in range(NSA_GROUPS):
            ha[g] = ha[g] + jnp.dot(xa, w1_ref[g, j], preferred_element_type=F32)
            hb[g] = hb[g] + jnp.dot(xb, w1_ref[g, CMP_STRIDE + j], preferred_element_type=F32)
            tail[g] = tail[g] + jnp.dot(pb8, w1_ref[g, CMP_STRIDE + j], preferred_element_type=F32)
    last = _row_iota((n_cmp, 1)) == n_cmp - 1
    outs = []
    for g in range(NSA_GROUPS):
        hb_next = jnp.where(last, tail[g][0:1, :], pltpu.roll(hb[g], n_cmp - 1, 0))
        h = jax.nn.gelu(ha[g] + hb_next)
        outs.append(jnp.dot(h.astype(BF16), w2_ref[...], preferred_element_type=F32))
    lo = _lane_iota(outs[0].shape) < HEAD_DIM
    o_ref[...] = jnp.where(lo, outs[0], outs[1]).astype(o_ref.dtype)


def _compress(x, col, pos, w1, w2):
    s = x.shape[0]
    n_cmp = s // CMP_STRIDE
    hidden = w1.shape[1]
    pos2 = jnp.concatenate([pos] * NSA_GROUPS, axis=1)
    w1r = w1.reshape(CMP_LEN, HEAD_DIM, hidden).astype(BF16)
    zero = jnp.zeros_like(w1r)
    w1g = jnp.stack([jnp.concatenate([w1r if gg == g else zero for gg in range(NSA_GROUPS)], axis=1)
                     for g in range(NSA_GROUPS)])
    w2d = jnp.concatenate([w2, w2], axis=1).astype(BF16)
    vmem = 2 * (_nbytes((s, NSA_KV_W), F32) + _nbytes(w1g.shape, BF16)) + 16 * _nbytes((n_cmp, hidden), F32)
    return pl.pallas_call(
        _compress_kernel,
        grid=(1,),
        in_specs=[pl.BlockSpec((s, NSA_KV_W), lambda i: (0, col)), pl.BlockSpec(pos2.shape, lambda i: (0, 0)),
                  pl.BlockSpec(w1g.shape, lambda i: (0, 0, 0, 0)), pl.BlockSpec(w2d.shape, lambda i: (0, 0))],
        out_specs=pl.BlockSpec((n_cmp, NSA_KV_W), lambda i: (0, 0)),
        out_shape=jax.ShapeDtypeStruct((n_cmp, NSA_KV_W), BF16),
        compiler_params=_cparams(("arbitrary",), vmem),
        name="nsa_compress",
    )(x, pos2, w1g, w2d)


def _stack_group_queries(q_ref, col0, kv_lane_group):
    parts = []
    for h in range(NSA_HPG):
        c = col0 + (h // 2) * LANES
        qc = q_ref[:, c:c + LANES].astype(F32)
        e = h % 2
        lane = _lane_iota(qc.shape)
        mine = (lane >= e * HEAD_DIM) & (lane < (e + 1) * HEAD_DIM)
        qh = jnp.where(mine, qc, 0.0)
        if e != kv_lane_group:
            qh = pltpu.roll(qh, HEAD_DIM, 1)
        parts.append(qh.astype(BF16))
    return jnp.concatenate(parts, axis=0)


def _unstack_group_outputs(o, kv_lane_group):
    chunks = []
    tq = o.shape[0] // NSA_HPG
    for c in range(NSA_HPG // 2):
        halves = []
        for e in range(2):
            h = 2 * c + e
            oh = o[h * tq:(h + 1) * tq, :]
            if e != kv_lane_group:
                oh = pltpu.roll(oh, HEAD_DIM, 1)
            halves.append(oh)
        lo = _lane_iota(halves[0].shape) < HEAD_DIM
        chunks.append(jnp.where(lo, halves[0], halves[1]))
    return jnp.concatenate(chunks, axis=1)


def _qk(q, k):
    return lax.dot_general(q, k, (((1,), (1,)), ((), ())), preferred_element_type=F32)


def _softmax_rows(s, mask, base2=False):
    sm = jnp.where(mask, s, MASKED)
    m = jnp.maximum(jnp.max(sm, axis=-1, keepdims=True), M_INIT)
    e = jnp.exp2(sm - m) if base2 else jnp.exp(sm - m)
    return e, jnp.sum(e, axis=-1, keepdims=True), m


def _cmp_select_kernel(q_ref, kc_ref, vc_ref, mmap_ref, oc_ref, nsel_ref, *, top_k, n_cls):
    b = pl.program_id(0)
    n_cmp = kc_ref.shape[0]
    n_slc = mmap_ref.shape[0]
    tq = q_ref.shape[0]
    rows = NSA_HPG * tq
    gw = NSA_HPG * HEAD_DIM
    t4 = b * tq + (_row_iota((rows, 1)) % tq)
    t = b * tq + _lane_iota((1, tq))

    def variant(nc, nb):
        valid = _lane_iota((1, nc)) * CMP_STRIDE + (CMP_LEN - 1) <= t4
        blk = _row_iota((nb, 1))
        cur = t // SLC_BLOCK
        forced = (blk == 0) | (blk == cur) | (blk == cur - 1)
        future = blk * SLC_BLOCK > t
        blk_f = blk.astype(F32)
        kc = kc_ref[0:nc, :]
        vc = vc_ref[0:nc, :]
        mmap_t = mmap_ref[0:nb, 0:nc]
        for g in range(NSA_GROUPS):
            qst = _stack_group_queries(q_ref, g * gw, g)
            e, den, _ = _softmax_rows(_qk(qst, kc), valid)
            p = e / jnp.maximum(den, 1e-30)
            o = jnp.dot(p.astype(BF16), vc, preferred_element_type=F32)
            oc_ref[:, g * gw:(g + 1) * gw] = _unstack_group_outputs(o, g)
            psum = p[0:tq]
            for h in range(1, NSA_HPG):
                psum = psum + p[h * tq:(h + 1) * tq]
            hi = psum.astype(BF16)
            r1 = psum - hi.astype(F32)
            mid = r1.astype(BF16)
            low = (r1 - mid.astype(F32)).astype(BF16)
            p_slc = _qk(mmap_t, hi) + _qk(mmap_t, mid) + _qk(mmap_t, low)
            score = jnp.where(forced, FORCE_SCORE, jnp.where(future, -1.0, p_slc))
            for _ in range(top_k):
                mx = jnp.max(score, axis=0, keepdims=True)
                first = jnp.min(jnp.where(score == mx, blk_f, float(nb)), axis=0, keepdims=True)
                score = jnp.where(blk_f == first, -2.0, score)
            not_picked = jnp.where(score == -2.0, 0.0, 1.0)
            nsel_ref[:, g * n_slc:g * n_slc + nb] = not_picked.T.astype(nsel_ref.dtype)
            if nb < n_slc:
                nsel_ref[:, g * n_slc + nb:(g + 1) * n_slc] = jnp.ones((tq, n_slc - nb), nsel_ref.dtype)

    for c in range(n_cls):
        nc = (c + 1) * (n_cmp // n_cls)
        nb = min(n_slc, -(-(nc // SLC_RATIO) // LANES) * LANES)
        lo_b = c * (nc // (c + 1)) // (tq // CMP_STRIDE)
        hi_b = nc // (tq // CMP_STRIDE)
        pl.when((b >= lo_b) & (b < hi_b))(functools.partial(variant, nc, nb))


def _cmp_select(q_u, kc, vc):
    s = q_u.shape[0]
    n_cmp, n_slc = s // CMP_STRIDE, s // SLC_BLOCK
    top_k = min(SLC_TOPK, n_slc)
    n_cls = max(n_cmp // (2 * LANES), 1)
    tq = min(CMP_Q_BLOCK, s)
    assert s % tq == 0 and n_cmp % n_cls == 0 and (n_cmp // n_cls) % (tq // CMP_STRIDE) == 0
    m = np.arange(n_cmp)[:, None]
    j = np.arange(n_slc)[None, :]
    mmap = jnp.asarray(((m >= SLC_RATIO * j - 1) & (m <= SLC_RATIO * j + SLC_RATIO - 1)).astype(np.float32).T, BF16)
    const = lambda b: (0, 0)
    rows = NSA_HPG * tq
    vmem = 2 * (2 * _nbytes((n_cmp, NSA_KV_W), BF16) + _nbytes((n_cmp, n_slc), BF16)
                + _nbytes((tq, NSA_Q_W), F32) * 2) + 6 * _nbytes((rows, n_cmp), F32)
    return pl.pallas_call(
        functools.partial(_cmp_select_kernel, top_k=top_k, n_cls=n_cls),
        grid=(s // tq,),
        in_specs=[pl.BlockSpec((tq, NSA_Q_W), lambda b: (b, 0)), pl.BlockSpec((n_cmp, NSA_KV_W), const),
                  pl.BlockSpec((n_cmp, NSA_KV_W), const), pl.BlockSpec((n_slc, n_cmp), const)],
        out_specs=[pl.BlockSpec((tq, NSA_Q_W), lambda b: (b, 0)),
                   pl.BlockSpec((tq, NSA_GROUPS * n_slc), lambda b: (b, 0))],
        out_shape=[jax.ShapeDtypeStruct((s, NSA_Q_W), F32), jax.ShapeDtypeStruct((s, NSA_GROUPS * n_slc), BF16)],
        compiler_params=_cparams(("parallel",), vmem),
        name="nsa_compressed_select",
    )(q_u, kc, vc, mmap)


def _sel_attn_kernel(q_ref, k_ref, v_ref, nsel_ref, o_ref, qa_ref, m_ref, acc_ref, s_ref, *, win):
    b = pl.program_id(0)
    n_slc = nsel_ref.shape[1] // NSA_GROUPS
    n_win = n_slc // win
    tiles_per_win = win * SLC_BLOCK // SEL_TILE
    gw = NSA_HPG * HEAD_DIM
    n_chunks = SEL_TILE // LANES
    rows = NSA_HPG * Q_BLOCK
    last = (b * Q_BLOCK) // SEL_TILE
    t = b * Q_BLOCK + _row_iota((Q_BLOCK, 1))
    lane = _lane_iota((Q_BLOCK, LANES))
    head_lanes = lane < HEAD_DIM
    flag_lanes = (lane >= HEAD_DIM) & (lane < HEAD_DIM + win)

    nsel_all = nsel_ref[...].astype(F32)
    if nsel_all.shape[1] < LANES:
        nsel_all = jnp.concatenate([nsel_all, jnp.zeros((Q_BLOCK, LANES - nsel_all.shape[1]), F32)], axis=1)
    for g in range(NSA_GROUPS):
        heads = []
        for h in range(NSA_HPG):
            c = g * gw + (h // 2) * LANES
            qc = q_ref[:, c:c + LANES].astype(F32)
            heads.append(qc if h % 2 == 0 else pltpu.roll(qc, HEAD_DIM, 1))
        for w in range(n_win):
            a = g * n_slc + w * win
            fl = nsel_all[:, (a // LANES) * LANES:(a // LANES + 1) * LANES]
            shift = (HEAD_DIM - a % LANES) % LANES
            if shift:
                fl = pltpu.roll(fl, shift, 1)
            fl = jnp.where(flag_lanes, fl, 0.0)
            for h in range(NSA_HPG):
                qa_ref[g, w, h * Q_BLOCK:(h + 1) * Q_BLOCK, :] = jnp.where(head_lanes, heads[h], fl).astype(BF16)
    m_ref[...] = jnp.full(m_ref.shape, M_INIT, F32)
    acc_ref[...] = jnp.zeros(acc_ref.shape, F32)

    def scores(kt, slot):
        off = pl.multiple_of(kt * SEL_TILE, SEL_TILE)
        for g in range(NSA_GROUPS):
            s_ref[g, slot] = _qk(qa_ref[g, kt // tiles_per_win], k_ref[g, pl.ds(off, SEL_TILE), :])

    def update(kt, slot, causal):
        off = pl.multiple_of(kt * SEL_TILE, SEL_TILE)
        for g in range(NSA_GROUPS):
            s = s_ref[g, slot]
            if causal:
                keep = off + _lane_iota((1, SEL_TILE)) <= t
                s = jnp.where(jnp.concatenate([keep] * NSA_HPG, axis=0), s, MASKED)
            m_old = m_ref[g]
            m_new = jnp.maximum(m_old, jnp.max(s, axis=-1, keepdims=True))
            alpha = jnp.exp2(m_old - m_new)
            p = jnp.concatenate([jnp.exp2(s[:, c * LANES:(c + 1) * LANES] - m_new).astype(BF16)
                                 for c in range(n_chunks)], axis=1)
            v = v_ref[g, pl.ds(off, SEL_TILE), :]
            acc_ref[g] = alpha * acc_ref[g] + jnp.dot(p, v, preferred_element_type=F32)
            m_ref[g] = m_new

    def finish():
        for g in range(NSA_GROUPS):
            acc = acc_ref[g]
            o = acc / jnp.maximum(acc[:, HEAD_DIM:HEAD_DIM + 1], 1e-30)
            for c in range(NSA_HPG // 2):
                even = o[(2 * c) * Q_BLOCK:(2 * c + 1) * Q_BLOCK, :]
                odd = pltpu.roll(o[(2 * c + 1) * Q_BLOCK:(2 * c + 2) * Q_BLOCK, :], HEAD_DIM, 1)
                o_ref[:, g * gw + c * LANES:g * gw + (c + 1) * LANES] = jnp.where(head_lanes, even, odd)

    scores(0, 0)

    def pair(j, carry):
        scores(2 * j + 1, 1)
        update(2 * j, 0, False)
        scores(2 * j + 2, 0)
        update(2 * j + 1, 1, False)
        return carry

    lax.fori_loop(0, last // 2, pair, 0)

    @pl.when(last % 2 == 1)
    def _():
        scores(last, 1)
        update(last - 1, 0, False)
        update(last, 1, True)
        finish()

    @pl.when(last % 2 == 0)
    def _():
        update(last, 0, True)
        finish()


def _sel_attn(rot_a, k_s, v_s, nsel):
    s = rot_a.shape[0]
    n_slc = s // SLC_BLOCK
    win = min(n_slc, LANES - HEAD_DIM)
    assert s % SEL_TILE == 0 and n_slc % win == 0 and (win * SLC_BLOCK) % SEL_TILE == 0
    n_win = n_slc // win
    spare = LANES - HEAD_DIM
    blk = (np.arange(s) // SLC_BLOCK) % win
    marks = jnp.asarray(np.where(np.arange(spare)[None, :] == blk[:, None], MASKED, 0.0).astype(np.float32), BF16)
    k_aug = jnp.stack([jnp.concatenate([k_s[:, g * HEAD_DIM:(g + 1) * HEAD_DIM], marks], axis=1)
                       for g in range(NSA_GROUPS)])
    v_aug = _ones_augmented_values(v_s)
    rows = NSA_HPG * Q_BLOCK
    vmem = (2 * (2 * _nbytes(k_aug.shape, BF16) + _nbytes((Q_BLOCK, NSA_Q_W), BF16)
                 + _nbytes((Q_BLOCK, NSA_GROUPS * n_slc), BF16) + _nbytes((Q_BLOCK, NSA_Q_W), F32))
            + NSA_GROUPS * (n_win * _nbytes((rows, LANES), BF16) + 2 * _nbytes((rows, LANES), F32)
                            + 2 * _nbytes((rows, SEL_TILE), F32))
            + 4 * _nbytes((rows, SEL_TILE), F32))
    return pl.pallas_call(
        functools.partial(_sel_attn_kernel, win=win),
        grid=(s // Q_BLOCK,),
        in_specs=[pl.BlockSpec((Q_BLOCK, NSA_Q_W), lambda b: (b, 0)),
                  pl.BlockSpec(k_aug.shape, lambda b: (0, 0, 0), pipeline_mode=pl.Buffered(1)),
                  pl.BlockSpec(v_aug.shape, lambda b: (0, 0, 0), pipeline_mode=pl.Buffered(1)),
                  pl.BlockSpec((Q_BLOCK, NSA_GROUPS * n_slc), lambda b: (b, 0))],
        out_specs=pl.BlockSpec((Q_BLOCK, NSA_Q_W), lambda b: (b, 0)),
        out_shape=jax.ShapeDtypeStruct((s, NSA_Q_W), F32),
        scratch_shapes=[pltpu.VMEM((NSA_GROUPS, n_win, rows, LANES), BF16),
                        pltpu.VMEM((NSA_GROUPS, rows, LANES), F32), pltpu.VMEM((NSA_GROUPS, rows, LANES), F32),
                        pltpu.VMEM((NSA_GROUPS, 2, rows, SEL_TILE), F32)],
        compiler_params=_cparams(("parallel",), vmem),
        name="nsa_selected",
    )(rot_a, k_aug, v_aug, nsel)


def _win_attn_kernel(q_ref, k_ref, v_ref, o_ref, *, span):
    b = pl.program_id(0)
    rows = NSA_HPG * Q_BLOCK
    n_back = NSA_WINDOW // Q_BLOCK
    start = pl.multiple_of(jnp.maximum(b - n_back, 0) * Q_BLOCK, Q_BLOCK)
    k = k_ref[pl.ds(start, span), :]
    t = b * Q_BLOCK + _row_iota((Q_BLOCK, 1))
    diff = t - (start + _lane_iota((1, span)))
    bias = jnp.where((diff >= 0) & (diff < NSA_WINDOW), 0.0, MASKED)
    bias = jnp.concatenate([bias] * NSA_HPG, axis=0)
    head_lanes = _lane_iota((Q_BLOCK, LANES)) < HEAD_DIM
    gw = NSA_HPG * HEAD_DIM
    for g in range(NSA_GROUPS):
        qst = _stack_group_queries(q_ref, g * gw, g)
        s = _qk(qst, k) + bias
        m = jnp.maximum(jnp.max(s, axis=-1, keepdims=True), M_INIT)
        e = jnp.exp2(s - m).astype(BF16)
        acc = jnp.dot(e, v_ref[g, pl.ds(start, span), :], preferred_element_type=F32)
        o = acc / jnp.maximum(acc[:, HEAD_DIM:HEAD_DIM + 1], 1e-30)
        for c in range(NSA_HPG // 2):
            even = o[(2 * c) * Q_BLOCK:(2 * c + 1) * Q_BLOCK, :]
            odd = pltpu.roll(o[(2 * c + 1) * Q_BLOCK:(2 * c + 2) * Q_BLOCK, :], HEAD_DIM, 1)
            o_ref[:, g * gw + c * LANES:g * gw + (c + 1) * LANES] = jnp.where(head_lanes, even, odd)


def _ones_augmented_values(v):
    s = v.shape[0]
    spare = LANES - HEAD_DIM
    ones = jnp.asarray((np.arange(spare)[None, :] == 0).astype(np.float32) * np.ones((s, 1), np.float32), BF16)
    return jnp.stack([jnp.concatenate([v[:, g * HEAD_DIM:(g + 1) * HEAD_DIM], ones], axis=1)
                      for g in range(NSA_GROUPS)])


def _win_attn(q_rot, k_rot, v_w, kw_col):
    s = q_rot.shape[0]
    span = min(NSA_WINDOW + Q_BLOCK, s)
    rows = NSA_HPG * Q_BLOCK
    v_aug = _ones_augmented_values(v_w)
    vmem = 2 * (_nbytes((s, NSA_KV_W), BF16) + _nbytes(v_aug.shape, BF16) + _nbytes((Q_BLOCK, NSA_Q_W), BF16)
                + _nbytes((Q_BLOCK, NSA_Q_W), F32)) + 8 * _nbytes((rows, span), F32)
    return pl.pallas_call(
        functools.partial(_win_attn_kernel, span=span),
        grid=(s // Q_BLOCK,),
        in_specs=[pl.BlockSpec((Q_BLOCK, NSA_Q_W), lambda b: (b, 0)),
                  pl.BlockSpec((s, NSA_KV_W), lambda b: (0, kw_col)),
                  pl.BlockSpec(v_aug.shape, lambda b: (0, 0, 0))],
        out_specs=pl.BlockSpec((Q_BLOCK, NSA_Q_W), lambda b: (b, 0)),
        out_shape=jax.ShapeDtypeStruct((s, NSA_Q_W), F32),
        compiler_params=_cparams(("parallel",), vmem),
        name="nsa_window",
    )(q_rot, k_rot, v_aug)


def _dil_attn_kernel(q_ref, kp_ref, kc_ref, vp_ref, vc_ref, o_ref, lse_ref, kbuf, vbuf, *, span, dil, nb):
    n = pl.program_id(0)
    unit = dil * Q_BLOCK
    kbuf[0:unit, :] = kp_ref[...]
    kbuf[unit:, :] = kc_ref[...]
    vbuf[0:unit, :] = vp_ref[...]
    vbuf[unit:, :] = vc_ref[...]
    qi = _row_iota((Q_BLOCK, 1)) + Q_BLOCK
    ki = _lane_iota((1, 2 * Q_BLOCK))
    delta = qi - ki
    band_bias = jnp.where((delta >= 0) & (delta <= span), 0.0, MASKED)
    lo = _lane_iota((Q_BLOCK, LANES)) < HEAD_DIM

    def item(idx, carry):
        j = idx // dil
        base = j * unit + idx % dil
        if dil > 1:
            q_rows, kv_rows = pl.ds(base, Q_BLOCK, stride=dil), pl.ds(base, 2 * Q_BLOCK, stride=dil)
        else:
            base = pl.multiple_of(base, Q_BLOCK)
            q_rows, kv_rows = pl.ds(base, Q_BLOCK), pl.ds(base, 2 * Q_BLOCK)
        bias = jnp.where((n * nb + j == 0) & (ki < Q_BLOCK), MASKED, band_bias)
        q = q_ref[q_rows, :]
        k = kbuf[kv_rows, :].astype(BF16)
        v = vbuf[kv_rows, :].astype(BF16)
        outs, lses = [], []
        for e in range(2):
            qh = jnp.where(lo if e == 0 else ~lo, q, 0.0).astype(BF16)
            s = _qk(qh, k) + bias
            m = jnp.maximum(jnp.max(s, axis=-1, keepdims=True), M_INIT)
            ex = jnp.exp(s - m)
            den = jnp.maximum(jnp.sum(ex, axis=-1, keepdims=True), 1e-30)
            outs.append(jnp.dot(ex.astype(BF16), v, preferred_element_type=F32) / den)
            lses.append(jnp.broadcast_to(m + jnp.log(den), (Q_BLOCK, LANES)))
        o_ref[q_rows, :] = jnp.where(lo, outs[0], outs[1])
        lse_ref[q_rows, :] = jnp.where(lo, lses[0], lses[1])
        return carry

    lax.fori_loop(0, nb * dil, item, 0, unroll=4)


def _dil_attn(qb, kb, vb, gidx, win, dil):
    s = qb.shape[0]
    unit = dil * Q_BLOCK
    tb = min(max(DIL_STEP_TOKENS, unit), s)
    assert s % tb == 0 and tb % unit == 0
    nb = tb // unit
    span = win // dil
    n_chunks = DIL_OUT_W // LANES
    cur = lambda n, c: (n, gidx * n_chunks + c)
    prev = lambda n, c: (jnp.maximum(n * nb - 1, 0), gidx * n_chunks + c)
    blk, pblk = (tb, LANES), (unit, LANES)
    vmem = (2 * (5 * _nbytes(blk, F32) + 2 * _nbytes(pblk, F32)) + 2 * _nbytes((tb + unit, LANES), F32)
            + 32 * _nbytes((Q_BLOCK, 2 * Q_BLOCK), F32))
    return pl.pallas_call(
        functools.partial(_dil_attn_kernel, span=span, dil=dil, nb=nb),
        grid=(s // tb, n_chunks),
        in_specs=[pl.BlockSpec(blk, cur), pl.BlockSpec(pblk, prev), pl.BlockSpec(blk, cur),
                  pl.BlockSpec(pblk, prev), pl.BlockSpec(blk, cur)],
        out_specs=[pl.BlockSpec(blk, lambda n, c: (n, c)), pl.BlockSpec(blk, lambda n, c: (n, c))],
        out_shape=[jax.ShapeDtypeStruct((s, DIL_OUT_W), F32)] * 2,
        scratch_shapes=[pltpu.VMEM((tb + unit, LANES), F32)] * 2,
        compiler_params=_cparams(("parallel", "parallel"), vmem),
        name=f"dilated_w{win}_d{dil}",
    )(qb, kb, kb, vb, vb)


def _sgu_kernel(uv_ref, g_ref, b_ref, ws_ref, bs_ref, o_ref):
    uv = jax.nn.gelu(uv_ref[...])
    u = uv[:, :SGU_WIDTH]
    v = _layer_norm(uv[:, SGU_WIDTH:], g_ref[...], b_ref[...]).astype(BF16)
    causal = _lane_iota((SGU_CHUNK, SGU_CHUNK)) <= _row_iota((SGU_CHUNK, SGU_CHUNK))
    bs = bs_ref[...]
    for n in range(uv.shape[0] // SGU_CHUNK):
        rs = slice(n * SGU_CHUNK, (n + 1) * SGU_CHUNK)
        for g in range(SGU_GROUPS):
            cs = slice(g * SGU_GROUP_CH, (g + 1) * SGU_GROUP_CH)
            ws = jnp.where(causal, ws_ref[g], 0.0).astype(BF16)
            sv = jnp.dot(ws, v[rs, cs], preferred_element_type=F32) + bs[:, cs]
            o_ref[rs, cs] = (u[rs, cs] * sv).astype(o_ref.dtype)


def _sgu(uv, ln_g, ln_b, w_s, b_s):
    s = uv.shape[0]
    tm = min(4 * SGU_CHUNK, s)
    assert s % tm == 0
    bs = jnp.repeat(b_s.T, SGU_GROUP_CH, axis=1)
    c2 = lambda i: (0, 0)
    vmem = 2 * (_nbytes((tm, 2 * SGU_WIDTH), F32) + _nbytes((tm, SGU_WIDTH), F32)) + 6 * _nbytes((tm, 2 * SGU_WIDTH), F32)
    return pl.pallas_call(
        _sgu_kernel,
        grid=(s // tm,),
        in_specs=[pl.BlockSpec((tm, 2 * SGU_WIDTH), lambda i: (i, 0)), pl.BlockSpec((1, SGU_WIDTH), c2),
                  pl.BlockSpec((1, SGU_WIDTH), c2), pl.BlockSpec(w_s.shape, lambda i: (0, 0, 0)),
                  pl.BlockSpec((SGU_CHUNK, SGU_WIDTH), c2)],
        out_specs=pl.BlockSpec((tm, SGU_WIDTH), lambda i: (i, 0)),
        out_shape=jax.ShapeDtypeStruct((s, SGU_WIDTH), F32),
        compiler_params=_cparams(("parallel",), vmem),
        name="sgu",
    )(uv, ln_g.reshape(1, -1), ln_b.reshape(1, -1), w_s, bs)


def _split_dot(x, w2):
    hi = x.astype(BF16)
    lo = (x - hi.astype(F32)).astype(BF16)
    return jnp.dot(jnp.concatenate([hi, lo], axis=1), w2, preferred_element_type=F32)


def _mix_kernel(x_ref, oc_ref, os_ref, ow_ref, ga_ref, eg_ref, d0_ref, d1_ref, d2_ref, l0_ref, l1_ref, l2_ref,
                sg_ref, gm_ref, wa_ref, wb_ref, wc_ref, wo_ref, g_ref, b_ref, o_ref, *, alpha):
    d = x_ref.shape[1]
    gates = _split_dot(jax.nn.sigmoid(ga_ref[...]), eg_ref[...])
    o_a = (gates[:, 0:NSA_Q_W] * oc_ref[...] + gates[:, NSA_Q_W:2 * NSA_Q_W] * os_ref[...]
           + gates[:, 2 * NSA_Q_W:3 * NSA_Q_W] * ow_ref[...])
    y_a = jnp.dot(o_a.astype(BF16), wa_ref[...], preferred_element_type=F32)
    l0, l1, l2 = l0_ref[...], l1_ref[...], l2_ref[...]
    lm = jnp.maximum(jnp.maximum(l0, l1), l2)
    e0, e1, e2 = jnp.exp(l0 - lm), jnp.exp(l1 - lm), jnp.exp(l2 - lm)
    den = e0 + e1 + e2
    o_b = (e0 / den) * d0_ref[...] + (e1 / den) * d1_ref[...] + (e2 / den) * d2_ref[...]
    y_b = jnp.dot(o_b.astype(BF16), wb_ref[...], preferred_element_type=F32)
    y_c = jnp.dot(sg_ref[...].astype(BF16), wc_ref[...], preferred_element_type=F32)
    gm = jax.nn.sigmoid(gm_ref[...])
    merged = gm[:, 0:d] * y_a + gm[:, d:2 * d] * y_b + gm[:, 2 * d:3 * d] * y_c
    mix = jnp.dot(merged.astype(BF16), wo_ref[...], preferred_element_type=F32)
    o_ref[...] = _layer_norm(alpha * x_ref[...] + mix, g_ref[...], b_ref[...])


def _mix(x, o_c, o_s, o_w, g_a, dil_o, dil_lse, sgu, g_m, wa, wb, wc, wo, g, b, alpha):
    s, d = x.shape
    tm = min(256, s)
    assert s % tm == 0
    col = np.arange(N_BRANCH * NSA_Q_W)
    head, br = (col % NSA_Q_W) // HEAD_DIM, col // NSA_Q_W
    eg1 = (np.arange(LANES)[:, None] == (head * N_BRANCH + br)[None, :]).astype(np.float32)
    eg = jnp.asarray(np.concatenate([eg1, eg1], axis=0), BF16)
    row = lambda i: (i, 0)
    const = lambda i: (0, 0)
    rspec = lambda w: pl.BlockSpec((tm, w), row)
    cspec = lambda a: pl.BlockSpec(a.shape, const)
    widths = [d, NSA_Q_W, NSA_Q_W, NSA_Q_W, LANES] + [DIL_OUT_W] * 6 + [SGU_WIDTH, N_BRANCH * d]
    vmem = (2 * sum(_nbytes((tm, w), F32) for w in widths) + 2 * _nbytes((tm, d), F32)
            + 2 * sum(_nbytes(a.shape, BF16) for a in (eg, wa, wb, wc, wo)) + 10 * _nbytes((tm, N_BRANCH * d), F32))
    return pl.pallas_call(
        functools.partial(_mix_kernel, alpha=alpha),
        grid=(s // tm,),
        in_specs=[rspec(d), rspec(NSA_Q_W), rspec(NSA_Q_W), rspec(NSA_Q_W), rspec(LANES), cspec(eg)]
                 + [rspec(DIL_OUT_W)] * 6 + [rspec(SGU_WIDTH), rspec(N_BRANCH * d)]
                 + [cspec(wa), cspec(wb), cspec(wc), cspec(wo), pl.BlockSpec((1, d), const), pl.BlockSpec((1, d), const)],
        out_specs=rspec(d),
        out_shape=jax.ShapeDtypeStruct((s, d), F32),
        compiler_params=_cparams(("parallel",), vmem),
        name="mixer_merge_deepnorm",
    )(x, o_c, o_s, o_w, g_a, eg, *dil_o, *dil_lse, sgu, g_m, wa, wb, wc, wo, g.reshape(1, d), b.reshape(1, d))


def _rope_tables(s):
    inv = 1.0 / (ROPE_THETA ** (jnp.arange(0, HEAD_DIM, 2, dtype=F32) / HEAD_DIM))
    ang = jnp.arange(s, dtype=F32)[:, None] * inv[None, :]
    cos, sin = jnp.cos(ang), jnp.sin(ang)
    reps = LANES // HEAD_DIM
    return jnp.tile(jnp.concatenate([cos, cos], axis=1), (1, reps)), jnp.tile(jnp.concatenate([-sin, sin], axis=1), (1, reps))


def kernel(x, ln_g, ln_b, ffn1_gate, ffn1_up, ffn1_down, ffn2_gate, ffn2_up, ffn2_down, w_in, phi_k_pos, phi_k_w1, phi_k_w2, phi_v_pos, phi_v_w1, phi_v_w2, sgu_ln_g, sgu_ln_b, sgu_w, sgu_b, w_branch_a, w_branch_b, w_branch_c, w_out):
    bsz, s, d = x.shape
    depth = ln_g.shape[0]
    alpha = float((2 * depth) ** 0.25)
    q_scale = float(HEAD_DIM ** -0.5)
    tabs = _rope_tables(s)
    sizes = [NSA_Q_W] + [NSA_KV_W] * 6 + [N_BRANCH * NSA_HEADS, DIL_W, DIL_W, DIL_W, 2 * SGU_WIDTH, N_BRANCH * d]
    offs = np.concatenate([[0], np.cumsum(sizes)]).tolist()
    outs = []
    for bi in range(bsz):
        h = x[bi]
        for l in range(depth):
            (w_qa, w_kc, w_vc, w_ks, w_vs, w_kw, w_vw, w_ga, w_qb, w_kb, w_vb, w_uv, w_gm) = [
                w_in[l][:, offs[i]:offs[i + 1]] for i in range(len(sizes))]
            w_ga = jnp.pad(w_ga, ((0, 0), (0, LANES - w_ga.shape[1])))
            groups = dict(qb=w_qb, kb=w_kb, vb=w_vb, k_rot=jnp.concatenate([w_ks, w_kw], axis=1),
                          kv_c=jnp.concatenate([w_kc, w_vc], axis=1), v_plain=jnp.concatenate([w_vs, w_vw], axis=1),
                          uv=w_uv, gm=w_gm, qa=w_qa, ga=w_ga)
            wp = jnp.concatenate(list(groups.values()), axis=1).astype(BF16)
            ends = np.cumsum([g.shape[1] for g in groups.values()]).tolist()
            cols = {name: (e - g.shape[1], e) for (name, g), e in zip(groups.items(), ends)}

            h, hb = _ffn(h, ffn1_gate[l].astype(BF16), ffn1_up[l].astype(BF16), ffn1_down[l].astype(BF16),
                         ln_g[l, 0], ln_b[l, 0], alpha)

            q_rot, q_u = _mm(hb, wp, cols["qa"], BF16, rope_tabs=tabs, scale=q_scale * LOG2E, plain_scale=q_scale)
            k_rot = _mm(hb, wp, cols["k_rot"], BF16, rope_tabs=tabs)
            qb = _mm(hb, wp, cols["qb"], F32, rope_tabs=tabs, scale=q_scale)
            kb = _mm(hb, wp, cols["kb"], F32, rope_tabs=tabs)
            vb = _mm(hb, wp, cols["vb"], F32)
            kv_c = _mm(hb, wp, cols["kv_c"], F32)
            v_plain = _mm(hb, wp, cols["v_plain"], BF16)
            uv = _mm(hb, wp, cols["uv"], F32)
            g_m = _mm(hb, wp, cols["gm"], F32)
            g_a = _mm(hb, wp, cols["ga"], F32)

            kc = _compress(kv_c, 0, phi_k_pos[l], phi_k_w1[l], phi_k_w2[l])
            vc = _compress(kv_c, 1, phi_v_pos[l], phi_v_w1[l], phi_v_w2[l])
            o_c, nsel = _cmp_select(q_u, kc, vc)
            o_s = _sel_attn(q_rot, k_rot[:, :NSA_KV_W], v_plain[:, :NSA_KV_W], nsel)
            o_w = _win_attn(q_rot, k_rot, v_plain[:, NSA_KV_W:], 1)
            dil = [_dil_attn(qb, kb, vb, gi, win, dl) for gi, (win, dl) in enumerate(DIL_PAIRS)]
            sg = _sgu(uv, sgu_ln_g[l], sgu_ln_b[l], sgu_w[l], sgu_b[l])
            h = _mix(h, o_c, o_s, o_w, g_a, [o for o, _ in dil], [e for _, e in dil], sg, g_m,
                     w_branch_a[l].astype(BF16), w_branch_b[l].astype(BF16), w_branch_c[l].astype(BF16),
                     w_out[l].astype(BF16), ln_g[l, 1], ln_b[l, 1], alpha)

            h, _ = _ffn(h, ffn2_gate[l].astype(BF16), ffn2_up[l].astype(BF16), ffn2_down[l].astype(BF16),
                        ln_g[l, 2], ln_b[l, 2], alpha)
        outs.append(h)
    return jnp.stack(outs, axis=0)
```

```python
import functools

import numpy as np
import jax
import jax.numpy as jnp
from jax import lax
from jax.experimental import pallas as pl
from jax.experimental.pallas import tpu as pltpu

F32 = jnp.float32
BF16 = jnp.bfloat16

HEAD_DIM = 64
ROPE_THETA = 10000.0
LN_EPS = 1e-5
Q_BLOCK = 128
NSA_HEADS = 8
NSA_GROUPS = 2
NSA_HPG = NSA_HEADS // NSA_GROUPS
NSA_Q_W = NSA_HEADS * HEAD_DIM
NSA_KV_W = NSA_GROUPS * HEAD_DIM
CMP_LEN = 32
CMP_STRIDE = 16
SLC_BLOCK = 64
SLC_RATIO = SLC_BLOCK // CMP_STRIDE
SLC_TOPK = 16
MM_ROWS = 2048
MM_SUB_ROWS = 512
FFN_ROWS = 1024
CMP_Q_BLOCK = 256
DIL_STEP_TOKENS = 2048
SEL_TILE = 1024
NSA_WINDOW = 512
FORCE_SCORE = 1e6
DIL_PAIRS = ((128, 1), (512, 4), (2048, 16))
DIL_HPG = 4
DIL_HEADS = DIL_HPG * 3
DIL_W = DIL_HEADS * HEAD_DIM
DIL_OUT_W = DIL_HPG * HEAD_DIM
SGU_CHUNK = 128
SGU_GROUPS = 4
SGU_GROUP_CH = 128
SGU_WIDTH = SGU_GROUPS * SGU_GROUP_CH
N_BRANCH = 3

LANES = 128
V7X_VMEM_BYTES = 64 * 1024 * 1024
VMEM_CAP = 56 * 1024 * 1024

M_INIT = -1e30
MASKED = -(2.0 ** 101)
LOG2E = 1.4426950408889634


def _cparams(sem, vmem_bytes):
    return pltpu.CompilerParams(dimension_semantics=sem,
                                vmem_limit_bytes=int(min(max(vmem_bytes, 16 * 1024 * 1024), VMEM_CAP)))


def _nbytes(shape, dtype):
    return int(np.prod(shape)) * jnp.dtype(dtype).itemsize


def _layer_norm(y, g, b):
    mu = jnp.mean(y, axis=-1, keepdims=True)
    d = y - mu
    var = jnp.mean(d * d, axis=-1, keepdims=True)
    return d * lax.rsqrt(var + LN_EPS) * g + b


def _lane_iota(shape):
    return lax.broadcasted_iota(jnp.int32, shape, len(shape) - 1)


def _row_iota(shape):
    return lax.broadcasted_iota(jnp.int32, shape, 0)


def _ffn_kernel(x_ref, wg_ref, wu_ref, wd_ref, g_ref, b_ref, o_ref, ob_ref, *, alpha, fc):
    n_f = wg_ref.shape[1]
    for r0 in range(0, x_ref.shape[0], MM_SUB_ROWS):
        rs = slice(r0, r0 + MM_SUB_ROWS)
        x = x_ref[rs, :]
        xb = x.astype(BF16)
        acc = jnp.zeros(x.shape, F32)
        for c in range(n_f // fc):
            sl = slice(c * fc, (c + 1) * fc)
            gate = jnp.dot(xb, wg_ref[:, sl], preferred_element_type=F32)
            up = jnp.dot(xb, wu_ref[:, sl], preferred_element_type=F32)
            h = (gate * jax.nn.sigmoid(gate)) * up
            acc = acc + jnp.dot(h.astype(BF16), wd_ref[sl, :], preferred_element_type=F32)
        out = _layer_norm(alpha * x + 0.5 * acc, g_ref[...], b_ref[...])
        o_ref[rs, :] = out
        ob_ref[rs, :] = out.astype(BF16)


def _ffn(x, wg, wu, wd, g, b, alpha):
    s, d = x.shape
    n_f = wg.shape[1]
    tm = min(FFN_ROWS, s)
    fc = 256
    assert s % tm == 0 and tm % MM_SUB_ROWS == 0 and n_f % fc == 0
    const = lambda i: (0, 0)
    row = lambda i: (i, 0)
    once = pl.Buffered(1)
    vmem = (2 * 2 * _nbytes((tm, d), F32) + 2 * _nbytes((tm, d), BF16)
            + 3 * _nbytes((d, n_f), BF16) + 8 * _nbytes((MM_SUB_ROWS, d), F32))
    return pl.pallas_call(
        functools.partial(_ffn_kernel, alpha=alpha, fc=fc),
        grid=(s // tm,),
        in_specs=[pl.BlockSpec((tm, d), row), pl.BlockSpec((d, n_f), const, pipeline_mode=once),
                  pl.BlockSpec((d, n_f), const, pipeline_mode=once),
                  pl.BlockSpec((n_f, d), const, pipeline_mode=once),
                  pl.BlockSpec((1, d), const), pl.BlockSpec((1, d), const)],
        out_specs=[pl.BlockSpec((tm, d), row), pl.BlockSpec((tm, d), row)],
        out_shape=[jax.ShapeDtypeStruct((s, d), F32), jax.ShapeDtypeStruct((s, d), BF16)],
        compiler_params=_cparams(("parallel",), vmem),
        name="ffn_deepnorm",
    )(x, wg, wu, wd, g.reshape(1, d), b.reshape(1, d))


def _swap_halves_64(a):
    half = HEAD_DIM // 2
    first = (_lane_iota(a.shape) % HEAD_DIM) < half
    return jnp.where(first, pltpu.roll(a, LANES - half, 1), pltpu.roll(a, half, 1))


def _mm_kernel(*refs, rope, scale, plain_scale):
    if rope:
        x_ref, w_ref, c_ref, s_ref, o_ref = refs[:5]
    else:
        x_ref, w_ref, o_ref = refs
    for r0 in range(0, x_ref.shape[0], MM_SUB_ROWS):
        rs = slice(r0, r0 + MM_SUB_ROWS)
        acc = jnp.dot(x_ref[rs, :], w_ref[...], preferred_element_type=F32)
        if plain_scale is not None:
            refs[5][rs, :] = (acc * plain_scale).astype(refs[5].dtype)
        if rope:
            cos = c_ref[rs, :]
            sin = s_ref[rs, :]
            for c in range(acc.shape[1] // LANES):
                a = acc[:, c * LANES:(c + 1) * LANES]
                r = a * cos + _swap_halves_64(a) * sin
                if scale != 1.0:
                    r = r * scale
                o_ref[rs, c * LANES:(c + 1) * LANES] = r.astype(o_ref.dtype)
        else:
            if scale != 1.0:
                acc = acc * scale
            o_ref[rs, :] = acc.astype(o_ref.dtype)


def _mm(xb, w, cols, out_dtype, *, rope_tabs=None, scale=1.0, plain_scale=None):
    s, k = xb.shape
    col0, n = cols[0], cols[1] - cols[0]
    tm = min(MM_ROWS, s)
    tn = next(c for c in (1024, 768, 512, 256, LANES) if n % c == 0 and col0 % c == 0)
    assert s % tm == 0 and tm % MM_SUB_ROWS == 0 and (plain_scale is None or rope_tabs is not None)
    jb = col0 // tn
    n_out = 1 if plain_scale is None else 2
    in_specs = [pl.BlockSpec((tm, k), lambda i, j: (i, 0)), pl.BlockSpec((k, tn), lambda i, j: (0, jb + j))]
    args = [xb, w]
    if rope_tabs is not None:
        in_specs += [pl.BlockSpec((tm, LANES), lambda i, j: (i, 0))] * 2
        args += list(rope_tabs)
    vmem = 2 * (_nbytes((tm, k), BF16) + _nbytes((k, tn), BF16) + n_out * _nbytes((tm, tn), out_dtype)
                + 2 * _nbytes((tm, LANES), F32)) + 6 * _nbytes((MM_SUB_ROWS, tn), F32)
    out = pl.pallas_call(
        functools.partial(_mm_kernel, rope=rope_tabs is not None, scale=scale, plain_scale=plain_scale),
        grid=(s // tm, n // tn),
        in_specs=in_specs,
        out_specs=[pl.BlockSpec((tm, tn), lambda i, j: (i, j))] * n_out,
        out_shape=[jax.ShapeDtypeStruct((s, n), out_dtype)] * n_out,
        compiler_params=_cparams(("parallel", "parallel"), vmem),
        name="proj_rope" if rope_tabs is not None else "proj",
    )(*args)
    return out[0] if n_out == 1 else out


def _compress_kernel(x_ref, pos_ref, w1_ref, w2_ref, o_ref):
    n_cmp = o_ref.shape[0]
    hidden = w1_ref.shape[3]
    ha = [jnp.zeros((n_cmp, hidden), F32) for _ in range(NSA_GROUPS)]
    hb = [jnp.zeros((n_cmp, hidden), F32) for _ in range(NSA_GROUPS)]
    tail = [jnp.zeros((8, hidden), F32) for _ in range(NSA_GROUPS)]
    for j in range(CMP_STRIDE):
        xj = x_ref[pl.ds(j, n_cmp, stride=CMP_STRIDE), :]
        pa = pos_ref[j:j + 1, :]
        pb = pos_ref[CMP_STRIDE + j:CMP_STRIDE + j + 1, :]
        xa = (xj + pa).astype(BF16)
        xb = (xj + pb).astype(BF16)
        pb8 = jnp.broadcast_to(pb, (8, pb.shape[1])).astype(BF16)
        for g in range(NSA_GROUPS):
            ha[g] = ha[g] + jnp.dot(xa, w1_ref[g, j], preferred_element_type=F32)
            hb[g] = hb[g] + jnp.dot(xb, w1_ref[g, CMP_STRIDE + j], preferred_element_type=F32)
            tail[g] = tail[g] + jnp.dot(pb8, w1_ref[g, CMP_STRIDE + j], preferred_element_type=F32)
    last = _row_iota((n_cmp, 1)) == n_cmp - 1
    outs = []
    for g in range(NSA_GROUPS):
        hb_next = jnp.where(last, tail[g][0:1, :], pltpu.roll(hb[g], n_cmp - 1, 0))
        h = jax.nn.gelu(ha[g] + hb_next)
        outs.append(jnp.dot(h.astype(BF16), w2_ref[...], preferred_element_type=F32))
    lo = _lane_iota(outs[0].shape) < HEAD_DIM
    o_ref[...] = jnp.where(lo, outs[0], outs[1]).astype(o_ref.dtype)


def _compress(x, col, pos, w1, w2):
    s = x.shape[0]
    n_cmp = s // CMP_STRIDE
    hidden = w1.shape[1]
    pos2 = jnp.concatenate([pos] * NSA_GROUPS, axis=1)
    w1r = w1.reshape(CMP_LEN, HEAD_DIM, hidden).astype(BF16)
    zero = jnp.zeros_like(w1r)
    w1g = jnp.stack([jnp.concatenate([w1r if gg == g else zero for gg in range(NSA_GROUPS)], axis=1)
                     for g in range(NSA_GROUPS)])
    w2d = jnp.concatenate([w2, w2], axis=1).astype(BF16)
    vmem = 2 * (_nbytes((s, NSA_KV_W), F32) + _nbytes(w1g.shape, BF16)) + 16 * _nbytes((n_cmp, hidden), F32)
    return pl.pallas_call(
        _compress_kernel,
        grid=(1,),
        in_specs=[pl.BlockSpec((s, NSA_KV_W), lambda i: (0, col)), pl.BlockSpec(pos2.shape, lambda i: (0, 0)),
                  pl.BlockSpec(w1g.shape, lambda i: (0, 0, 0, 0)), pl.BlockSpec(w2d.shape, lambda i: (0, 0))],
        out_specs=pl.BlockSpec((n_cmp, NSA_KV_W), lambda i: (0, 0)),
        out_shape=jax.ShapeDtypeStruct((n_cmp, NSA_KV_W), BF16),
        compiler_params=_cparams(("arbitrary",), vmem),
        name="nsa_compress",
    )(x, pos2, w1g, w2d)


def _stack_group_queries(q_ref, col0, kv_lane_group):
    parts = []
    for h in range(NSA_HPG):
        c = col0 + (h // 2) * LANES
        qc = q_ref[:, c:c + LANES].astype(F32)
        e = h % 2
        lane = _lane_iota(qc.shape)
        mine = (lane >= e * HEAD_DIM) & (lane < (e + 1) * HEAD_DIM)
        qh = jnp.where(mine, qc, 0.0)
        if e != kv_lane_group:
            qh = pltpu.roll(qh, HEAD_DIM, 1)
        parts.append(qh.astype(BF16))
    return jnp.concatenate(parts, axis=0)


def _unstack_group_outputs(o, kv_lane_group):
    chunks = []
    tq = o.shape[0] // NSA_HPG
    for c in range(NSA_HPG // 2):
        halves = []
        for e in range(2):
            h = 2 * c + e
            oh = o[h * tq:(h + 1) * tq, :]
            if e != kv_lane_group:
                oh = pltpu.roll(oh, HEAD_DIM, 1)
            halves.append(oh)
        lo = _lane_iota(halves[0].shape) < HEAD_DIM
        chunks.append(jnp.where(lo, halves[0], halves[1]))
    return jnp.concatenate(chunks, axis=1)


def _qk(q, k):
    return lax.dot_general(q, k, (((1,), (1,)), ((), ())), preferred_element_type=F32)


def _softmax_rows(s, mask, base2=False):
    sm = jnp.where(mask, s, MASKED)
    m = jnp.maximum(jnp.max(sm, axis=-1, keepdims=True), M_INIT)
    e = jnp.exp2(sm - m) if base2 else jnp.exp(sm - m)
    return e, jnp.sum(e, axis=-1, keepdims=True), m


def _cmp_select_kernel(q_ref, kc_ref, vc_ref, mmap_ref, oc_ref, nsel_ref, *, top_k, n_cls):
    b = pl.program_id(0)
    n_cmp = kc_ref.shape[0]
    n_slc = mmap_ref.shape[0]
    tq = q_ref.shape[0]
    rows = NSA_HPG * tq
    gw = NSA_HPG * HEAD_DIM
    t4 = b * tq + (_row_iota((rows, 1)) % tq)
    t = b * tq + _lane_iota((1, tq))

    def variant(nc, nb):
        valid = _lane_iota((1, nc)) * CMP_STRIDE + (CMP_LEN - 1) <= t4
        blk = _row_iota((nb, 1))
        cur = t // SLC_BLOCK
        forced = (blk == 0) | (blk == cur) | (blk == cur - 1)
        future = blk * SLC_BLOCK > t
        blk_f = blk.astype(F32)
        kc = kc_ref[0:nc, :]
        vc = vc_ref[0:nc, :]
        mmap_t = mmap_ref[0:nb, 0:nc]
        for g in range(NSA_GROUPS):
            qst = _stack_group_queries(q_ref, g * gw, g)
            e, den, _ = _softmax_rows(_qk(qst, kc), valid)
            p = e / jnp.maximum(den, 1e-30)
            o = jnp.dot(p.astype(BF16), vc, preferred_element_type=F32)
            oc_ref[:, g * gw:(g + 1) * gw] = _unstack_group_outputs(o, g).astype(oc_ref.dtype)
            psum = p[0:tq]
            for h in range(1, NSA_HPG):
                psum = psum + p[h * tq:(h + 1) * tq]
            hi = psum.astype(BF16)
            r1 = psum - hi.astype(F32)
            mid = r1.astype(BF16)
            low = (r1 - mid.astype(F32)).astype(BF16)
            p_slc = _qk(mmap_t, hi) + _qk(mmap_t, mid) + _qk(mmap_t, low)
            score = jnp.where(forced, FORCE_SCORE, jnp.where(future, -1.0, p_slc))
            for _ in range(top_k):
                mx = jnp.max(score, axis=0, keepdims=True)
                first = jnp.min(jnp.where(score == mx, blk_f, float(nb)), axis=0, keepdims=True)
                score = jnp.where(blk_f == first, -2.0, score)
            not_picked = jnp.where(score == -2.0, 0.0, 1.0)
            nsel_ref[:, g * n_slc:g * n_slc + nb] = not_picked.T.astype(nsel_ref.dtype)
            if nb < n_slc:
                nsel_ref[:, g * n_slc + nb:(g + 1) * n_slc] = jnp.ones((tq, n_slc - nb), nsel_ref.dtype)

    for c in range(n_cls):
        nc = (c + 1) * (n_cmp // n_cls)
        nb = min(n_slc, -(-(nc // SLC_RATIO) // LANES) * LANES)
        lo_b = c * (nc // (c + 1)) // (tq // CMP_STRIDE)
        hi_b = nc // (tq // CMP_STRIDE)
        pl.when((b >= lo_b) & (b < hi_b))(functools.partial(variant, nc, nb))


def _cmp_select(q_u, kc, vc):
    s = q_u.shape[0]
    n_cmp, n_slc = s // CMP_STRIDE, s // SLC_BLOCK
    top_k = min(SLC_TOPK, n_slc)
    n_cls = max(n_cmp // (2 * LANES), 1)
    tq = min(CMP_Q_BLOCK, s)
    assert s % tq == 0 and n_cmp % n_cls == 0 and (n_cmp // n_cls) % (tq // CMP_STRIDE) == 0
    m = np.arange(n_cmp)[:, None]
    j = np.arange(n_slc)[None, :]
    mmap = jnp.asarray(((m >= SLC_RATIO * j - 1) & (m <= SLC_RATIO * j + SLC_RATIO - 1)).astype(np.float32).T, BF16)
    const = lambda b: (0, 0)
    rows = NSA_HPG * tq
    vmem = 2 * (2 * _nbytes((n_cmp, NSA_KV_W), BF16) + _nbytes((n_cmp, n_slc), BF16)
                + _nbytes((tq, NSA_Q_W), F32) * 2) + 6 * _nbytes((rows, n_cmp), F32)
    return pl.pallas_call(
        functools.partial(_cmp_select_kernel, top_k=top_k, n_cls=n_cls),
        grid=(s // tq,),
        in_specs=[pl.BlockSpec((tq, NSA_Q_W), lambda b: (b, 0)), pl.BlockSpec((n_cmp, NSA_KV_W), const),
                  pl.BlockSpec((n_cmp, NSA_KV_W), const), pl.BlockSpec((n_slc, n_cmp), const)],
        out_specs=[pl.BlockSpec((tq, NSA_Q_W), lambda b: (b, 0)),
                   pl.BlockSpec((tq, NSA_GROUPS * n_slc), lambda b: (b, 0))],
        out_shape=[jax.ShapeDtypeStruct((s, NSA_Q_W), BF16), jax.ShapeDtypeStruct((s, NSA_GROUPS * n_slc), BF16)],
        compiler_params=_cparams(("parallel",), vmem),
        name="nsa_compressed_select",
    )(q_u, kc, vc, mmap)


def _sel_attn_kernel(q_ref, k_ref, v_ref, nsel_ref, o_ref, qa_ref, m_ref, acc_ref, s_ref, *, win):
    b = pl.program_id(0)
    n_slc = nsel_ref.shape[1] // NSA_GROUPS
    n_win = n_slc // win
    tiles_per_win = win * SLC_BLOCK // SEL_TILE
    gw = NSA_HPG * HEAD_DIM
    n_chunks = SEL_TILE // LANES
    rows = NSA_HPG * Q_BLOCK
    last = (b * Q_BLOCK) // SEL_TILE
    t = b * Q_BLOCK + _row_iota((Q_BLOCK, 1))
    lane = _lane_iota((Q_BLOCK, LANES))
    head_lanes = lane < HEAD_DIM
    flag_lanes = (lane >= HEAD_DIM) & (lane < HEAD_DIM + win)

    nsel_all = nsel_ref[...].astype(F32)
    if nsel_all.shape[1] < LANES:
        nsel_all = jnp.concatenate([nsel_all, jnp.zeros((Q_BLOCK, LANES - nsel_all.shape[1]), F32)], axis=1)
    for g in range(NSA_GROUPS):
        heads = []
        for h in range(NSA_HPG):
            c = g * gw + (h // 2) * LANES
            qc = q_ref[:, c:c + LANES].astype(F32)
            heads.append(qc if h % 2 == 0 else pltpu.roll(qc, HEAD_DIM, 1))
        for w in range(n_win):
            a = g * n_slc + w * win
            fl = nsel_all[:, (a // LANES) * LANES:(a // LANES + 1) * LANES]
            shift = (HEAD_DIM - a % LANES) % LANES
            if shift:
                fl = pltpu.roll(fl, shift, 1)
            fl = jnp.where(flag_lanes, fl, 0.0)
            for h in range(NSA_HPG):
                qa_ref[g, w, h * Q_BLOCK:(h + 1) * Q_BLOCK, :] = jnp.where(head_lanes, heads[h], fl).astype(BF16)
    m_ref[...] = jnp.full(m_ref.shape, M_INIT, F32)
    acc_ref[...] = jnp.zeros(acc_ref.shape, F32)

    def scores(kt, slot):
        off = pl.multiple_of(kt * SEL_TILE, SEL_TILE)
        for g in range(NSA_GROUPS):
            s_ref[g, slot] = _qk(qa_ref[g, kt // tiles_per_win], k_ref[g, pl.ds(off, SEL_TILE), :])

    def update(kt, slot, causal):
        off = pl.multiple_of(kt * SEL_TILE, SEL_TILE)
        for g in range(NSA_GROUPS):
            s = s_ref[g, slot]
            if causal:
                keep = off + _lane_iota((1, SEL_TILE)) <= t
                s = jnp.where(jnp.concatenate([keep] * NSA_HPG, axis=0), s, MASKED)
            m_old = m_ref[g]
            m_new = jnp.maximum(m_old, jnp.max(s, axis=-1, keepdims=True))
            alpha = jnp.exp2(m_old - m_new)
            p = jnp.concatenate([jnp.exp2(s[:, c * LANES:(c + 1) * LANES] - m_new).astype(BF16)
                                 for c in range(n_chunks)], axis=1)
            v = v_ref[g, pl.ds(off, SEL_TILE), :]
            acc_ref[g] = alpha * acc_ref[g] + jnp.dot(p, v, preferred_element_type=F32)
            m_ref[g] = m_new

    def finish():
        for g in range(NSA_GROUPS):
            acc = acc_ref[g]
            o = acc / jnp.maximum(acc[:, HEAD_DIM:HEAD_DIM + 1], 1e-30)
            for c in range(NSA_HPG // 2):
                even = o[(2 * c) * Q_BLOCK:(2 * c + 1) * Q_BLOCK, :]
                odd = pltpu.roll(o[(2 * c + 1) * Q_BLOCK:(2 * c + 2) * Q_BLOCK, :], HEAD_DIM, 1)
                o_ref[:, g * gw + c * LANES:g * gw + (c + 1) * LANES] = jnp.where(head_lanes, even, odd).astype(o_ref.dtype)

    scores(0, 0)

    def pair(j, carry):
        scores(2 * j + 1, 1)
        update(2 * j, 0, False)
        scores(2 * j + 2, 0)
        update(2 * j + 1, 1, False)
        return carry

    lax.fori_loop(0, last // 2, pair, 0)

    @pl.when(last % 2 == 1)
    def _():
        scores(last, 1)
        update(last - 1, 0, False)
        update(last, 1, True)
        finish()

    @pl.when(last % 2 == 0)
    def _():
        update(last, 0, True)
        finish()


def _sel_attn(rot_a, k_s, v_s, nsel):
    s = rot_a.shape[0]
    n_slc = s // SLC_BLOCK
    win = min(n_slc, LANES - HEAD_DIM)
    assert s % SEL_TILE == 0 and n_slc % win == 0 and (win * SLC_BLOCK) % SEL_TILE == 0
    n_win = n_slc // win
    spare = LANES - HEAD_DIM
    blk = (np.arange(s) // SLC_BLOCK) % win
    marks = jnp.asarray(np.where(np.arange(spare)[None, :] == blk[:, None], MASKED, 0.0).astype(np.float32), BF16)
    k_aug = jnp.stack([jnp.concatenate([k_s[:, g * HEAD_DIM:(g + 1) * HEAD_DIM], marks], axis=1)
                       for g in range(NSA_GROUPS)])
    v_aug = _ones_augmented_values(v_s)
    rows = NSA_HPG * Q_BLOCK
    vmem = (2 * (2 * _nbytes(k_aug.shape, BF16) + _nbytes((Q_BLOCK, NSA_Q_W), BF16)
                 + _nbytes((Q_BLOCK, NSA_GROUPS * n_slc), BF16) + _nbytes((Q_BLOCK, NSA_Q_W), F32))
            + NSA_GROUPS * (n_win * _nbytes((rows, LANES), BF16) + 2 * _nbytes((rows, LANES), F32)
                            + 2 * _nbytes((rows, SEL_TILE), F32))
            + 4 * _nbytes((rows, SEL_TILE), F32))
    return pl.pallas_call(
        functools.partial(_sel_attn_kernel, win=win),
        grid=(s // Q_BLOCK,),
        in_specs=[pl.BlockSpec((Q_BLOCK, NSA_Q_W), lambda b: (b, 0)),
                  pl.BlockSpec(k_aug.shape, lambda b: (0, 0, 0), pipeline_mode=pl.Buffered(1)),
                  pl.BlockSpec(v_aug.shape, lambda b: (0, 0, 0), pipeline_mode=pl.Buffered(1)),
                  pl.BlockSpec((Q_BLOCK, NSA_GROUPS * n_slc), lambda b: (b, 0))],
        out_specs=pl.BlockSpec((Q_BLOCK, NSA_Q_W), lambda b: (b, 0)),
        out_shape=jax.ShapeDtypeStruct((s, NSA_Q_W), BF16),
        scratch_shapes=[pltpu.VMEM((NSA_GROUPS, n_win, rows, LANES), BF16),
                        pltpu.VMEM((NSA_GROUPS, rows, LANES), F32), pltpu.VMEM((NSA_GROUPS, rows, LANES), F32),
                        pltpu.VMEM((NSA_GROUPS, 2, rows, SEL_TILE), F32)],
        compiler_params=_cparams(("parallel",), vmem),
        name="nsa_selected",
    )(rot_a, k_aug, v_aug, nsel)


def _win_attn_kernel(q_ref, k_ref, v_ref, o_ref, *, span):
    b = pl.program_id(0)
    rows = NSA_HPG * Q_BLOCK
    n_back = NSA_WINDOW // Q_BLOCK
    start = pl.multiple_of(jnp.maximum(b - n_back, 0) * Q_BLOCK, Q_BLOCK)
    k = k_ref[pl.ds(start, span), :]
    t = b * Q_BLOCK + _row_iota((Q_BLOCK, 1))
    diff = t - (start + _lane_iota((1, span)))
    bias = jnp.where((diff >= 0) & (diff < NSA_WINDOW), 0.0, MASKED)
    bias = jnp.concatenate([bias] * NSA_HPG, axis=0)
    head_lanes = _lane_iota((Q_BLOCK, LANES)) < HEAD_DIM
    gw = NSA_HPG * HEAD_DIM
    for g in range(NSA_GROUPS):
        qst = _stack_group_queries(q_ref, g * gw, g)
        s = _qk(qst, k) + bias
        m = jnp.maximum(jnp.max(s, axis=-1, keepdims=True), M_INIT)
        e = jnp.exp2(s - m).astype(BF16)
        acc = jnp.dot(e, v_ref[g, pl.ds(start, span), :], preferred_element_type=F32)
        o = acc / jnp.maximum(acc[:, HEAD_DIM:HEAD_DIM + 1], 1e-30)
        for c in range(NSA_HPG // 2):
            even = o[(2 * c) * Q_BLOCK:(2 * c + 1) * Q_BLOCK, :]
            odd = pltpu.roll(o[(2 * c + 1) * Q_BLOCK:(2 * c + 2) * Q_BLOCK, :], HEAD_DIM, 1)
            o_ref[:, g * gw + c * LANES:g * gw + (c + 1) * LANES] = jnp.where(head_lanes, even, odd).astype(o_ref.dtype)


def _ones_augmented_values(v):
    s = v.shape[0]
    spare = LANES - HEAD_DIM
    ones = jnp.asarray((np.arange(spare)[None, :] == 0).astype(np.float32) * np.ones((s, 1), np.float32), BF16)
    return jnp.stack([jnp.concatenate([v[:, g * HEAD_DIM:(g + 1) * HEAD_DIM], ones], axis=1)
                      for g in range(NSA_GROUPS)])


def _win_attn(q_rot, k_rot, v_w, kw_col):
    s = q_rot.shape[0]
    span = min(NSA_WINDOW + Q_BLOCK, s)
    rows = NSA_HPG * Q_BLOCK
    v_aug = _ones_augmented_values(v_w)
    vmem = 2 * (_nbytes((s, NSA_KV_W), BF16) + _nbytes(v_aug.shape, BF16) + _nbytes((Q_BLOCK, NSA_Q_W), BF16)
                + _nbytes((Q_BLOCK, NSA_Q_W), F32)) + 8 * _nbytes((rows, span), F32)
    return pl.pallas_call(
        functools.partial(_win_attn_kernel, span=span),
        grid=(s // Q_BLOCK,),
        in_specs=[pl.BlockSpec((Q_BLOCK, NSA_Q_W), lambda b: (b, 0)),
                  pl.BlockSpec((s, NSA_KV_W), lambda b: (0, kw_col)),
                  pl.BlockSpec(v_aug.shape, lambda b: (0, 0, 0))],
        out_specs=pl.BlockSpec((Q_BLOCK, NSA_Q_W), lambda b: (b, 0)),
        out_shape=jax.ShapeDtypeStruct((s, NSA_Q_W), BF16),
        compiler_params=_cparams(("parallel",), vmem),
        name="nsa_window",
    )(q_rot, k_rot, v_aug)


def _dil_attn_kernel(q_ref, kp_ref, kc_ref, vp_ref, vc_ref, o_ref, lse_ref, kbuf, vbuf, *, span, dil, nb):
    n = pl.program_id(0)
    unit = dil * Q_BLOCK
    kbuf[0:unit, :] = kp_ref[...]
    kbuf[unit:, :] = kc_ref[...]
    vbuf[0:unit, :] = vp_ref[...]
    vbuf[unit:, :] = vc_ref[...]
    qi = _row_iota((Q_BLOCK, 1)) + Q_BLOCK
    ki = _lane_iota((1, 2 * Q_BLOCK))
    delta = qi - ki
    band_bias = jnp.where((delta >= 0) & (delta <= span), 0.0, MASKED)
    lo = _lane_iota((Q_BLOCK, LANES)) < HEAD_DIM

    def item(idx, carry):
        j = idx // dil
        base = j * unit + idx % dil
        if dil > 1:
            q_rows, kv_rows = pl.ds(base, Q_BLOCK, stride=dil), pl.ds(base, 2 * Q_BLOCK, stride=dil)
        else:
            base = pl.multiple_of(base, Q_BLOCK)
            q_rows, kv_rows = pl.ds(base, Q_BLOCK), pl.ds(base, 2 * Q_BLOCK)
        bias = jnp.where((n * nb + j == 0) & (ki < Q_BLOCK), MASKED, band_bias)
        q = q_ref[q_rows, :]
        k = kbuf[kv_rows, :].astype(BF16)
        v = vbuf[kv_rows, :].astype(BF16)
        outs, lses = [], []
        for e in range(2):
            qh = jnp.where(lo if e == 0 else ~lo, q, 0.0).astype(BF16)
            s = _qk(qh, k) + bias
            m = jnp.maximum(jnp.max(s, axis=-1, keepdims=True), M_INIT)
            ex = jnp.exp(s - m)
            den = jnp.maximum(jnp.sum(ex, axis=-1, keepdims=True), 1e-30)
            outs.append(jnp.dot(ex.astype(BF16), v, preferred_element_type=F32) / den)
            lses.append(jnp.broadcast_to(m + jnp.log(den), (Q_BLOCK, LANES)))
        o_ref[q_rows, :] = jnp.where(lo, outs[0], outs[1])
        lse_ref[q_rows, :] = jnp.where(lo, lses[0], lses[1])
        return carry

    lax.fori_loop(0, nb * dil, item, 0, unroll=4)


def _dil_attn(qb, kb, vb, gidx, win, dil):
    s = qb.shape[0]
    unit = dil * Q_BLOCK
    tb = min(max(DIL_STEP_TOKENS, unit), s)
    assert s % tb == 0 and tb % unit == 0
    nb = tb // unit
    span = win // dil
    n_chunks = DIL_OUT_W // LANES
    cur = lambda n, c: (n, gidx * n_chunks + c)
    prev = lambda n, c: (jnp.maximum(n * nb - 1, 0), gidx * n_chunks + c)
    blk, pblk = (tb, LANES), (unit, LANES)
    vmem = (2 * (5 * _nbytes(blk, F32) + 2 * _nbytes(pblk, F32)) + 2 * _nbytes((tb + unit, LANES), F32)
            + 32 * _nbytes((Q_BLOCK, 2 * Q_BLOCK), F32))
    return pl.pallas_call(
        functools.partial(_dil_attn_kernel, span=span, dil=dil, nb=nb),
        grid=(s // tb, n_chunks),
        in_specs=[pl.BlockSpec(blk, cur), pl.BlockSpec(pblk, prev), pl.BlockSpec(blk, cur),
                  pl.BlockSpec(pblk, prev), pl.BlockSpec(blk, cur)],
        out_specs=[pl.BlockSpec(blk, lambda n, c: (n, c)), pl.BlockSpec(blk, lambda n, c: (n, c))],
        out_shape=[jax.ShapeDtypeStruct((s, DIL_OUT_W), F32)] * 2,
        scratch_shapes=[pltpu.VMEM((tb + unit, LANES), F32)] * 2,
        compiler_params=_cparams(("parallel", "parallel"), vmem),
        name=f"dilated_w{win}_d{dil}",
    )(qb, kb, kb, vb, vb)


def _sgu_kernel(uv_ref, g_ref, b_ref, ws_ref, bs_ref, o_ref):
    uv = jax.nn.gelu(uv_ref[...])
    u = uv[:, :SGU_WIDTH]
    v = _layer_norm(uv[:, SGU_WIDTH:], g_ref[...], b_ref[...]).astype(BF16)
    causal = _lane_iota((SGU_CHUNK, SGU_CHUNK)) <= _row_iota((SGU_CHUNK, SGU_CHUNK))
    bs = bs_ref[...]
    for n in range(uv.shape[0] // SGU_CHUNK):
        rs = slice(n * SGU_CHUNK, (n + 1) * SGU_CHUNK)
        for g in range(SGU_GROUPS):
            cs = slice(g * SGU_GROUP_CH, (g + 1) * SGU_GROUP_CH)
            ws = jnp.where(causal, ws_ref[g], 0.0).astype(BF16)
            sv = jnp.dot(ws, v[rs, cs], preferred_element_type=F32) + bs[:, cs]
            o_ref[rs, cs] = (u[rs, cs] * sv).astype(o_ref.dtype)


def _sgu(uv, ln_g, ln_b, w_s, b_s):
    s = uv.shape[0]
    tm = min(4 * SGU_CHUNK, s)
    assert s % tm == 0
    bs = jnp.repeat(b_s.T, SGU_GROUP_CH, axis=1)
    c2 = lambda i: (0, 0)
    vmem = 2 * (_nbytes((tm, 2 * SGU_WIDTH), F32) + _nbytes((tm, SGU_WIDTH), F32)) + 6 * _nbytes((tm, 2 * SGU_WIDTH), F32)
    return pl.pallas_call(
        _sgu_kernel,
        grid=(s // tm,),
        in_specs=[pl.BlockSpec((tm, 2 * SGU_WIDTH), lambda i: (i, 0)), pl.BlockSpec((1, SGU_WIDTH), c2),
                  pl.BlockSpec((1, SGU_WIDTH), c2), pl.BlockSpec(w_s.shape, lambda i: (0, 0, 0)),
                  pl.BlockSpec((SGU_CHUNK, SGU_WIDTH), c2)],
        out_specs=pl.BlockSpec((tm, SGU_WIDTH), lambda i: (i, 0)),
        out_shape=jax.ShapeDtypeStruct((s, SGU_WIDTH), BF16),
        compiler_params=_cparams(("parallel",), vmem),
        name="sgu",
    )(uv, ln_g.reshape(1, -1), ln_b.reshape(1, -1), w_s, bs)


def _split_dot(x, w2):
    hi = x.astype(BF16)
    lo = (x - hi.astype(F32)).astype(BF16)
    return jnp.dot(jnp.concatenate([hi, lo], axis=1), w2, preferred_element_type=F32)


def _mix_kernel(x_ref, oc_ref, os_ref, ow_ref, ga_ref, eg_ref, d0_ref, d1_ref, d2_ref, l0_ref, l1_ref, l2_ref,
                sg_ref, gm_ref, wa_ref, wb_ref, wc_ref, wo_ref, g_ref, b_ref, o_ref, *, alpha):
    d = x_ref.shape[1]
    gates = _split_dot(jax.nn.sigmoid(ga_ref[...]), eg_ref[...])
    o_a = (gates[:, 0:NSA_Q_W] * oc_ref[...].astype(F32) + gates[:, NSA_Q_W:2 * NSA_Q_W] * os_ref[...].astype(F32)
           + gates[:, 2 * NSA_Q_W:3 * NSA_Q_W] * ow_ref[...].astype(F32))
    y_a = jnp.dot(o_a.astype(BF16), wa_ref[...], preferred_element_type=F32)
    l0, l1, l2 = l0_ref[...], l1_ref[...], l2_ref[...]
    lm = jnp.maximum(jnp.maximum(l0, l1), l2)
    e0, e1, e2 = jnp.exp(l0 - lm), jnp.exp(l1 - lm), jnp.exp(l2 - lm)
    den = e0 + e1 + e2
    o_b = (e0 / den) * d0_ref[...] + (e1 / den) * d1_ref[...] + (e2 / den) * d2_ref[...]
    y_b = jnp.dot(o_b.astype(BF16), wb_ref[...], preferred_element_type=F32)
    y_c = jnp.dot(sg_ref[...].astype(BF16), wc_ref[...], preferred_element_type=F32)
    gm = jax.nn.sigmoid(gm_ref[...].astype(F32))
    merged = gm[:, 0:d] * y_a + gm[:, d:2 * d] * y_b + gm[:, 2 * d:3 * d] * y_c
    mix = jnp.dot(merged.astype(BF16), wo_ref[...], preferred_element_type=F32)
    o_ref[...] = _layer_norm(alpha * x_ref[...] + mix, g_ref[...], b_ref[...])


def _mix(x, o_c, o_s, o_w, g_a, dil_o, dil_lse, sgu, g_m, wa, wb, wc, wo, g, b, alpha):
    s, d = x.shape
    tm = min(256, s)
    assert s % tm == 0
    col = np.arange(N_BRANCH * NSA_Q_W)
    head, br = (col % NSA_Q_W) // HEAD_DIM, col // NSA_Q_W
    eg1 = (np.arange(LANES)[:, None] == (head * N_BRANCH + br)[None, :]).astype(np.float32)
    eg = jnp.asarray(np.concatenate([eg1, eg1], axis=0), BF16)
    row = lambda i: (i, 0)
    const = lambda i: (0, 0)
    rspec = lambda w: pl.BlockSpec((tm, w), row)
    cspec = lambda a: pl.BlockSpec(a.shape, const)
    widths = [d, NSA_Q_W, NSA_Q_W, NSA_Q_W, LANES] + [DIL_OUT_W] * 6 + [SGU_WIDTH, N_BRANCH * d]
    vmem = (2 * sum(_nbytes((tm, w), F32) for w in widths) + 2 * _nbytes((tm, d), F32)
            + 2 * sum(_nbytes(a.shape, BF16) for a in (eg, wa, wb, wc, wo)) + 10 * _nbytes((tm, N_BRANCH * d), F32))
    return pl.pallas_call(
        functools.partial(_mix_kernel, alpha=alpha),
        grid=(s // tm,),
        in_specs=[rspec(d), rspec(NSA_Q_W), rspec(NSA_Q_W), rspec(NSA_Q_W), rspec(LANES), cspec(eg)]
                 + [rspec(DIL_OUT_W)] * 6 + [rspec(SGU_WIDTH), rspec(N_BRANCH * d)]
                 + [cspec(wa), cspec(wb), cspec(wc), cspec(wo), pl.BlockSpec((1, d), const), pl.BlockSpec((1, d), const)],
        out_specs=rspec(d),
        out_shape=jax.ShapeDtypeStruct((s, d), F32),
        compiler_params=_cparams(("parallel",), vmem),
        name="mixer_merge_deepnorm",
    )(x, o_c, o_s, o_w, g_a, eg, *dil_o, *dil_lse, sgu, g_m, wa, wb, wc, wo, g.reshape(1, d), b.reshape(1, d))


def _rope_tables(s):
    inv = 1.0 / (ROPE_THETA ** (np.arange(0, HEAD_DIM, 2, dtype=np.float32) / HEAD_DIM)).astype(np.float32)
    ang = (np.arange(s, dtype=np.float32)[:, None] * inv[None, :]).astype(np.float64)
    cos, sin = jnp.asarray(np.cos(ang), F32), jnp.asarray(np.sin(ang), F32)
    reps = LANES // HEAD_DIM
    return jnp.tile(jnp.concatenate([cos, cos], axis=1), (1, reps)), jnp.tile(jnp.concatenate([-sin, sin], axis=1), (1, reps))


def kernel(x, ln_g, ln_b, ffn1_gate, ffn1_up, ffn1_down, ffn2_gate, ffn2_up, ffn2_down, w_in, phi_k_pos, phi_k_w1, phi_k_w2, phi_v_pos, phi_v_w1, phi_v_w2, sgu_ln_g, sgu_ln_b, sgu_w, sgu_b, w_branch_a, w_branch_b, w_branch_c, w_out):
    bsz, s, d = x.shape
    depth = ln_g.shape[0]
    alpha = float((2 * depth) ** 0.25)
    q_scale = float(HEAD_DIM ** -0.5)
    tabs = _rope_tables(s)
    sizes = [NSA_Q_W] + [NSA_KV_W] * 6 + [N_BRANCH * NSA_HEADS, DIL_W, DIL_W, DIL_W, 2 * SGU_WIDTH, N_BRANCH * d]
    offs = np.concatenate([[0], np.cumsum(sizes)]).tolist()
    outs = []
    for bi in range(bsz):
        h = x[bi]
        for l in range(depth):
            (w_qa, w_kc, w_vc, w_ks, w_vs, w_kw, w_vw, w_ga, w_qb, w_kb, w_vb, w_uv, w_gm) = [
                w_in[l][:, offs[i]:offs[i + 1]] for i in range(len(sizes))]
            w_ga = jnp.pad(w_ga, ((0, 0), (0, LANES - w_ga.shape[1])))
            groups = dict(qb=w_qb, kb=w_kb, vb=w_vb, k_rot=jnp.concatenate([w_ks, w_kw], axis=1),
                          kv_c=jnp.concatenate([w_kc, w_vc], axis=1), v_plain=jnp.concatenate([w_vs, w_vw], axis=1),
                          uv=w_uv, gm=w_gm, qa=w_qa, ga=w_ga)
            wp = jnp.concatenate(list(groups.values()), axis=1).astype(BF16)
            ends = np.cumsum([g.shape[1] for g in groups.values()]).tolist()
            cols = {name: (e - g.shape[1], e) for (name, g), e in zip(groups.items(), ends)}

            h, hb = _ffn(h, ffn1_gate[l].astype(BF16), ffn1_up[l].astype(BF16), ffn1_down[l].astype(BF16),
                         ln_g[l, 0], ln_b[l, 0], alpha)

            q_rot, q_u = _mm(hb, wp, cols["qa"], BF16, rope_tabs=tabs, scale=q_scale * LOG2E, plain_scale=q_scale)
            k_rot = _mm(hb, wp, cols["k_rot"], BF16, rope_tabs=tabs)
            qb = _mm(hb, wp, cols["qb"], F32, rope_tabs=tabs, scale=q_scale)
            kb = _mm(hb, wp, cols["kb"], F32, rope_tabs=tabs)
            vb = _mm(hb, wp, cols["vb"], F32)
            kv_c = _mm(hb, wp, cols["kv_c"], F32)
            v_plain = _mm(hb, wp, cols["v_plain"], BF16)
            uv = _mm(hb, wp, cols["uv"], F32)
            g_m = _mm(hb, wp, cols["gm"], BF16)
            g_a = _mm(hb, wp, cols["ga"], F32)

            kc = _compress(kv_c, 0, phi_k_pos[l], phi_k_w1[l], phi_k_w2[l])
            vc = _compress(kv_c, 1, phi_v_pos[l], phi_v_w1[l], phi_v_w2[l])
            o_c, nsel = _cmp_select(q_u, kc, vc)
            o_s = _sel_attn(q_rot, k_rot[:, :NSA_KV_W], v_plain[:, :NSA_KV_W], nsel)
            o_w = _win_attn(q_rot, k_rot, v_plain[:, NSA_KV_W:], 1)
            dil = [_dil_attn(qb, kb, vb, gi, win, dl) for gi, (win, dl) in enumerate(DIL_PAIRS)]
            sg = _sgu(uv, sgu_ln_g[l], sgu_ln_b[l], sgu_w[l], sgu_b[l])
            h = _mix(h, o_c, o_s, o_w, g_a, [o for o, _ in dil], [e for _, e in dil], sg, g_m,
                     w_branch_a[l].astype(BF16), w_branch_b[l].astype(BF16), w_branch_c[l].astype(BF16),
                     w_out[l].astype(BF16), ln_g[l, 1], ln_b[l, 1], alpha)

            h, _ = _ffn(h, ffn2_gate[l].astype(BF16), ffn2_up[l].astype(BF16), ffn2_down[l].astype(BF16),
                        ln_g[l, 2], ln_b[l, 2], alpha)
        outs.append(h)
    return jnp.stack(outs, axis=0)
```

```python
import functools

import numpy as np
import jax
import jax.numpy as jnp
from jax import lax
from jax.experimental import pallas as pl
from jax.experimental.pallas import tpu as pltpu

F32 = jnp.float32
BF16 = jnp.bfloat16

HEAD_DIM = 64
ROPE_THETA = 10000.0
LN_EPS = 1e-5
Q_BLOCK = 128
NSA_HEADS = 8
NSA_GROUPS = 2
NSA_HPG = NSA_HEADS // NSA_GROUPS
NSA_Q_W = NSA_HEADS * HEAD_DIM
NSA_KV_W = NSA_GROUPS * HEAD_DIM
CMP_LEN = 32
CMP_STRIDE = 16
SLC_BLOCK = 64
SLC_RATIO = SLC_BLOCK // CMP_STRIDE
SLC_TOPK = 16
MM_ROWS = 2048
MM_SUB_ROWS = 512
FFN_ROWS = 1024
CMP_Q_BLOCK = 256
DIL_STEP_TOKENS = 2048
SEL_TILE = 1024
SEL_VT_ROWS = 80
NSA_WINDOW = 512
FORCE_SCORE = 1e6
DIL_PAIRS = ((128, 1), (512, 4), (2048, 16))
DIL_HPG = 4
DIL_HEADS = DIL_HPG * 3
DIL_W = DIL_HEADS * HEAD_DIM
DIL_OUT_W = DIL_HPG * HEAD_DIM
SGU_CHUNK = 128
SGU_GROUPS = 4
SGU_GROUP_CH = 128
SGU_WIDTH = SGU_GROUPS * SGU_GROUP_CH
N_BRANCH = 3

LANES = 128
V7X_VMEM_BYTES = 64 * 1024 * 1024
VMEM_CAP = 56 * 1024 * 1024

M_INIT = -1e30
MASKED = -(2.0 ** 101)
LOG2E = 1.4426950408889634


def _cparams(sem, vmem_bytes):
    return pltpu.CompilerParams(dimension_semantics=sem,
                                vmem_limit_bytes=int(min(max(vmem_bytes, 16 * 1024 * 1024), VMEM_CAP)))


def _nbytes(shape, dtype):
    return int(np.prod(shape)) * jnp.dtype(dtype).itemsize


def _layer_norm(y, g, b):
    mu = jnp.mean(y, axis=-1, keepdims=True)
    d = y - mu
    var = jnp.mean(d * d, axis=-1, keepdims=True)
    return d * lax.rsqrt(var + LN_EPS) * g + b


def _lane_iota(shape):
    return lax.broadcasted_iota(jnp.int32, shape, len(shape) - 1)


def _row_iota(shape):
    return lax.broadcasted_iota(jnp.int32, shape, 0)


def _ffn_kernel(x_ref, wg_ref, wu_ref, wd_ref, g_ref, b_ref, o_ref, ob_ref, *, alpha, fc):
    n_f = wg_ref.shape[1]
    for r0 in range(0, x_ref.shape[0], MM_SUB_ROWS):
        rs = slice(r0, r0 + MM_SUB_ROWS)
        x = x_ref[rs, :]
        xb = x.astype(BF16)
        acc = jnp.zeros(x.shape, F32)
        for c in range(n_f // fc):
            sl = slice(c * fc, (c + 1) * fc)
            gate = jnp.dot(xb, wg_ref[:, sl], preferred_element_type=F32)
            up = jnp.dot(xb, wu_ref[:, sl], preferred_element_type=F32)
            h = (gate * jax.nn.sigmoid(gate)) * up
            acc = acc + jnp.dot(h.astype(BF16), wd_ref[sl, :], preferred_element_type=F32)
        out = _layer_norm(alpha * x + 0.5 * acc, g_ref[...], b_ref[...])
        o_ref[rs, :] = out
        ob_ref[rs, :] = out.astype(BF16)


def _ffn(x, wg, wu, wd, g, b, alpha):
    s, d = x.shape
    n_f = wg.shape[1]
    tm = min(FFN_ROWS, s)
    fc = 256
    assert s % tm == 0 and tm % MM_SUB_ROWS == 0 and n_f % fc == 0
    const = lambda i: (0, 0)
    row = lambda i: (i, 0)
    once = pl.Buffered(1)
    vmem = (2 * 2 * _nbytes((tm, d), F32) + 2 * _nbytes((tm, d), BF16)
            + 3 * _nbytes((d, n_f), BF16) + 8 * _nbytes((MM_SUB_ROWS, d), F32))
    return pl.pallas_call(
        functools.partial(_ffn_kernel, alpha=alpha, fc=fc),
        grid=(s // tm,),
        in_specs=[pl.BlockSpec((tm, d), row), pl.BlockSpec((d, n_f), const, pipeline_mode=once),
                  pl.BlockSpec((d, n_f), const, pipeline_mode=once),
                  pl.BlockSpec((n_f, d), const, pipeline_mode=once),
                  pl.BlockSpec((1, d), const), pl.BlockSpec((1, d), const)],
        out_specs=[pl.BlockSpec((tm, d), row), pl.BlockSpec((tm, d), row)],
        out_shape=[jax.ShapeDtypeStruct((s, d), F32), jax.ShapeDtypeStruct((s, d), BF16)],
        compiler_params=_cparams(("parallel",), vmem),
        name="ffn_deepnorm",
    )(x, wg, wu, wd, g.reshape(1, d), b.reshape(1, d))


def _swap_halves_64(a):
    half = HEAD_DIM // 2
    first = (_lane_iota(a.shape) % HEAD_DIM) < half
    return jnp.where(first, pltpu.roll(a, LANES - half, 1), pltpu.roll(a, half, 1))


def _mm_kernel(*refs, rope, scale, plain_scale):
    if rope:
        x_ref, w_ref, c_ref, s_ref, o_ref = refs[:5]
    else:
        x_ref, w_ref, o_ref = refs
    for r0 in range(0, x_ref.shape[0], MM_SUB_ROWS):
        rs = slice(r0, r0 + MM_SUB_ROWS)
        acc = jnp.dot(x_ref[rs, :], w_ref[...], preferred_element_type=F32)
        if plain_scale is not None:
            refs[5][rs, :] = (acc * plain_scale).astype(refs[5].dtype)
        if rope:
            cos = c_ref[rs, :]
            sin = s_ref[rs, :]
            for c in range(acc.shape[1] // LANES):
                a = acc[:, c * LANES:(c + 1) * LANES]
                r = a * cos + _swap_halves_64(a) * sin
                if scale != 1.0:
                    r = r * scale
                o_ref[rs, c * LANES:(c + 1) * LANES] = r.astype(o_ref.dtype)
        else:
            if scale != 1.0:
                acc = acc * scale
            o_ref[rs, :] = acc.astype(o_ref.dtype)


def _mm(xb, w, cols, out_dtype, *, rope_tabs=None, scale=1.0, plain_scale=None):
    s, k = xb.shape
    col0, n = cols[0], cols[1] - cols[0]
    tm = min(MM_ROWS, s)
    tn = next(c for c in (1024, 768, 512, 256, LANES) if n % c == 0 and col0 % c == 0)
    assert s % tm == 0 and tm % MM_SUB_ROWS == 0 and (plain_scale is None or rope_tabs is not None)
    jb = col0 // tn
    n_out = 1 if plain_scale is None else 2
    in_specs = [pl.BlockSpec((tm, k), lambda i, j: (i, 0)), pl.BlockSpec((k, tn), lambda i, j: (0, jb + j))]
    args = [xb, w]
    if rope_tabs is not None:
        in_specs += [pl.BlockSpec((tm, LANES), lambda i, j: (i, 0))] * 2
        args += list(rope_tabs)
    vmem = 2 * (_nbytes((tm, k), BF16) + _nbytes((k, tn), BF16) + n_out * _nbytes((tm, tn), out_dtype)
                + 2 * _nbytes((tm, LANES), F32)) + 6 * _nbytes((MM_SUB_ROWS, tn), F32)
    out = pl.pallas_call(
        functools.partial(_mm_kernel, rope=rope_tabs is not None, scale=scale, plain_scale=plain_scale),
        grid=(s // tm, n // tn),
        in_specs=in_specs,
        out_specs=[pl.BlockSpec((tm, tn), lambda i, j: (i, j))] * n_out,
        out_shape=[jax.ShapeDtypeStruct((s, n), out_dtype)] * n_out,
        compiler_params=_cparams(("parallel", "parallel"), vmem),
        name="proj_rope" if rope_tabs is not None else "proj",
    )(*args)
    return out[0] if n_out == 1 else out


def _compress_kernel(x_ref, pos_ref, w1_ref, w2_ref, o_ref):
    n_cmp = o_ref.shape[0]
    hidden = w1_ref.shape[3]
    ha = [jnp.zeros((n_cmp, hidden), F32) for _ in range(NSA_GROUPS)]
    hb = [jnp.zeros((n_cmp, hidden), F32) for _ in range(NSA_GROUPS)]
    tail = [jnp.zeros((8, hidden), F32) for _ in range(NSA_GROUPS)]
    for j in range(CMP_STRIDE):
        xj = x_ref[pl.ds(j, n_cmp, stride=CMP_STRIDE), :]
        pa = pos_ref[j:j + 1, :]
        pb = pos_ref[CMP_STRIDE + j:CMP_STRIDE + j + 1, :]
        xa = (xj + pa).astype(BF16)
        xb = (xj + pb).astype(BF16)
        pb8 = jnp.broadcast_to(pb, (8, pb.shape[1])).astype(BF16)
        for g in range(NSA_GROUPS):
            ha[g] = ha[g] + jnp.dot(xa, w1_ref[g, j], preferred_element_type=F32)
            hb[g] = hb[g] + jnp.dot(xb, w1_ref[g, CMP_STRIDE + j], preferred_element_type=F32)
            tail[g] = tail[g] + jnp.dot(pb8, w1_ref[g, CMP_STRIDE + j], preferred_element_type=F32)
    last = _row_iota((n_cmp, 1)) == n_cmp - 1
    outs = []
    for g in range(NSA_GROUPS):
        hb_next = jnp.where(last, tail[g][0:1, :], pltpu.roll(hb[g], n_cmp - 1, 0))
        h = jax.nn.gelu(ha[g] + hb_next)
        outs.append(jnp.dot(h.astype(BF16), w2_ref[...], preferred_element_type=F32))
    lo = _lane_iota(outs[0].shape) < HEAD_DIM
    o_ref[...] = jnp.where(lo, outs[0], outs[1]).astype(o_ref.dtype)


def _compress(x, col, pos, w1, w2):
    s = x.shape[0]
    n_cmp = s // CMP_STRIDE
    hidden = w1.shape[1]
    pos2 = jnp.concatenate([pos] * NSA_GROUPS, axis=1)
    w1r = w1.reshape(CMP_LEN, HEAD_DIM, hidden).astype(BF16)
    zero = jnp.zeros_like(w1r)
    w1g = jnp.stack([jnp.concatenate([w1r if gg == g else zero for gg in range(NSA_GROUPS)], axis=1)
                     for g in range(NSA_GROUPS)])
    w2d = jnp.concatenate([w2, w2], axis=1).astype(BF16)
    vmem = 2 * (_nbytes((s, NSA_KV_W), F32) + _nbytes(w1g.shape, BF16)) + 16 * _nbytes((n_cmp, hidden), F32)
    return pl.pallas_call(
        _compress_kernel,
        grid=(1,),
        in_specs=[pl.BlockSpec((s, NSA_KV_W), lambda i: (0, col)), pl.BlockSpec(pos2.shape, lambda i: (0, 0)),
                  pl.BlockSpec(w1g.shape, lambda i: (0, 0, 0, 0)), pl.BlockSpec(w2d.shape, lambda i: (0, 0))],
        out_specs=pl.BlockSpec((n_cmp, NSA_KV_W), lambda i: (0, 0)),
        out_shape=jax.ShapeDtypeStruct((n_cmp, NSA_KV_W), BF16),
        compiler_params=_cparams(("arbitrary",), vmem),
        name="nsa_compress",
    )(x, pos2, w1g, w2d)


def _stack_group_queries(q_ref, col0, kv_lane_group):
    parts = []
    for h in range(NSA_HPG):
        c = col0 + (h // 2) * LANES
        qc = q_ref[:, c:c + LANES].astype(F32)
        e = h % 2
        lane = _lane_iota(qc.shape)
        mine = (lane >= e * HEAD_DIM) & (lane < (e + 1) * HEAD_DIM)
        qh = jnp.where(mine, qc, 0.0)
        if e != kv_lane_group:
            qh = pltpu.roll(qh, HEAD_DIM, 1)
        parts.append(qh.astype(BF16))
    return jnp.concatenate(parts, axis=0)


def _unstack_group_outputs(o, kv_lane_group):
    chunks = []
    tq = o.shape[0] // NSA_HPG
    for c in range(NSA_HPG // 2):
        halves = []
        for e in range(2):
            h = 2 * c + e
            oh = o[h * tq:(h + 1) * tq, :]
            if e != kv_lane_group:
                oh = pltpu.roll(oh, HEAD_DIM, 1)
            halves.append(oh)
        lo = _lane_iota(halves[0].shape) < HEAD_DIM
        chunks.append(jnp.where(lo, halves[0], halves[1]))
    return jnp.concatenate(chunks, axis=1)


def _qk(q, k):
    return lax.dot_general(q, k, (((1,), (1,)), ((), ())), preferred_element_type=F32)


def _softmax_rows(s, mask, base2=False):
    sm = jnp.where(mask, s, MASKED)
    m = jnp.maximum(jnp.max(sm, axis=-1, keepdims=True), M_INIT)
    e = jnp.exp2(sm - m) if base2 else jnp.exp(sm - m)
    return e, jnp.sum(e, axis=-1, keepdims=True), m


def _cmp_select_kernel(q_ref, kc_ref, vc_ref, mmap_ref, oc_ref, nsel_ref, *, top_k, n_cls):
    b = pl.program_id(0)
    n_cmp = kc_ref.shape[0]
    n_slc = mmap_ref.shape[0]
    tq = q_ref.shape[0]
    rows = NSA_HPG * tq
    gw = NSA_HPG * HEAD_DIM
    t4 = b * tq + (_row_iota((rows, 1)) % tq)
    t = b * tq + _lane_iota((1, tq))

    def variant(nc, nb):
        valid = _lane_iota((1, nc)) * CMP_STRIDE + (CMP_LEN - 1) <= t4
        blk = _row_iota((nb, 1))
        cur = t // SLC_BLOCK
        forced = (blk == 0) | (blk == cur) | (blk == cur - 1)
        future = blk * SLC_BLOCK > t
        blk_f = blk.astype(F32)
        kc = kc_ref[0:nc, :]
        vc = vc_ref[0:nc, :]
        mmap_t = mmap_ref[0:nb, 0:nc]
        for g in range(NSA_GROUPS):
            qst = _stack_group_queries(q_ref, g * gw, g)
            e, den, _ = _softmax_rows(_qk(qst, kc), valid)
            p = e / jnp.maximum(den, 1e-30)
            o = jnp.dot(p.astype(BF16), vc, preferred_element_type=F32)
            oc_ref[:, g * gw:(g + 1) * gw] = _unstack_group_outputs(o, g).astype(oc_ref.dtype)
            psum = p[0:tq]
            for h in range(1, NSA_HPG):
                psum = psum + p[h * tq:(h + 1) * tq]
            hi = psum.astype(BF16)
            r1 = psum - hi.astype(F32)
            mid = r1.astype(BF16)
            low = (r1 - mid.astype(F32)).astype(BF16)
            p_slc = _qk(mmap_t, hi) + _qk(mmap_t, mid) + _qk(mmap_t, low)
            score = jnp.where(forced, FORCE_SCORE, jnp.where(future, -1.0, p_slc))
            for _ in range(top_k):
                mx = jnp.max(score, axis=0, keepdims=True)
                first = jnp.min(jnp.where(score == mx, blk_f, float(nb)), axis=0, keepdims=True)
                score = jnp.where(blk_f == first, -2.0, score)
            not_picked = jnp.where(score == -2.0, 0.0, 1.0)
            nsel_ref[:, g * n_slc:g * n_slc + nb] = not_picked.T.astype(nsel_ref.dtype)
            if nb < n_slc:
                nsel_ref[:, g * n_slc + nb:(g + 1) * n_slc] = jnp.ones((tq, n_slc - nb), nsel_ref.dtype)

    for c in range(n_cls):
        nc = (c + 1) * (n_cmp // n_cls)
        nb = min(n_slc, -(-(nc // SLC_RATIO) // LANES) * LANES)
        lo_b = c * (nc // (c + 1)) // (tq // CMP_STRIDE)
        hi_b = nc // (tq // CMP_STRIDE)
        pl.when((b >= lo_b) & (b < hi_b))(functools.partial(variant, nc, nb))


def _cmp_select(q_u, kc, vc):
    s = q_u.shape[0]
    n_cmp, n_slc = s // CMP_STRIDE, s // SLC_BLOCK
    top_k = min(SLC_TOPK, n_slc)
    n_cls = max(n_cmp // (2 * LANES), 1)
    tq = min(CMP_Q_BLOCK, s)
    assert s % tq == 0 and n_cmp % n_cls == 0 and (n_cmp // n_cls) % (tq // CMP_STRIDE) == 0
    m = np.arange(n_cmp)[:, None]
    j = np.arange(n_slc)[None, :]
    mmap = jnp.asarray(((m >= SLC_RATIO * j - 1) & (m <= SLC_RATIO * j + SLC_RATIO - 1)).astype(np.float32).T, BF16)
    const = lambda b: (0, 0)
    rows = NSA_HPG * tq
    vmem = 2 * (2 * _nbytes((n_cmp, NSA_KV_W), BF16) + _nbytes((n_cmp, n_slc), BF16)
                + _nbytes((tq, NSA_Q_W), F32) * 2) + 6 * _nbytes((rows, n_cmp), F32)
    return pl.pallas_call(
        functools.partial(_cmp_select_kernel, top_k=top_k, n_cls=n_cls),
        grid=(s // tq,),
        in_specs=[pl.BlockSpec((tq, NSA_Q_W), lambda b: (b, 0)), pl.BlockSpec((n_cmp, NSA_KV_W), const),
                  pl.BlockSpec((n_cmp, NSA_KV_W), const), pl.BlockSpec((n_slc, n_cmp), const)],
        out_specs=[pl.BlockSpec((tq, NSA_Q_W), lambda b: (b, 0)),
                   pl.BlockSpec((tq, NSA_GROUPS * n_slc), lambda b: (b, 0))],
        out_shape=[jax.ShapeDtypeStruct((s, NSA_Q_W), BF16), jax.ShapeDtypeStruct((s, NSA_GROUPS * n_slc), BF16)],
        compiler_params=_cparams(("parallel",), vmem),
        name="nsa_compressed_select",
    )(q_u, kc, vc, mmap)


def _sel_attn_kernel(q_ref, k_ref, vt_ref, nsel_ref, o_ref, qa_ref, m_ref, acc_ref, s_ref, *, win):
    b = pl.program_id(0)
    n_slc = nsel_ref.shape[1] // NSA_GROUPS
    n_win = n_slc // win
    tiles_per_win = win * SLC_BLOCK // SEL_TILE
    gw = NSA_HPG * HEAD_DIM
    last = (b * Q_BLOCK) // SEL_TILE
    t_lane = b * Q_BLOCK + _lane_iota((1, Q_BLOCK))
    lane = _lane_iota((Q_BLOCK, LANES))
    head_lanes = lane < HEAD_DIM
    flag_lanes = (lane >= HEAD_DIM) & (lane < HEAD_DIM + win)

    nsel_all = nsel_ref[...].astype(F32)
    if nsel_all.shape[1] < LANES:
        nsel_all = jnp.concatenate([nsel_all, jnp.zeros((Q_BLOCK, LANES - nsel_all.shape[1]), F32)], axis=1)
    for g in range(NSA_GROUPS):
        heads = []
        for h in range(NSA_HPG):
            c = g * gw + (h // 2) * LANES
            qc = q_ref[:, c:c + LANES].astype(F32)
            heads.append(qc if h % 2 == 0 else pltpu.roll(qc, HEAD_DIM, 1))
        for w in range(n_win):
            a = g * n_slc + w * win
            fl = nsel_all[:, (a // LANES) * LANES:(a // LANES + 1) * LANES]
            shift = (HEAD_DIM - a % LANES) % LANES
            if shift:
                fl = pltpu.roll(fl, shift, 1)
            fl = jnp.where(flag_lanes, fl, 0.0)
            for h in range(NSA_HPG):
                qa_ref[g, w, h * Q_BLOCK:(h + 1) * Q_BLOCK, :] = jnp.where(head_lanes, heads[h], fl).astype(BF16)
    m_ref[...] = jnp.full(m_ref.shape, M_INIT, F32)
    acc_ref[...] = jnp.zeros(acc_ref.shape, F32)

    def scores(kt, slot):
        off = pl.multiple_of(kt * SEL_TILE, SEL_TILE)
        for g in range(NSA_GROUPS):
            s_ref[g, slot] = _qk(k_ref[g, pl.ds(off, SEL_TILE), :], qa_ref[g, kt // tiles_per_win])

    def update(kt, slot, causal):
        off = pl.multiple_of(kt * SEL_TILE, SEL_TILE)
        for g in range(NSA_GROUPS):
            s = s_ref[g, slot]
            if causal:
                keep = off + _row_iota((SEL_TILE, 1)) <= t_lane
                s = jnp.where(jnp.concatenate([keep] * NSA_HPG, axis=1), s, MASKED)
            m_old = m_ref[g]
            m_new = jnp.maximum(m_old, jnp.max(s, axis=0, keepdims=True))
            alpha = jnp.exp2(m_old - m_new)
            p = jnp.exp2(s - m_new[0:1, :]).astype(BF16)
            pv = jnp.dot(vt_ref[g, kt], p, preferred_element_type=F32)
            acc_ref[g] = alpha[0:1, :] * acc_ref[g] + pv
            m_ref[g] = m_new

    def finish():
        for g in range(NSA_GROUPS):
            acc = acc_ref[g]
            o_t = acc / jnp.maximum(acc[HEAD_DIM:HEAD_DIM + 1, :], 1e-30)
            o_t = jnp.concatenate([o_t, jnp.zeros((LANES - o_t.shape[0], o_t.shape[1]), F32)], axis=0)
            heads = [o_t[:, h * Q_BLOCK:(h + 1) * Q_BLOCK].T for h in range(NSA_HPG)]
            for c in range(NSA_HPG // 2):
                even, odd = heads[2 * c], pltpu.roll(heads[2 * c + 1], HEAD_DIM, 1)
                o_ref[:, g * gw + c * LANES:g * gw + (c + 1) * LANES] = jnp.where(head_lanes, even, odd).astype(o_ref.dtype)

    scores(0, 0)

    def pair(j, carry):
        scores(2 * j + 1, 1)
        update(2 * j, 0, False)
        scores(2 * j + 2, 0)
        update(2 * j + 1, 1, False)
        return carry

    lax.fori_loop(0, last // 2, pair, 0)

    @pl.when(last % 2 == 1)
    def _():
        scores(last, 1)
        update(last - 1, 0, False)
        update(last, 1, True)
        finish()

    @pl.when(last % 2 == 0)
    def _():
        update(last, 0, True)
        finish()


def _sel_attn(rot_a, k_s, v_s, nsel):
    s = rot_a.shape[0]
    n_slc = s // SLC_BLOCK
    win = min(n_slc, LANES - HEAD_DIM)
    assert s % SEL_TILE == 0 and n_slc % win == 0 and (win * SLC_BLOCK) % SEL_TILE == 0
    n_win = n_slc // win
    spare = LANES - HEAD_DIM
    blk = (np.arange(s) // SLC_BLOCK) % win
    marks = jnp.asarray(np.where(np.arange(spare)[None, :] == blk[:, None], MASKED, 0.0).astype(np.float32), BF16)
    k_aug = jnp.stack([jnp.concatenate([k_s[:, g * HEAD_DIM:(g + 1) * HEAD_DIM], marks], axis=1)
                       for g in range(NSA_GROUPS)])
    vt_aug = _ones_augmented_values(v_s).reshape(NSA_GROUPS, s // SEL_TILE, SEL_TILE, LANES).transpose(0, 1, 3, 2)
    vt_aug = vt_aug[:, :, :SEL_VT_ROWS, :]
    rows = NSA_HPG * Q_BLOCK
    vmem = (2 * _nbytes(k_aug.shape, BF16)
            + 2 * (_nbytes((Q_BLOCK, NSA_Q_W), BF16) + _nbytes((Q_BLOCK, NSA_GROUPS * n_slc), BF16)
                   + _nbytes((Q_BLOCK, NSA_Q_W), BF16))
            + NSA_GROUPS * (n_win * _nbytes((rows, LANES), BF16) + _nbytes((8 + SEL_VT_ROWS, rows), F32)
                            + 2 * _nbytes((SEL_TILE, rows), F32))
            + 4 * _nbytes((SEL_TILE, rows), F32))
    return pl.pallas_call(
        functools.partial(_sel_attn_kernel, win=win),
        grid=(s // Q_BLOCK,),
        in_specs=[pl.BlockSpec((Q_BLOCK, NSA_Q_W), lambda b: (b, 0)),
                  pl.BlockSpec(k_aug.shape, lambda b: (0, 0, 0), pipeline_mode=pl.Buffered(1)),
                  pl.BlockSpec(vt_aug.shape, lambda b: (0, 0, 0, 0), pipeline_mode=pl.Buffered(1)),
                  pl.BlockSpec((Q_BLOCK, NSA_GROUPS * n_slc), lambda b: (b, 0))],
        out_specs=pl.BlockSpec((Q_BLOCK, NSA_Q_W), lambda b: (b, 0)),
        out_shape=jax.ShapeDtypeStruct((s, NSA_Q_W), BF16),
        scratch_shapes=[pltpu.VMEM((NSA_GROUPS, n_win, rows, LANES), BF16),
                        pltpu.VMEM((NSA_GROUPS, 8, rows), F32), pltpu.VMEM((NSA_GROUPS, SEL_VT_ROWS, rows), F32),
                        pltpu.VMEM((NSA_GROUPS, 2, SEL_TILE, rows), F32)],
        compiler_params=_cparams(("parallel",), vmem),
        name="nsa_selected",
    )(rot_a, k_aug, vt_aug, nsel)


def _win_attn_kernel(q_ref, k_ref, v_ref, o_ref, *, span):
    b = pl.program_id(0)
    rows = NSA_HPG * Q_BLOCK
    n_back = NSA_WINDOW // Q_BLOCK
    start = pl.multiple_of(jnp.maximum(b - n_back, 0) * Q_BLOCK, Q_BLOCK)
    k = k_ref[pl.ds(start, span), :]
    t = b * Q_BLOCK + _row_iota((Q_BLOCK, 1))
    diff = t - (start + _lane_iota((1, span)))
    bias = jnp.where((diff >= 0) & (diff < NSA_WINDOW), 0.0, MASKED)
    bias = jnp.concatenate([bias] * NSA_HPG, axis=0)
    head_lanes = _lane_iota((Q_BLOCK, LANES)) < HEAD_DIM
    gw = NSA_HPG * HEAD_DIM
    for g in range(NSA_GROUPS):
        qst = _stack_group_queries(q_ref, g * gw, g)
        s = _qk(qst, k) + bias
        m = jnp.maximum(jnp.max(s, axis=-1, keepdims=True), M_INIT)
        e = jnp.exp2(s - m).astype(BF16)
        acc = jnp.dot(e, v_ref[g, pl.ds(start, span), :], preferred_element_type=F32)
        o = acc / jnp.maximum(acc[:, HEAD_DIM:HEAD_DIM + 1], 1e-30)
        for c in range(NSA_HPG // 2):
            even = o[(2 * c) * Q_BLOCK:(2 * c + 1) * Q_BLOCK, :]
            odd = pltpu.roll(o[(2 * c + 1) * Q_BLOCK:(2 * c + 2) * Q_BLOCK, :], HEAD_DIM, 1)
            o_ref[:, g * gw + c * LANES:g * gw + (c + 1) * LANES] = jnp.where(head_lanes, even, odd).astype(o_ref.dtype)


def _ones_augmented_values(v):
    s = v.shape[0]
    spare = LANES - HEAD_DIM
    ones = jnp.asarray((np.arange(spare)[None, :] == 0).astype(np.float32) * np.ones((s, 1), np.float32), BF16)
    return jnp.stack([jnp.concatenate([v[:, g * HEAD_DIM:(g + 1) * HEAD_DIM], ones], axis=1)
                      for g in range(NSA_GROUPS)])


def _win_attn(q_rot, k_rot, v_w, kw_col):
    s = q_rot.shape[0]
    span = min(NSA_WINDOW + Q_BLOCK, s)
    rows = NSA_HPG * Q_BLOCK
    v_aug = _ones_augmented_values(v_w)
    vmem = 2 * (_nbytes((s, NSA_KV_W), BF16) + _nbytes(v_aug.shape, BF16) + _nbytes((Q_BLOCK, NSA_Q_W), BF16)
                + _nbytes((Q_BLOCK, NSA_Q_W), F32)) + 8 * _nbytes((rows, span), F32)
    return pl.pallas_call(
        functools.partial(_win_attn_kernel, span=span),
        grid=(s // Q_BLOCK,),
        in_specs=[pl.BlockSpec((Q_BLOCK, NSA_Q_W), lambda b: (b, 0)),
                  pl.BlockSpec((s, NSA_KV_W), lambda b: (0, kw_col)),
                  pl.BlockSpec(v_aug.shape, lambda b: (0, 0, 0))],
        out_specs=pl.BlockSpec((Q_BLOCK, NSA_Q_W), lambda b: (b, 0)),
        out_shape=jax.ShapeDtypeStruct((s, NSA_Q_W), BF16),
        compiler_params=_cparams(("parallel",), vmem),
        name="nsa_window",
    )(q_rot, k_rot, v_aug)


def _dil_attn_kernel(q_ref, kp_ref, kc_ref, vp_ref, vc_ref, o_ref, lse_ref, kbuf, vbuf, *, span, dil, nb):
    n = pl.program_id(0)
    unit = dil * Q_BLOCK
    kbuf[0:unit, :] = kp_ref[...]
    kbuf[unit:, :] = kc_ref[...]
    vbuf[0:unit, :] = vp_ref[...]
    vbuf[unit:, :] = vc_ref[...]
    qi = _row_iota((Q_BLOCK, 1)) + Q_BLOCK
    ki = _lane_iota((1, 2 * Q_BLOCK))
    delta = qi - ki
    band_bias = jnp.where((delta >= 0) & (delta <= span), 0.0, MASKED)
    lo = _lane_iota((Q_BLOCK, LANES)) < HEAD_DIM

    def item(idx, carry):
        j = idx // dil
        base = j * unit + idx % dil
        if dil > 1:
            q_rows, kv_rows = pl.ds(base, Q_BLOCK, stride=dil), pl.ds(base, 2 * Q_BLOCK, stride=dil)
        else:
            base = pl.multiple_of(base, Q_BLOCK)
            q_rows, kv_rows = pl.ds(base, Q_BLOCK), pl.ds(base, 2 * Q_BLOCK)
        bias = jnp.where((n * nb + j == 0) & (ki < Q_BLOCK), MASKED, band_bias)
        q = q_ref[q_rows, :]
        k = kbuf[kv_rows, :].astype(BF16)
        v = vbuf[kv_rows, :].astype(BF16)
        outs, lses = [], []
        for e in range(2):
            qh = jnp.where(lo if e == 0 else ~lo, q, 0.0).astype(BF16)
            s = _qk(qh, k) + bias
            m = jnp.maximum(jnp.max(s, axis=-1, keepdims=True), M_INIT)
            ex = jnp.exp(s - m)
            den = jnp.maximum(jnp.sum(ex, axis=-1, keepdims=True), 1e-30)
            outs.append(jnp.dot(ex.astype(BF16), v, preferred_element_type=F32) / den)
            lses.append(jnp.broadcast_to(m + jnp.log(den), (Q_BLOCK, LANES)))
        o_ref[q_rows, :] = jnp.where(lo, outs[0], outs[1])
        lse_ref[q_rows, :] = jnp.where(lo, lses[0], lses[1])
        return carry

    lax.fori_loop(0, nb * dil, item, 0, unroll=4)


def _dil_attn(qb, kb, vb, gidx, win, dil):
    s = qb.shape[0]
    unit = dil * Q_BLOCK
    tb = min(max(DIL_STEP_TOKENS, unit), s)
    assert s % tb == 0 and tb % unit == 0
    nb = tb // unit
    span = win // dil
    n_chunks = DIL_OUT_W // LANES
    cur = lambda n, c: (n, gidx * n_chunks + c)
    prev = lambda n, c: (jnp.maximum(n * nb - 1, 0), gidx * n_chunks + c)
    blk, pblk = (tb, LANES), (unit, LANES)
    vmem = (2 * (5 * _nbytes(blk, F32) + 2 * _nbytes(pblk, F32)) + 2 * _nbytes((tb + unit, LANES), F32)
            + 32 * _nbytes((Q_BLOCK, 2 * Q_BLOCK), F32))
    return pl.pallas_call(
        functools.partial(_dil_attn_kernel, span=span, dil=dil, nb=nb),
        grid=(s // tb, n_chunks),
        in_specs=[pl.BlockSpec(blk, cur), pl.BlockSpec(pblk, prev), pl.BlockSpec(blk, cur),
                  pl.BlockSpec(pblk, prev), pl.BlockSpec(blk, cur)],
        out_specs=[pl.BlockSpec(blk, lambda n, c: (n, c)), pl.BlockSpec(blk, lambda n, c: (n, c))],
        out_shape=[jax.ShapeDtypeStruct((s, DIL_OUT_W), F32)] * 2,
        scratch_shapes=[pltpu.VMEM((tb + unit, LANES), F32)] * 2,
        compiler_params=_cparams(("parallel", "parallel"), vmem),
        name=f"dilated_w{win}_d{dil}",
    )(qb, kb, kb, vb, vb)


def _sgu_kernel(uv_ref, g_ref, b_ref, ws_ref, bs_ref, o_ref):
    uv = jax.nn.gelu(uv_ref[...])
    u = uv[:, :SGU_WIDTH]
    v = _layer_norm(uv[:, SGU_WIDTH:], g_ref[...], b_ref[...]).astype(BF16)
    causal = _lane_iota((SGU_CHUNK, SGU_CHUNK)) <= _row_iota((SGU_CHUNK, SGU_CHUNK))
    bs = bs_ref[...]
    for n in range(uv.shape[0] // SGU_CHUNK):
        rs = slice(n * SGU_CHUNK, (n + 1) * SGU_CHUNK)
        for g in range(SGU_GROUPS):
            cs = slice(g * SGU_GROUP_CH, (g + 1) * SGU_GROUP_CH)
            ws = jnp.where(causal, ws_ref[g], 0.0).astype(BF16)
            sv = jnp.dot(ws, v[rs, cs], preferred_element_type=F32) + bs[:, cs]
            o_ref[rs, cs] = (u[rs, cs] * sv).astype(o_ref.dtype)


def _sgu(uv, ln_g, ln_b, w_s, b_s):
    s = uv.shape[0]
    tm = min(4 * SGU_CHUNK, s)
    assert s % tm == 0
    bs = jnp.repeat(b_s.T, SGU_GROUP_CH, axis=1)
    c2 = lambda i: (0, 0)
    vmem = 2 * (_nbytes((tm, 2 * SGU_WIDTH), F32) + _nbytes((tm, SGU_WIDTH), F32)) + 6 * _nbytes((tm, 2 * SGU_WIDTH), F32)
    return pl.pallas_call(
        _sgu_kernel,
        grid=(s // tm,),
        in_specs=[pl.BlockSpec((tm, 2 * SGU_WIDTH), lambda i: (i, 0)), pl.BlockSpec((1, SGU_WIDTH), c2),
                  pl.BlockSpec((1, SGU_WIDTH), c2), pl.BlockSpec(w_s.shape, lambda i: (0, 0, 0)),
                  pl.BlockSpec((SGU_CHUNK, SGU_WIDTH), c2)],
        out_specs=pl.BlockSpec((tm, SGU_WIDTH), lambda i: (i, 0)),
        out_shape=jax.ShapeDtypeStruct((s, SGU_WIDTH), BF16),
        compiler_params=_cparams(("parallel",), vmem),
        name="sgu",
    )(uv, ln_g.reshape(1, -1), ln_b.reshape(1, -1), w_s, bs)


def _split_dot(x, w2):
    hi = x.astype(BF16)
    lo = (x - hi.astype(F32)).astype(BF16)
    return jnp.dot(jnp.concatenate([hi, lo], axis=1), w2, preferred_element_type=F32)


def _mix_kernel(x_ref, oc_ref, os_ref, ow_ref, ga_ref, eg_ref, d0_ref, d1_ref, d2_ref, l0_ref, l1_ref, l2_ref,
                sg_ref, gm_ref, wa_ref, wb_ref, wc_ref, wo_ref, g_ref, b_ref, o_ref, *, alpha):
    d = x_ref.shape[1]
    gates = _split_dot(jax.nn.sigmoid(ga_ref[...]), eg_ref[...])
    o_a = (gates[:, 0:NSA_Q_W] * oc_ref[...].astype(F32) + gates[:, NSA_Q_W:2 * NSA_Q_W] * os_ref[...].astype(F32)
           + gates[:, 2 * NSA_Q_W:3 * NSA_Q_W] * ow_ref[...].astype(F32))
    y_a = jnp.dot(o_a.astype(BF16), wa_ref[...], preferred_element_type=F32)
    l0, l1, l2 = l0_ref[...], l1_ref[...], l2_ref[...]
    lm = jnp.maximum(jnp.maximum(l0, l1), l2)
    e0, e1, e2 = jnp.exp(l0 - lm), jnp.exp(l1 - lm), jnp.exp(l2 - lm)
    den = e0 + e1 + e2
    o_b = (e0 / den) * d0_ref[...] + (e1 / den) * d1_ref[...] + (e2 / den) * d2_ref[...]
    y_b = jnp.dot(o_b.astype(BF16), wb_ref[...], preferred_element_type=F32)
    y_c = jnp.dot(sg_ref[...].astype(BF16), wc_ref[...], preferred_element_type=F32)
    gm = jax.nn.sigmoid(gm_ref[...].astype(F32))
    merged = gm[:, 0:d] * y_a + gm[:, d:2 * d] * y_b + gm[:, 2 * d:3 * d] * y_c
    mix = jnp.dot(merged.astype(BF16), wo_ref[...], preferred_element_type=F32)
    o_ref[...] = _layer_norm(alpha * x_ref[...] + mix, g_ref[...], b_ref[...])


def _mix(x, o_c, o_s, o_w, g_a, dil_o, dil_lse, sgu, g_m, wa, wb, wc, wo, g, b, alpha):
    s, d = x.shape
    tm = min(256, s)
    assert s % tm == 0
    col = np.arange(N_BRANCH * NSA_Q_W)
    head, br = (col % NSA_Q_W) // HEAD_DIM, col // NSA_Q_W
    eg1 = (np.arange(LANES)[:, None] == (head * N_BRANCH + br)[None, :]).astype(np.float32)
    eg = jnp.asarray(np.concatenate([eg1, eg1], axis=0), BF16)
    row = lambda i: (i, 0)
    const = lambda i: (0, 0)
    rspec = lambda w: pl.BlockSpec((tm, w), row)
    cspec = lambda a: pl.BlockSpec(a.shape, const)
    widths = [d, NSA_Q_W, NSA_Q_W, NSA_Q_W, LANES] + [DIL_OUT_W] * 6 + [SGU_WIDTH, N_BRANCH * d]
    vmem = (2 * sum(_nbytes((tm, w), F32) for w in widths) + 2 * _nbytes((tm, d), F32)
            + 2 * sum(_nbytes(a.shape, BF16) for a in (eg, wa, wb, wc, wo)) + 10 * _nbytes((tm, N_BRANCH * d), F32))
    return pl.pallas_call(
        functools.partial(_mix_kernel, alpha=alpha),
        grid=(s // tm,),
        in_specs=[rspec(d), rspec(NSA_Q_W), rspec(NSA_Q_W), rspec(NSA_Q_W), rspec(LANES), cspec(eg)]
                 + [rspec(DIL_OUT_W)] * 6 + [rspec(SGU_WIDTH), rspec(N_BRANCH * d)]
                 + [cspec(wa), cspec(wb), cspec(wc), cspec(wo), pl.BlockSpec((1, d), const), pl.BlockSpec((1, d), const)],
        out_specs=rspec(d),
        out_shape=jax.ShapeDtypeStruct((s, d), F32),
        compiler_params=_cparams(("parallel",), vmem),
        name="mixer_merge_deepnorm",
    )(x, o_c, o_s, o_w, g_a, eg, *dil_o, *dil_lse, sgu, g_m, wa, wb, wc, wo, g.reshape(1, d), b.reshape(1, d))


def _rope_tables(s):
    inv = 1.0 / (ROPE_THETA ** (np.arange(0, HEAD_DIM, 2, dtype=np.float32) / HEAD_DIM)).astype(np.float32)
    ang = (np.arange(s, dtype=np.float32)[:, None] * inv[None, :]).astype(np.float64)
    cos, sin = jnp.asarray(np.cos(ang), F32), jnp.asarray(np.sin(ang), F32)
    reps = LANES // HEAD_DIM
    return jnp.tile(jnp.concatenate([cos, cos], axis=1), (1, reps)), jnp.tile(jnp.concatenate([-sin, sin], axis=1), (1, reps))


def kernel(x, ln_g, ln_b, ffn1_gate, ffn1_up, ffn1_down, ffn2_gate, ffn2_up, ffn2_down, w_in, phi_k_pos, phi_k_w1, phi_k_w2, phi_v_pos, phi_v_w1, phi_v_w2, sgu_ln_g, sgu_ln_b, sgu_w, sgu_b, w_branch_a, w_branch_b, w_branch_c, w_out):
    bsz, s, d = x.shape
    depth = ln_g.shape[0]
    alpha = float((2 * depth) ** 0.25)
    q_scale = float(HEAD_DIM ** -0.5)
    tabs = _rope_tables(s)
    sizes = [NSA_Q_W] + [NSA_KV_W] * 6 + [N_BRANCH * NSA_HEADS, DIL_W, DIL_W, DIL_W, 2 * SGU_WIDTH, N_BRANCH * d]
    offs = np.concatenate([[0], np.cumsum(sizes)]).tolist()
    outs = []
    for bi in range(bsz):
        h = x[bi]
        for l in range(depth):
            (w_qa, w_kc, w_vc, w_ks, w_vs, w_kw, w_vw, w_ga, w_qb, w_kb, w_vb, w_uv, w_gm) = [
                w_in[l][:, offs[i]:offs[i + 1]] for i in range(len(sizes))]
            w_ga = jnp.pad(w_ga, ((0, 0), (0, LANES - w_ga.shape[1])))
            groups = dict(qb=w_qb, kb=w_kb, vb=w_vb, k_rot=jnp.concatenate([w_ks, w_kw], axis=1),
                          kv_c=jnp.concatenate([w_kc, w_vc], axis=1), v_plain=jnp.concatenate([w_vs, w_vw], axis=1),
                          uv=w_uv, gm=w_gm, qa=w_qa, ga=w_ga)
            wp = jnp.concatenate(list(groups.values()), axis=1).astype(BF16)
            ends = np.cumsum([g.shape[1] for g in groups.values()]).tolist()
            cols = {name: (e - g.shape[1], e) for (name, g), e in zip(groups.items(), ends)}

            h, hb = _ffn(h, ffn1_gate[l].astype(BF16), ffn1_up[l].astype(BF16), ffn1_down[l].astype(BF16),
                         ln_g[l, 0], ln_b[l, 0], alpha)

            q_rot, q_u = _mm(hb, wp, cols["qa"], BF16, rope_tabs=tabs, scale=q_scale * LOG2E, plain_scale=q_scale)
            k_rot = _mm(hb, wp, cols["k_rot"], BF16, rope_tabs=tabs)
            qb = _mm(hb, wp, cols["qb"], F32, rope_tabs=tabs, scale=q_scale)
            kb = _mm(hb, wp, cols["kb"], F32, rope_tabs=tabs)
            vb = _mm(hb, wp, cols["vb"], F32)
            kv_c = _mm(hb, wp, cols["kv_c"], F32)
            v_plain = _mm(hb, wp, cols["v_plain"], BF16)
            uv = _mm(hb, wp, cols["uv"], F32)
            g_m = _mm(hb, wp, cols["gm"], BF16)
            g_a = _mm(hb, wp, cols["ga"], F32)

            kc = _compress(kv_c, 0, phi_k_pos[l], phi_k_w1[l], phi_k_w2[l])
            vc = _compress(kv_c, 1, phi_v_pos[l], phi_v_w1[l], phi_v_w2[l])
            o_c, nsel = _cmp_select(q_u, kc, vc)
            o_s = _sel_attn(q_rot, k_rot[:, :NSA_KV_W], v_plain[:, :NSA_KV_W], nsel)
            o_w = _win_attn(q_rot, k_rot, v_plain[:, NSA_KV_W:], 1)
            dil = [_dil_attn(qb, kb, vb, gi, win, dl) for gi, (win, dl) in enumerate(DIL_PAIRS)]
            sg = _sgu(uv, sgu_ln_g[l], sgu_ln_b[l], sgu_w[l], sgu_b[l])
            h = _mix(h, o_c, o_s, o_w, g_a, [o for o, _ in dil], [e for _, e in dil], sg, g_m,
                     w_branch_a[l].astype(BF16), w_branch_b[l].astype(BF16), w_branch_c[l].astype(BF16),
                     w_out[l].astype(BF16), ln_g[l, 1], ln_b[l, 1], alpha)

            h, _ = _ffn(h, ffn2_gate[l].astype(BF16), ffn2_up[l].astype(BF16), ffn2_down[l].astype(BF16),
                        ln_g[l, 2], ln_b[l, 2], alpha)
        outs.append(h)
    return jnp.stack(outs, axis=0)
```

```python
import functools

import numpy as np
import jax
import jax.numpy as jnp
from jax import lax
from jax.experimental import pallas as pl
from jax.experimental.pallas import tpu as pltpu

F32 = jnp.float32
BF16 = jnp.bfloat16

HEAD_DIM = 64
ROPE_THETA = 10000.0
LN_EPS = 1e-5
Q_BLOCK = 128
NSA_HEADS = 8
NSA_GROUPS = 2
NSA_HPG = NSA_HEADS // NSA_GROUPS
NSA_Q_W = NSA_HEADS * HEAD_DIM
NSA_KV_W = NSA_GROUPS * HEAD_DIM
CMP_LEN = 32
CMP_STRIDE = 16
SLC_BLOCK = 64
SLC_RATIO = SLC_BLOCK // CMP_STRIDE
SLC_TOPK = 16
MM_ROWS = 2048
MM_SUB_ROWS = 512
FFN_ROWS = 1024
CMP_Q_BLOCK = 256
DIL_STEP_TOKENS = 2048
SEL_TILE = 1024
SEL_VT_ROWS = 80
NSA_WINDOW = 512
FORCE_SCORE = 1e6
DIL_PAIRS = ((128, 1), (512, 4), (2048, 16))
DIL_HPG = 4
DIL_HEADS = DIL_HPG * 3
DIL_W = DIL_HEADS * HEAD_DIM
DIL_OUT_W = DIL_HPG * HEAD_DIM
SGU_CHUNK = 128
SGU_GROUPS = 4
SGU_GROUP_CH = 128
SGU_WIDTH = SGU_GROUPS * SGU_GROUP_CH
N_BRANCH = 3

LANES = 128
V7X_VMEM_BYTES = 64 * 1024 * 1024
VMEM_CAP = 56 * 1024 * 1024

M_INIT = -1e30
MASKED = -(2.0 ** 101)
LOG2E = 1.4426950408889634


def _cparams(sem, vmem_bytes):
    return pltpu.CompilerParams(dimension_semantics=sem,
                                vmem_limit_bytes=int(min(max(vmem_bytes, 16 * 1024 * 1024), VMEM_CAP)))


def _nbytes(shape, dtype):
    return int(np.prod(shape)) * jnp.dtype(dtype).itemsize


def _layer_norm(y, g, b):
    mu = jnp.mean(y, axis=-1, keepdims=True)
    d = y - mu
    var = jnp.mean(d * d, axis=-1, keepdims=True)
    return d * lax.rsqrt(var + LN_EPS) * g + b


def _lane_iota(shape):
    return lax.broadcasted_iota(jnp.int32, shape, len(shape) - 1)


def _row_iota(shape):
    return lax.broadcasted_iota(jnp.int32, shape, 0)


def _ffn_kernel(x_ref, wg_ref, wu_ref, wd_ref, g_ref, b_ref, o_ref, ob_ref, *, alpha, fc):
    n_f = wg_ref.shape[1]
    for r0 in range(0, x_ref.shape[0], MM_SUB_ROWS):
        rs = slice(r0, r0 + MM_SUB_ROWS)
        x = x_ref[rs, :]
        xb = x.astype(BF16)
        acc = jnp.zeros(x.shape, F32)
        for c in range(n_f // fc):
            sl = slice(c * fc, (c + 1) * fc)
            gate = jnp.dot(xb, wg_ref[:, sl], preferred_element_type=F32)
            up = jnp.dot(xb, wu_ref[:, sl], preferred_element_type=F32)
            h = (gate * jax.nn.sigmoid(gate)) * up
            acc = acc + jnp.dot(h.astype(BF16), wd_ref[sl, :], preferred_element_type=F32)
        out = _layer_norm(alpha * x + 0.5 * acc, g_ref[...], b_ref[...])
        o_ref[rs, :] = out
        ob_ref[rs, :] = out.astype(BF16)


def _ffn(x, wg, wu, wd, g, b, alpha):
    s, d = x.shape
    n_f = wg.shape[1]
    tm = min(FFN_ROWS, s)
    fc = 256
    assert s % tm == 0 and tm % MM_SUB_ROWS == 0 and n_f % fc == 0
    const = lambda i: (0, 0)
    row = lambda i: (i, 0)
    once = pl.Buffered(1)
    vmem = (2 * 2 * _nbytes((tm, d), F32) + 2 * _nbytes((tm, d), BF16)
            + 3 * _nbytes((d, n_f), BF16) + 8 * _nbytes((MM_SUB_ROWS, d), F32))
    return pl.pallas_call(
        functools.partial(_ffn_kernel, alpha=alpha, fc=fc),
        grid=(s // tm,),
        in_specs=[pl.BlockSpec((tm, d), row), pl.BlockSpec((d, n_f), const, pipeline_mode=once),
                  pl.BlockSpec((d, n_f), const, pipeline_mode=once),
                  pl.BlockSpec((n_f, d), const, pipeline_mode=once),
                  pl.BlockSpec((1, d), const), pl.BlockSpec((1, d), const)],
        out_specs=[pl.BlockSpec((tm, d), row), pl.BlockSpec((tm, d), row)],
        out_shape=[jax.ShapeDtypeStruct((s, d), F32), jax.ShapeDtypeStruct((s, d), BF16)],
        compiler_params=_cparams(("parallel",), vmem),
        name="ffn_deepnorm",
    )(x, wg, wu, wd, g.reshape(1, d), b.reshape(1, d))


def _swap_halves_64(a):
    half = HEAD_DIM // 2
    first = (_lane_iota(a.shape) % HEAD_DIM) < half
    return jnp.where(first, pltpu.roll(a, LANES - half, 1), pltpu.roll(a, half, 1))


def _mm_kernel(*refs, rope, scale, plain_scale):
    if rope:
        x_ref, w_ref, c_ref, s_ref, o_ref = refs[:5]
    else:
        x_ref, w_ref, o_ref = refs
    for r0 in range(0, x_ref.shape[0], MM_SUB_ROWS):
        rs = slice(r0, r0 + MM_SUB_ROWS)
        acc = jnp.dot(x_ref[rs, :], w_ref[...], preferred_element_type=F32)
        if plain_scale is not None:
            refs[5][rs, :] = (acc * plain_scale).astype(refs[5].dtype)
        if rope:
            cos = c_ref[rs, :]
            sin = s_ref[rs, :]
            for c in range(acc.shape[1] // LANES):
                a = acc[:, c * LANES:(c + 1) * LANES]
                r = a * cos + _swap_halves_64(a) * sin
                if scale != 1.0:
                    r = r * scale
                o_ref[rs, c * LANES:(c + 1) * LANES] = r.astype(o_ref.dtype)
        else:
            if scale != 1.0:
                acc = acc * scale
            o_ref[rs, :] = acc.astype(o_ref.dtype)


def _mm(xb, w, cols, out_dtype, *, rope_tabs=None, scale=1.0, plain_scale=None):
    s, k = xb.shape
    col0, n = cols[0], cols[1] - cols[0]
    tm = min(MM_ROWS, s)
    tn = next(c for c in (1024, 768, 512, 256, LANES) if n % c == 0 and col0 % c == 0)
    assert s % tm == 0 and tm % MM_SUB_ROWS == 0 and (plain_scale is None or rope_tabs is not None)
    jb = col0 // tn
    n_out = 1 if plain_scale is None else 2
    in_specs = [pl.BlockSpec((tm, k), lambda i, j: (i, 0)), pl.BlockSpec((k, tn), lambda i, j: (0, jb + j))]
    args = [xb, w]
    if rope_tabs is not None:
        in_specs += [pl.BlockSpec((tm, LANES), lambda i, j: (i, 0))] * 2
        args += list(rope_tabs)
    vmem = 2 * (_nbytes((tm, k), BF16) + _nbytes((k, tn), BF16) + n_out * _nbytes((tm, tn), out_dtype)
                + 2 * _nbytes((tm, LANES), F32)) + 6 * _nbytes((MM_SUB_ROWS, tn), F32)
    out = pl.pallas_call(
        functools.partial(_mm_kernel, rope=rope_tabs is not None, scale=scale, plain_scale=plain_scale),
        grid=(s // tm, n // tn),
        in_specs=in_specs,
        out_specs=[pl.BlockSpec((tm, tn), lambda i, j: (i, j))] * n_out,
        out_shape=[jax.ShapeDtypeStruct((s, n), out_dtype)] * n_out,
        compiler_params=_cparams(("parallel", "parallel"), vmem),
        name="proj_rope" if rope_tabs is not None else "proj",
    )(*args)
    return out[0] if n_out == 1 else out


def _compress_kernel(x_ref, pos_ref, w1_ref, w2_ref, o_ref):
    n_cmp = o_ref.shape[0]
    hidden = w1_ref.shape[3]
    ha = [jnp.zeros((n_cmp, hidden), F32) for _ in range(NSA_GROUPS)]
    hb = [jnp.zeros((n_cmp, hidden), F32) for _ in range(NSA_GROUPS)]
    tail = [jnp.zeros((8, hidden), F32) for _ in range(NSA_GROUPS)]
    for j in range(CMP_STRIDE):
        xj = x_ref[pl.ds(j, n_cmp, stride=CMP_STRIDE), :]
        pa = pos_ref[j:j + 1, :]
        pb = pos_ref[CMP_STRIDE + j:CMP_STRIDE + j + 1, :]
        xa = (xj + pa).astype(BF16)
        xb = (xj + pb).astype(BF16)
        pb8 = jnp.broadcast_to(pb, (8, pb.shape[1])).astype(BF16)
        for g in range(NSA_GROUPS):
            ha[g] = ha[g] + jnp.dot(xa, w1_ref[g, j], preferred_element_type=F32)
            hb[g] = hb[g] + jnp.dot(xb, w1_ref[g, CMP_STRIDE + j], preferred_element_type=F32)
            tail[g] = tail[g] + jnp.dot(pb8, w1_ref[g, CMP_STRIDE + j], preferred_element_type=F32)
    last = _row_iota((n_cmp, 1)) == n_cmp - 1
    outs = []
    for g in range(NSA_GROUPS):
        hb_next = jnp.where(last, tail[g][0:1, :], pltpu.roll(hb[g], n_cmp - 1, 0))
        h = jax.nn.gelu(ha[g] + hb_next)
        outs.append(jnp.dot(h.astype(BF16), w2_ref[...], preferred_element_type=F32))
    lo = _lane_iota(outs[0].shape) < HEAD_DIM
    o_ref[...] = jnp.where(lo, outs[0], outs[1]).astype(o_ref.dtype)


def _compress(x, col, pos, w1, w2):
    s = x.shape[0]
    n_cmp = s // CMP_STRIDE
    hidden = w1.shape[1]
    pos2 = jnp.concatenate([pos] * NSA_GROUPS, axis=1)
    w1r = w1.reshape(CMP_LEN, HEAD_DIM, hidden).astype(BF16)
    zero = jnp.zeros_like(w1r)
    w1g = jnp.stack([jnp.concatenate([w1r if gg == g else zero for gg in range(NSA_GROUPS)], axis=1)
                     for g in range(NSA_GROUPS)])
    w2d = jnp.concatenate([w2, w2], axis=1).astype(BF16)
    vmem = 2 * (_nbytes((s, NSA_KV_W), F32) + _nbytes(w1g.shape, BF16)) + 16 * _nbytes((n_cmp, hidden), F32)
    return pl.pallas_call(
        _compress_kernel,
        grid=(1,),
        in_specs=[pl.BlockSpec((s, NSA_KV_W), lambda i: (0, col)), pl.BlockSpec(pos2.shape, lambda i: (0, 0)),
                  pl.BlockSpec(w1g.shape, lambda i: (0, 0, 0, 0)), pl.BlockSpec(w2d.shape, lambda i: (0, 0))],
        out_specs=pl.BlockSpec((n_cmp, NSA_KV_W), lambda i: (0, 0)),
        out_shape=jax.ShapeDtypeStruct((n_cmp, NSA_KV_W), BF16),
        compiler_params=_cparams(("arbitrary",), vmem),
        name="nsa_compress",
    )(x, pos2, w1g, w2d)


def _stack_group_queries(q_ref, col0, kv_lane_group):
    parts = []
    for h in range(NSA_HPG):
        c = col0 + (h // 2) * LANES
        qc = q_ref[:, c:c + LANES].astype(F32)
        e = h % 2
        lane = _lane_iota(qc.shape)
        mine = (lane >= e * HEAD_DIM) & (lane < (e + 1) * HEAD_DIM)
        qh = jnp.where(mine, qc, 0.0)
        if e != kv_lane_group:
            qh = pltpu.roll(qh, HEAD_DIM, 1)
        parts.append(qh.astype(BF16))
    return jnp.concatenate(parts, axis=0)


def _unstack_group_outputs(o, kv_lane_group):
    chunks = []
    tq = o.shape[0] // NSA_HPG
    for c in range(NSA_HPG // 2):
        halves = []
        for e in range(2):
            h = 2 * c + e
            oh = o[h * tq:(h + 1) * tq, :]
            if e != kv_lane_group:
                oh = pltpu.roll(oh, HEAD_DIM, 1)
            halves.append(oh)
        lo = _lane_iota(halves[0].shape) < HEAD_DIM
        chunks.append(jnp.where(lo, halves[0], halves[1]))
    return jnp.concatenate(chunks, axis=1)


def _qk(q, k):
    return lax.dot_general(q, k, (((1,), (1,)), ((), ())), preferred_element_type=F32)


def _cmp_select_kernel(q_ref, kc_ref, vc_ref, mmap_ref, oc_ref, nsel_ref, *, top_k, n_cls):
    b = pl.program_id(0)
    n_cmp = kc_ref.shape[0]
    n_slc = mmap_ref.shape[0]
    tq = q_ref.shape[0]
    rows = NSA_HPG * tq
    gw = NSA_HPG * HEAD_DIM
    t_row = b * tq + _row_iota((tq, 1))
    t = b * tq + _lane_iota((1, tq))

    def variant(nc, nb):
        valid = _lane_iota((1, nc)) * CMP_STRIDE + (CMP_LEN - 1) <= t_row
        bias = jnp.concatenate([jnp.where(valid, 0.0, MASKED)] * NSA_HPG, axis=0)
        blk = _row_iota((nb, 1))
        cur = t // SLC_BLOCK
        forced = (blk == 0) | (blk == cur) | (blk == cur - 1)
        future = blk * SLC_BLOCK > t
        blk_f = blk.astype(F32)
        kc = kc_ref[0:nc, :]
        vc = vc_ref[0:nc, :]
        mmap_t = mmap_ref[0:nb, 0:nc]
        for g in range(NSA_GROUPS):
            qst = _stack_group_queries(q_ref, g * gw, g)
            s = _qk(qst, kc) + bias
            m = jnp.maximum(jnp.max(s, axis=-1, keepdims=True), M_INIT)
            e = jnp.exp2(s - m)
            den = jnp.sum(e, axis=-1, keepdims=True)
            p = e / jnp.maximum(den, 1e-30)
            o = jnp.dot(p.astype(BF16), vc, preferred_element_type=F32)
            oc_ref[:, g * gw:(g + 1) * gw] = _unstack_group_outputs(o, g).astype(oc_ref.dtype)
            psum = p[0:tq]
            for h in range(1, NSA_HPG):
                psum = psum + p[h * tq:(h + 1) * tq]
            hi = psum.astype(BF16)
            r1 = psum - hi.astype(F32)
            mid = r1.astype(BF16)
            low = (r1 - mid.astype(F32)).astype(BF16)
            p_slc = _qk(mmap_t, hi) + _qk(mmap_t, mid) + _qk(mmap_t, low)
            score = jnp.where(forced, FORCE_SCORE, jnp.where(future, -1.0, p_slc))
            for _ in range(top_k):
                mx = jnp.max(score, axis=0, keepdims=True)
                first = jnp.min(jnp.where(score == mx, blk_f, float(nb)), axis=0, keepdims=True)
                score = jnp.where(blk_f == first, -2.0, score)
            not_picked = jnp.where(score == -2.0, 0.0, 1.0)
            nsel_ref[:, g * n_slc:g * n_slc + nb] = not_picked.T.astype(nsel_ref.dtype)
            if nb < n_slc:
                nsel_ref[:, g * n_slc + nb:(g + 1) * n_slc] = jnp.ones((tq, n_slc - nb), nsel_ref.dtype)

    for c in range(n_cls):
        nc = (c + 1) * (n_cmp // n_cls)
        nb = min(n_slc, -(-(nc // SLC_RATIO) // LANES) * LANES)
        lo_b = c * (nc // (c + 1)) // (tq // CMP_STRIDE)
        hi_b = nc // (tq // CMP_STRIDE)
        pl.when((b >= lo_b) & (b < hi_b))(functools.partial(variant, nc, nb))


def _cmp_select(q_u, kc, vc):
    s = q_u.shape[0]
    n_cmp, n_slc = s // CMP_STRIDE, s // SLC_BLOCK
    top_k = min(SLC_TOPK, n_slc)
    n_cls = max(n_cmp // (2 * LANES), 1)
    tq = min(CMP_Q_BLOCK, s)
    assert s % tq == 0 and n_cmp % n_cls == 0 and (n_cmp // n_cls) % (tq // CMP_STRIDE) == 0
    m = np.arange(n_cmp)[:, None]
    j = np.arange(n_slc)[None, :]
    mmap = jnp.asarray(((m >= SLC_RATIO * j - 1) & (m <= SLC_RATIO * j + SLC_RATIO - 1)).astype(np.float32).T, BF16)
    const = lambda b: (0, 0)
    rows = NSA_HPG * tq
    vmem = 2 * (2 * _nbytes((n_cmp, NSA_KV_W), BF16) + _nbytes((n_cmp, n_slc), BF16)
                + _nbytes((tq, NSA_Q_W), F32) * 2) + 6 * _nbytes((rows, n_cmp), F32)
    return pl.pallas_call(
        functools.partial(_cmp_select_kernel, top_k=top_k, n_cls=n_cls),
        grid=(s // tq,),
        in_specs=[pl.BlockSpec((tq, NSA_Q_W), lambda b: (b, 0)), pl.BlockSpec((n_cmp, NSA_KV_W), const),
                  pl.BlockSpec((n_cmp, NSA_KV_W), const), pl.BlockSpec((n_slc, n_cmp), const)],
        out_specs=[pl.BlockSpec((tq, NSA_Q_W), lambda b: (b, 0)),
                   pl.BlockSpec((tq, NSA_GROUPS * n_slc), lambda b: (b, 0))],
        out_shape=[jax.ShapeDtypeStruct((s, NSA_Q_W), BF16), jax.ShapeDtypeStruct((s, NSA_GROUPS * n_slc), BF16)],
        compiler_params=_cparams(("parallel",), vmem),
        name="nsa_compressed_select",
    )(q_u, kc, vc, mmap)


def _sel_attn_kernel(q_ref, k_ref, vt_ref, nsel_ref, o_ref, qa_ref, m_ref, acc_ref, s_ref, *, win):
    b = pl.program_id(0)
    n_slc = nsel_ref.shape[1] // NSA_GROUPS
    n_win = n_slc // win
    tiles_per_win = win * SLC_BLOCK // SEL_TILE
    gw = NSA_HPG * HEAD_DIM
    last = (b * Q_BLOCK) // SEL_TILE
    t_lane = b * Q_BLOCK + _lane_iota((1, Q_BLOCK))
    lane = _lane_iota((Q_BLOCK, LANES))
    head_lanes = lane < HEAD_DIM
    flag_lanes = (lane >= HEAD_DIM) & (lane < HEAD_DIM + win)

    nsel_all = nsel_ref[...].astype(F32)
    if nsel_all.shape[1] < LANES:
        nsel_all = jnp.concatenate([nsel_all, jnp.zeros((Q_BLOCK, LANES - nsel_all.shape[1]), F32)], axis=1)
    for g in range(NSA_GROUPS):
        heads = []
        for h in range(NSA_HPG):
            c = g * gw + (h // 2) * LANES
            qc = q_ref[:, c:c + LANES].astype(F32)
            heads.append(qc if h % 2 == 0 else pltpu.roll(qc, HEAD_DIM, 1))
        for w in range(n_win):
            a = g * n_slc + w * win
            fl = nsel_all[:, (a // LANES) * LANES:(a // LANES + 1) * LANES]
            shift = (HEAD_DIM - a % LANES) % LANES
            if shift:
                fl = pltpu.roll(fl, shift, 1)
            fl = jnp.where(flag_lanes, fl, 0.0)
            for h in range(NSA_HPG):
                qa_ref[g, w, h * Q_BLOCK:(h + 1) * Q_BLOCK, :] = jnp.where(head_lanes, heads[h], fl).astype(BF16)
    m_ref[...] = jnp.full(m_ref.shape, M_INIT, F32)
    acc_ref[...] = jnp.zeros(acc_ref.shape, F32)

    def scores(kt, slot):
        off = pl.multiple_of(kt * SEL_TILE, SEL_TILE)
        for g in range(NSA_GROUPS):
            s_ref[g, slot] = _qk(k_ref[g, pl.ds(off, SEL_TILE), :], qa_ref[g, kt // tiles_per_win])

    def update(kt, slot, causal):
        off = pl.multiple_of(kt * SEL_TILE, SEL_TILE)
        for g in range(NSA_GROUPS):
            s = s_ref[g, slot]
            if causal:
                keep = off + _row_iota((SEL_TILE, 1)) <= t_lane
                s = jnp.where(jnp.concatenate([keep] * NSA_HPG, axis=1), s, MASKED)
            m_old = m_ref[g]
            m_new = jnp.maximum(m_old, jnp.max(s, axis=0, keepdims=True))
            alpha = jnp.exp2(m_old - m_new)
            p = jnp.exp2(s - m_new[0:1, :]).astype(BF16)
            pv = jnp.dot(vt_ref[g, kt], p, preferred_element_type=F32)
            acc_ref[g] = alpha[0:1, :] * acc_ref[g] + pv
            m_ref[g] = m_new

    def finish():
        for g in range(NSA_GROUPS):
            acc = acc_ref[g]
            o_t = acc / jnp.maximum(acc[HEAD_DIM:HEAD_DIM + 1, :], 1e-30)
            o_t = jnp.concatenate([o_t, jnp.zeros((LANES - o_t.shape[0], o_t.shape[1]), F32)], axis=0)
            heads = [o_t[:, h * Q_BLOCK:(h + 1) * Q_BLOCK].T for h in range(NSA_HPG)]
            for c in range(NSA_HPG // 2):
                even, odd = heads[2 * c], pltpu.roll(heads[2 * c + 1], HEAD_DIM, 1)
                o_ref[:, g * gw + c * LANES:g * gw + (c + 1) * LANES] = jnp.where(head_lanes, even, odd).astype(o_ref.dtype)

    scores(0, 0)

    def pair(j, carry):
        scores(2 * j + 1, 1)
        update(2 * j, 0, False)
        scores(2 * j + 2, 0)
        update(2 * j + 1, 1, False)
        return carry

    lax.fori_loop(0, last // 2, pair, 0)

    @pl.when(last % 2 == 1)
    def _():
        scores(last, 1)
        update(last - 1, 0, False)
        update(last, 1, True)
        finish()

    @pl.when(last % 2 == 0)
    def _():
        update(last, 0, True)
        finish()


def _sel_attn(rot_a, k_s, v_s, nsel):
    s = rot_a.shape[0]
    n_slc = s // SLC_BLOCK
    win = min(n_slc, LANES - HEAD_DIM)
    assert s % SEL_TILE == 0 and n_slc % win == 0 and (win * SLC_BLOCK) % SEL_TILE == 0
    n_win = n_slc // win
    spare = LANES - HEAD_DIM
    blk = (np.arange(s) // SLC_BLOCK) % win
    marks = jnp.asarray(np.where(np.arange(spare)[None, :] == blk[:, None], MASKED, 0.0).astype(np.float32), BF16)
    k_aug = jnp.stack([jnp.concatenate([k_s[:, g * HEAD_DIM:(g + 1) * HEAD_DIM], marks], axis=1)
                       for g in range(NSA_GROUPS)])
    vt_aug = _ones_augmented_values(v_s).reshape(NSA_GROUPS, s // SEL_TILE, SEL_TILE, LANES).transpose(0, 1, 3, 2)
    vt_aug = vt_aug[:, :, :SEL_VT_ROWS, :]
    rows = NSA_HPG * Q_BLOCK
    vmem = (2 * _nbytes(k_aug.shape, BF16)
            + 2 * (_nbytes((Q_BLOCK, NSA_Q_W), BF16) + _nbytes((Q_BLOCK, NSA_GROUPS * n_slc), BF16)
                   + _nbytes((Q_BLOCK, NSA_Q_W), BF16))
            + NSA_GROUPS * (n_win * _nbytes((rows, LANES), BF16) + _nbytes((8 + SEL_VT_ROWS, rows), F32)
                            + 2 * _nbytes((SEL_TILE, rows), F32))
            + 4 * _nbytes((SEL_TILE, rows), F32))
    return pl.pallas_call(
        functools.partial(_sel_attn_kernel, win=win),
        grid=(s // Q_BLOCK,),
        in_specs=[pl.BlockSpec((Q_BLOCK, NSA_Q_W), lambda b: (b, 0)),
                  pl.BlockSpec(k_aug.shape, lambda b: (0, 0, 0), pipeline_mode=pl.Buffered(1)),
                  pl.BlockSpec(vt_aug.shape, lambda b: (0, 0, 0, 0), pipeline_mode=pl.Buffered(1)),
                  pl.BlockSpec((Q_BLOCK, NSA_GROUPS * n_slc), lambda b: (b, 0))],
        out_specs=pl.BlockSpec((Q_BLOCK, NSA_Q_W), lambda b: (b, 0)),
        out_shape=jax.ShapeDtypeStruct((s, NSA_Q_W), BF16),
        scratch_shapes=[pltpu.VMEM((NSA_GROUPS, n_win, rows, LANES), BF16),
                        pltpu.VMEM((NSA_GROUPS, 8, rows), F32), pltpu.VMEM((NSA_GROUPS, SEL_VT_ROWS, rows), F32),
                        pltpu.VMEM((NSA_GROUPS, 2, SEL_TILE, rows), F32)],
        compiler_params=_cparams(("parallel",), vmem),
        name="nsa_selected",
    )(rot_a, k_aug, vt_aug, nsel)


def _win_attn_kernel(q_ref, k_ref, v_ref, o_ref, *, span):
    b = pl.program_id(0)
    rows = NSA_HPG * Q_BLOCK
    n_back = NSA_WINDOW // Q_BLOCK
    start = pl.multiple_of(jnp.maximum(b - n_back, 0) * Q_BLOCK, Q_BLOCK)
    k = k_ref[pl.ds(start, span), :]
    t = b * Q_BLOCK + _row_iota((Q_BLOCK, 1))
    diff = t - (start + _lane_iota((1, span)))
    bias = jnp.where((diff >= 0) & (diff < NSA_WINDOW), 0.0, MASKED)
    bias = jnp.concatenate([bias] * NSA_HPG, axis=0)
    head_lanes = _lane_iota((Q_BLOCK, LANES)) < HEAD_DIM
    gw = NSA_HPG * HEAD_DIM
    for g in range(NSA_GROUPS):
        qst = _stack_group_queries(q_ref, g * gw, g)
        s = _qk(qst, k) + bias
        m = jnp.maximum(jnp.max(s, axis=-1, keepdims=True), M_INIT)
        e = jnp.exp2(s - m).astype(BF16)
        acc = jnp.dot(e, v_ref[g, pl.ds(start, span), :], preferred_element_type=F32)
        o = acc / jnp.maximum(acc[:, HEAD_DIM:HEAD_DIM + 1], 1e-30)
        for c in range(NSA_HPG // 2):
            even = o[(2 * c) * Q_BLOCK:(2 * c + 1) * Q_BLOCK, :]
            odd = pltpu.roll(o[(2 * c + 1) * Q_BLOCK:(2 * c + 2) * Q_BLOCK, :], HEAD_DIM, 1)
            o_ref[:, g * gw + c * LANES:g * gw + (c + 1) * LANES] = jnp.where(head_lanes, even, odd).astype(o_ref.dtype)


def _ones_augmented_values(v):
    s = v.shape[0]
    spare = LANES - HEAD_DIM
    ones = jnp.asarray((np.arange(spare)[None, :] == 0).astype(np.float32) * np.ones((s, 1), np.float32), BF16)
    return jnp.stack([jnp.concatenate([v[:, g * HEAD_DIM:(g + 1) * HEAD_DIM], ones], axis=1)
                      for g in range(NSA_GROUPS)])


def _win_attn(q_rot, k_rot, v_w, kw_col):
    s = q_rot.shape[0]
    span = min(NSA_WINDOW + Q_BLOCK, s)
    rows = NSA_HPG * Q_BLOCK
    v_aug = _ones_augmented_values(v_w)
    vmem = 2 * (_nbytes((s, NSA_KV_W), BF16) + _nbytes(v_aug.shape, BF16) + _nbytes((Q_BLOCK, NSA_Q_W), BF16)
                + _nbytes((Q_BLOCK, NSA_Q_W), F32)) + 8 * _nbytes((rows, span), F32)
    return pl.pallas_call(
        functools.partial(_win_attn_kernel, span=span),
        grid=(s // Q_BLOCK,),
        in_specs=[pl.BlockSpec((Q_BLOCK, NSA_Q_W), lambda b: (b, 0)),
                  pl.BlockSpec((s, NSA_KV_W), lambda b: (0, kw_col)),
                  pl.BlockSpec(v_aug.shape, lambda b: (0, 0, 0))],
        out_specs=pl.BlockSpec((Q_BLOCK, NSA_Q_W), lambda b: (b, 0)),
        out_shape=jax.ShapeDtypeStruct((s, NSA_Q_W), BF16),
        compiler_params=_cparams(("parallel",), vmem),
        name="nsa_window",
    )(q_rot, k_rot, v_aug)


def _dil_attn_kernel(q_ref, kp_ref, kc_ref, vp_ref, vc_ref, o_ref, lse_ref, kbuf, vbuf, *, span, dil, nb):
    n = pl.program_id(0)
    unit = dil * Q_BLOCK
    kbuf[0:unit, :] = kp_ref[...]
    kbuf[unit:, :] = kc_ref[...]
    vbuf[0:unit, :] = vp_ref[...]
    vbuf[unit:, :] = vc_ref[...]
    qi = _row_iota((Q_BLOCK, 1)) + Q_BLOCK
    ki = _lane_iota((1, 2 * Q_BLOCK))
    delta = qi - ki
    band_bias = jnp.where((delta >= 0) & (delta <= span), 0.0, MASKED)
    lo = _lane_iota((Q_BLOCK, LANES)) < HEAD_DIM

    def item(idx, carry):
        j = idx // dil
        base = j * unit + idx % dil
        if dil > 1:
            q_rows, kv_rows = pl.ds(base, Q_BLOCK, stride=dil), pl.ds(base, 2 * Q_BLOCK, stride=dil)
        else:
            base = pl.multiple_of(base, Q_BLOCK)
            q_rows, kv_rows = pl.ds(base, Q_BLOCK), pl.ds(base, 2 * Q_BLOCK)
        bias = jnp.where((n * nb + j == 0) & (ki < Q_BLOCK), MASKED, band_bias)
        q = q_ref[q_rows, :]
        k = kbuf[kv_rows, :].astype(BF16)
        v = vbuf[kv_rows, :].astype(BF16)
        outs, lses = [], []
        for e in range(2):
            qh = jnp.where(lo if e == 0 else ~lo, q, 0.0).astype(BF16)
            s = _qk(qh, k) + bias
            m = jnp.maximum(jnp.max(s, axis=-1, keepdims=True), M_INIT)
            ex = jnp.exp(s - m)
            den = jnp.maximum(jnp.sum(ex, axis=-1, keepdims=True), 1e-30)
            outs.append(jnp.dot(ex.astype(BF16), v, preferred_element_type=F32) / den)
            lses.append(jnp.broadcast_to(m + jnp.log(den), (Q_BLOCK, LANES)))
        o_ref[q_rows, :] = jnp.where(lo, outs[0], outs[1])
        lse_ref[q_rows, :] = jnp.where(lo, lses[0], lses[1])
        return carry

    lax.fori_loop(0, nb * dil, item, 0, unroll=8)


def _dil_attn(qb, kb, vb, gidx, win, dil):
    s = qb.shape[0]
    unit = dil * Q_BLOCK
    tb = min(max(DIL_STEP_TOKENS, unit), s)
    assert s % tb == 0 and tb % unit == 0
    nb = tb // unit
    span = win // dil
    n_chunks = DIL_OUT_W // LANES
    cur = lambda n, c: (n, gidx * n_chunks + c)
    prev = lambda n, c: (jnp.maximum(n * nb - 1, 0), gidx * n_chunks + c)
    blk, pblk = (tb, LANES), (unit, LANES)
    vmem = (2 * (5 * _nbytes(blk, F32) + 2 * _nbytes(pblk, F32)) + 2 * _nbytes((tb + unit, LANES), F32)
            + 32 * _nbytes((Q_BLOCK, 2 * Q_BLOCK), F32))
    return pl.pallas_call(
        functools.partial(_dil_attn_kernel, span=span, dil=dil, nb=nb),
        grid=(s // tb, n_chunks),
        in_specs=[pl.BlockSpec(blk, cur), pl.BlockSpec(pblk, prev), pl.BlockSpec(blk, cur),
                  pl.BlockSpec(pblk, prev), pl.BlockSpec(blk, cur)],
        out_specs=[pl.BlockSpec(blk, lambda n, c: (n, c)), pl.BlockSpec(blk, lambda n, c: (n, c))],
        out_shape=[jax.ShapeDtypeStruct((s, DIL_OUT_W), F32)] * 2,
        scratch_shapes=[pltpu.VMEM((tb + unit, LANES), F32)] * 2,
        compiler_params=_cparams(("parallel", "parallel"), vmem),
        name=f"dilated_w{win}_d{dil}",
    )(qb, kb, kb, vb, vb)


def _sgu_kernel(uv_ref, g_ref, b_ref, ws_ref, bs_ref, o_ref):
    uv = jax.nn.gelu(uv_ref[...])
    u = uv[:, :SGU_WIDTH]
    v = _layer_norm(uv[:, SGU_WIDTH:], g_ref[...], b_ref[...]).astype(BF16)
    causal = _lane_iota((SGU_CHUNK, SGU_CHUNK)) <= _row_iota((SGU_CHUNK, SGU_CHUNK))
    bs = bs_ref[...]
    for n in range(uv.shape[0] // SGU_CHUNK):
        rs = slice(n * SGU_CHUNK, (n + 1) * SGU_CHUNK)
        for g in range(SGU_GROUPS):
            cs = slice(g * SGU_GROUP_CH, (g + 1) * SGU_GROUP_CH)
            ws = jnp.where(causal, ws_ref[g], 0.0).astype(BF16)
            sv = jnp.dot(ws, v[rs, cs], preferred_element_type=F32) + bs[:, cs]
            o_ref[rs, cs] = (u[rs, cs] * sv).astype(o_ref.dtype)


def _sgu(uv, ln_g, ln_b, w_s, b_s):
    s = uv.shape[0]
    tm = min(4 * SGU_CHUNK, s)
    assert s % tm == 0
    bs = jnp.repeat(b_s.T, SGU_GROUP_CH, axis=1)
    c2 = lambda i: (0, 0)
    vmem = 2 * (_nbytes((tm, 2 * SGU_WIDTH), F32) + _nbytes((tm, SGU_WIDTH), F32)) + 6 * _nbytes((tm, 2 * SGU_WIDTH), F32)
    return pl.pallas_call(
        _sgu_kernel,
        grid=(s // tm,),
        in_specs=[pl.BlockSpec((tm, 2 * SGU_WIDTH), lambda i: (i, 0)), pl.BlockSpec((1, SGU_WIDTH), c2),
                  pl.BlockSpec((1, SGU_WIDTH), c2), pl.BlockSpec(w_s.shape, lambda i: (0, 0, 0)),
                  pl.BlockSpec((SGU_CHUNK, SGU_WIDTH), c2)],
        out_specs=pl.BlockSpec((tm, SGU_WIDTH), lambda i: (i, 0)),
        out_shape=jax.ShapeDtypeStruct((s, SGU_WIDTH), BF16),
        compiler_params=_cparams(("parallel",), vmem),
        name="sgu",
    )(uv, ln_g.reshape(1, -1), ln_b.reshape(1, -1), w_s, bs)


def _split_dot(x, w2):
    hi = x.astype(BF16)
    lo = (x - hi.astype(F32)).astype(BF16)
    return jnp.dot(jnp.concatenate([hi, lo], axis=1), w2, preferred_element_type=F32)


def _mix_kernel(x_ref, oc_ref, os_ref, ow_ref, ga_ref, eg_ref, d0_ref, d1_ref, d2_ref, l0_ref, l1_ref, l2_ref,
                sg_ref, gm_ref, wa_ref, wb_ref, wc_ref, wo_ref, g_ref, b_ref, o_ref, *, alpha):
    d = x_ref.shape[1]
    gates = _split_dot(jax.nn.sigmoid(ga_ref[...]), eg_ref[...])
    o_a = (gates[:, 0:NSA_Q_W] * oc_ref[...].astype(F32) + gates[:, NSA_Q_W:2 * NSA_Q_W] * os_ref[...].astype(F32)
           + gates[:, 2 * NSA_Q_W:3 * NSA_Q_W] * ow_ref[...].astype(F32))
    y_a = jnp.dot(o_a.astype(BF16), wa_ref[...], preferred_element_type=F32)
    l0, l1, l2 = l0_ref[...], l1_ref[...], l2_ref[...]
    lm = jnp.maximum(jnp.maximum(l0, l1), l2)
    e0, e1, e2 = jnp.exp(l0 - lm), jnp.exp(l1 - lm), jnp.exp(l2 - lm)
    den = e0 + e1 + e2
    o_b = (e0 / den) * d0_ref[...] + (e1 / den) * d1_ref[...] + (e2 / den) * d2_ref[...]
    y_b = jnp.dot(o_b.astype(BF16), wb_ref[...], preferred_element_type=F32)
    y_c = jnp.dot(sg_ref[...].astype(BF16), wc_ref[...], preferred_element_type=F32)
    gm = jax.nn.sigmoid(gm_ref[...].astype(F32))
    merged = gm[:, 0:d] * y_a + gm[:, d:2 * d] * y_b + gm[:, 2 * d:3 * d] * y_c
    mix = jnp.dot(merged.astype(BF16), wo_ref[...], preferred_element_type=F32)
    o_ref[...] = _layer_norm(alpha * x_ref[...] + mix, g_ref[...], b_ref[...])


def _mix(x, o_c, o_s, o_w, g_a, dil_o, dil_lse, sgu, g_m, wa, wb, wc, wo, g, b, alpha):
    s, d = x.shape
    tm = min(256, s)
    assert s % tm == 0
    col = np.arange(N_BRANCH * NSA_Q_W)
    head, br = (col % NSA_Q_W) // HEAD_DIM, col // NSA_Q_W
    eg1 = (np.arange(LANES)[:, None] == (head * N_BRANCH + br)[None, :]).astype(np.float32)
    eg = jnp.asarray(np.concatenate([eg1, eg1], axis=0), BF16)
    row = lambda i: (i, 0)
    const = lambda i: (0, 0)
    rspec = lambda w: pl.BlockSpec((tm, w), row)
    cspec = lambda a: pl.BlockSpec(a.shape, const)
    widths = [d, NSA_Q_W, NSA_Q_W, NSA_Q_W, LANES] + [DIL_OUT_W] * 6 + [SGU_WIDTH, N_BRANCH * d]
    vmem = (2 * sum(_nbytes((tm, w), F32) for w in widths) + 2 * _nbytes((tm, d), F32)
            + 2 * sum(_nbytes(a.shape, BF16) for a in (eg, wa, wb, wc, wo)) + 10 * _nbytes((tm, N_BRANCH * d), F32))
    return pl.pallas_call(
        functools.partial(_mix_kernel, alpha=alpha),
        grid=(s // tm,),
        in_specs=[rspec(d), rspec(NSA_Q_W), rspec(NSA_Q_W), rspec(NSA_Q_W), rspec(LANES), cspec(eg)]
                 + [rspec(DIL_OUT_W)] * 6 + [rspec(SGU_WIDTH), rspec(N_BRANCH * d)]
                 + [cspec(wa), cspec(wb), cspec(wc), cspec(wo), pl.BlockSpec((1, d), const), pl.BlockSpec((1, d), const)],
        out_specs=rspec(d),
        out_shape=jax.ShapeDtypeStruct((s, d), F32),
        compiler_params=_cparams(("parallel",), vmem),
        name="mixer_merge_deepnorm",
    )(x, o_c, o_s, o_w, g_a, eg, *dil_o, *dil_lse, sgu, g_m, wa, wb, wc, wo, g.reshape(1, d), b.reshape(1, d))


def _rope_tables(s):
    inv = 1.0 / (ROPE_THETA ** (np.arange(0, HEAD_DIM, 2, dtype=np.float32) / HEAD_DIM)).astype(np.float32)
    ang = (np.arange(s, dtype=np.float32)[:, None] * inv[None, :]).astype(np.float64)
    cos, sin = jnp.asarray(np.cos(ang), F32), jnp.asarray(np.sin(ang), F32)
    reps = LANES // HEAD_DIM
    return jnp.tile(jnp.concatenate([cos, cos], axis=1), (1, reps)), jnp.tile(jnp.concatenate([-sin, sin], axis=1), (1, reps))


def kernel(x, ln_g, ln_b, ffn1_gate, ffn1_up, ffn1_down, ffn2_gate, ffn2_up, ffn2_down, w_in, phi_k_pos, phi_k_w1, phi_k_w2, phi_v_pos, phi_v_w1, phi_v_w2, sgu_ln_g, sgu_ln_b, sgu_w, sgu_b, w_branch_a, w_branch_b, w_branch_c, w_out):
    bsz, s, d = x.shape
    depth = ln_g.shape[0]
    alpha = float((2 * depth) ** 0.25)
    q_scale = float(HEAD_DIM ** -0.5)
    tabs = _rope_tables(s)
    sizes = [NSA_Q_W] + [NSA_KV_W] * 6 + [N_BRANCH * NSA_HEADS, DIL_W, DIL_W, DIL_W, 2 * SGU_WIDTH, N_BRANCH * d]
    offs = np.concatenate([[0], np.cumsum(sizes)]).tolist()
    outs = []
    for bi in range(bsz):
        h = x[bi]
        for l in range(depth):
            (w_qa, w_kc, w_vc, w_ks, w_vs, w_kw, w_vw, w_ga, w_qb, w_kb, w_vb, w_uv, w_gm) = [
                w_in[l][:, offs[i]:offs[i + 1]] for i in range(len(sizes))]
            w_ga = jnp.pad(w_ga, ((0, 0), (0, LANES - w_ga.shape[1])))
            groups = dict(qb=w_qb, kb=w_kb, vb=w_vb, k_rot=jnp.concatenate([w_ks, w_kw], axis=1),
                          kv_c=jnp.concatenate([w_kc, w_vc], axis=1), v_plain=jnp.concatenate([w_vs, w_vw], axis=1),
                          uv=w_uv, gm=w_gm, qa=w_qa, ga=w_ga)
            wp = jnp.concatenate(list(groups.values()), axis=1).astype(BF16)
            ends = np.cumsum([g.shape[1] for g in groups.values()]).tolist()
            cols = {name: (e - g.shape[1], e) for (name, g), e in zip(groups.items(), ends)}

            h, hb = _ffn(h, ffn1_gate[l].astype(BF16), ffn1_up[l].astype(BF16), ffn1_down[l].astype(BF16),
                         ln_g[l, 0], ln_b[l, 0], alpha)

            q_rot, q_u = _mm(hb, wp, cols["qa"], BF16, rope_tabs=tabs, scale=q_scale * LOG2E, plain_scale=q_scale * LOG2E)
            k_rot = _mm(hb, wp, cols["k_rot"], BF16, rope_tabs=tabs)
            qb = _mm(hb, wp, cols["qb"], F32, rope_tabs=tabs, scale=q_scale)
            kb = _mm(hb, wp, cols["kb"], F32, rope_tabs=tabs)
            vb = _mm(hb, wp, cols["vb"], F32)
            kv_c = _mm(hb, wp, cols["kv_c"], F32)
            v_plain = _mm(hb, wp, cols["v_plain"], BF16)
            uv = _mm(hb, wp, cols["uv"], F32)
            g_m = _mm(hb, wp, cols["gm"], BF16)
            g_a = _mm(hb, wp, cols["ga"], F32)

            kc = _compress(kv_c, 0, phi_k_pos[l], phi_k_w1[l], phi_k_w2[l])
            vc = _compress(kv_c, 1, phi_v_pos[l], phi_v_w1[l], phi_v_w2[l])
            o_c, nsel = _cmp_select(q_u, kc, vc)
            o_s = _sel_attn(q_rot, k_rot[:, :NSA_KV_W], v_plain[:, :NSA_KV_W], nsel)
            o_w = _win_attn(q_rot, k_rot, v_plain[:, NSA_KV_W:], 1)
            dil = [_dil_attn(qb, kb, vb, gi, win, dl) for gi, (win, dl) in enumerate(DIL_PAIRS)]
            sg = _sgu(uv, sgu_ln_g[l], sgu_ln_b[l], sgu_w[l], sgu_b[l])
            h = _mix(h, o_c, o_s, o_w, g_a, [o for o, _ in dil], [e for _, e in dil], sg, g_m,
                     w_branch_a[l].astype(BF16), w_branch_b[l].astype(BF16), w_branch_c[l].astype(BF16),
                     w_out[l].astype(BF16), ln_g[l, 1], ln_b[l, 1], alpha)

            h, _ = _ffn(h, ffn2_gate[l].astype(BF16), ffn2_up[l].astype(BF16), ffn2_down[l].astype(BF16),
                        ln_g[l, 2], ln_b[l, 2], alpha)
        outs.append(h)
    return jnp.stack(outs, axis=0)
```

```python
import functools

import numpy as np
import jax
import jax.numpy as jnp
from jax import lax
from jax.experimental import pallas as pl
from jax.experimental.pallas import tpu as pltpu

F32 = jnp.float32
BF16 = jnp.bfloat16

HEAD_DIM = 64
ROPE_THETA = 10000.0
LN_EPS = 1e-5
Q_BLOCK = 128
NSA_HEADS = 8
NSA_GROUPS = 2
NSA_HPG = NSA_HEADS // NSA_GROUPS
NSA_Q_W = NSA_HEADS * HEAD_DIM
NSA_KV_W = NSA_GROUPS * HEAD_DIM
CMP_LEN = 32
CMP_STRIDE = 16
SLC_BLOCK = 64
SLC_RATIO = SLC_BLOCK // CMP_STRIDE
SLC_TOPK = 16
MM_ROWS = 2048
MM_SUB_ROWS = 512
FFN_ROWS = 1024
CMP_Q_BLOCK = 256
DIL_STEP_TOKENS = 2048
SEL_TILE = 1024
SEL_VT_ROWS = 80
NSA_WINDOW = 512
FORCE_SCORE = 1e6
DIL_PAIRS = ((128, 1), (512, 4), (2048, 16))
DIL_HPG = 4
DIL_HEADS = DIL_HPG * 3
DIL_W = DIL_HEADS * HEAD_DIM
DIL_OUT_W = DIL_HPG * HEAD_DIM
SGU_CHUNK = 128
SGU_GROUPS = 4
SGU_GROUP_CH = 128
SGU_WIDTH = SGU_GROUPS * SGU_GROUP_CH
N_BRANCH = 3

LANES = 128
V7X_VMEM_BYTES = 64 * 1024 * 1024
VMEM_CAP = 56 * 1024 * 1024

M_INIT = -1e30
MASKED = -(2.0 ** 101)
LOG2E = 1.4426950408889634


def _cparams(sem, vmem_bytes):
    return pltpu.CompilerParams(dimension_semantics=sem,
                                vmem_limit_bytes=int(min(max(vmem_bytes, 16 * 1024 * 1024), VMEM_CAP)))


def _nbytes(shape, dtype):
    return int(np.prod(shape)) * jnp.dtype(dtype).itemsize


def _layer_norm(y, g, b):
    mu = jnp.mean(y, axis=-1, keepdims=True)
    d = y - mu
    var = jnp.mean(d * d, axis=-1, keepdims=True)
    return d * lax.rsqrt(var + LN_EPS) * g + b


def _lane_iota(shape):
    return lax.broadcasted_iota(jnp.int32, shape, len(shape) - 1)


def _row_iota(shape):
    return lax.broadcasted_iota(jnp.int32, shape, 0)


def _ffn_kernel(x_ref, wg_ref, wu_ref, wd_ref, g_ref, b_ref, o_ref, ob_ref, *, alpha, fc):
    n_f = wg_ref.shape[1]
    for r0 in range(0, x_ref.shape[0], MM_SUB_ROWS):
        rs = slice(r0, r0 + MM_SUB_ROWS)
        x = x_ref[rs, :]
        xb = x.astype(BF16)
        acc = jnp.zeros(x.shape, F32)
        for c in range(n_f // fc):
            sl = slice(c * fc, (c + 1) * fc)
            gate = jnp.dot(xb, wg_ref[:, sl], preferred_element_type=F32)
            up = jnp.dot(xb, wu_ref[:, sl], preferred_element_type=F32)
            h = (gate * jax.nn.sigmoid(gate)) * up
            acc = acc + jnp.dot(h.astype(BF16), wd_ref[sl, :], preferred_element_type=F32)
        out = _layer_norm(alpha * x + 0.5 * acc, g_ref[...], b_ref[...])
        o_ref[rs, :] = out
        ob_ref[rs, :] = out.astype(BF16)


def _ffn(x, wg, wu, wd, g, b, alpha):
    s, d = x.shape
    n_f = wg.shape[1]
    tm = min(FFN_ROWS, s)
    fc = 256
    assert s % tm == 0 and tm % MM_SUB_ROWS == 0 and n_f % fc == 0
    const = lambda i: (0, 0)
    row = lambda i: (i, 0)
    once = pl.Buffered(1)
    vmem = (2 * 2 * _nbytes((tm, d), F32) + 2 * _nbytes((tm, d), BF16)
            + 3 * _nbytes((d, n_f), BF16) + 8 * _nbytes((MM_SUB_ROWS, d), F32))
    return pl.pallas_call(
        functools.partial(_ffn_kernel, alpha=alpha, fc=fc),
        grid=(s // tm,),
        in_specs=[pl.BlockSpec((tm, d), row), pl.BlockSpec((d, n_f), const, pipeline_mode=once),
                  pl.BlockSpec((d, n_f), const, pipeline_mode=once),
                  pl.BlockSpec((n_f, d), const, pipeline_mode=once),
                  pl.BlockSpec((1, d), const), pl.BlockSpec((1, d), const)],
        out_specs=[pl.BlockSpec((tm, d), row), pl.BlockSpec((tm, d), row)],
        out_shape=[jax.ShapeDtypeStruct((s, d), F32), jax.ShapeDtypeStruct((s, d), BF16)],
        compiler_params=_cparams(("parallel",), vmem),
        name="ffn_deepnorm",
    )(x, wg, wu, wd, g.reshape(1, d), b.reshape(1, d))


def _swap_halves_64(a):
    half = HEAD_DIM // 2
    first = (_lane_iota(a.shape) % HEAD_DIM) < half
    return jnp.where(first, pltpu.roll(a, LANES - half, 1), pltpu.roll(a, half, 1))


def _mm_kernel(*refs, rope, scale, plain_scale):
    if rope:
        x_ref, w_ref, c_ref, s_ref, o_ref = refs[:5]
    else:
        x_ref, w_ref, o_ref = refs
    for r0 in range(0, x_ref.shape[0], MM_SUB_ROWS):
        rs = slice(r0, r0 + MM_SUB_ROWS)
        acc = jnp.dot(x_ref[rs, :], w_ref[...], preferred_element_type=F32)
        if plain_scale is not None:
            refs[5][rs, :] = (acc * plain_scale).astype(refs[5].dtype)
        if rope:
            cos = c_ref[rs, :]
            sin = s_ref[rs, :]
            for c in range(acc.shape[1] // LANES):
                a = acc[:, c * LANES:(c + 1) * LANES]
                r = a * cos + _swap_halves_64(a) * sin
                if scale != 1.0:
                    r = r * scale
                o_ref[rs, c * LANES:(c + 1) * LANES] = r.astype(o_ref.dtype)
        else:
            if scale != 1.0:
                acc = acc * scale
            o_ref[rs, :] = acc.astype(o_ref.dtype)


def _mm(xb, w, cols, out_dtype, *, rope_tabs=None, scale=1.0, plain_scale=None):
    s, k = xb.shape
    col0, n = cols[0], cols[1] - cols[0]
    tm = min(MM_ROWS, s)
    tn = next(c for c in (1024, 768, 512, 256, LANES) if n % c == 0 and col0 % c == 0)
    assert s % tm == 0 and tm % MM_SUB_ROWS == 0 and (plain_scale is None or rope_tabs is not None)
    jb = col0 // tn
    n_out = 1 if plain_scale is None else 2
    in_specs = [pl.BlockSpec((tm, k), lambda i, j: (i, 0)), pl.BlockSpec((k, tn), lambda i, j: (0, jb + j))]
    args = [xb, w]
    if rope_tabs is not None:
        in_specs += [pl.BlockSpec((tm, LANES), lambda i, j: (i, 0))] * 2
        args += list(rope_tabs)
    vmem = 2 * (_nbytes((tm, k), BF16) + _nbytes((k, tn), BF16) + n_out * _nbytes((tm, tn), out_dtype)
                + 2 * _nbytes((tm, LANES), F32)) + 6 * _nbytes((MM_SUB_ROWS, tn), F32)
    out = pl.pallas_call(
        functools.partial(_mm_kernel, rope=rope_tabs is not None, scale=scale, plain_scale=plain_scale),
        grid=(s // tm, n // tn),
        in_specs=in_specs,
        out_specs=[pl.BlockSpec((tm, tn), lambda i, j: (i, j))] * n_out,
        out_shape=[jax.ShapeDtypeStruct((s, n), out_dtype)] * n_out,
        compiler_params=_cparams(("parallel", "parallel"), vmem),
        name="proj_rope" if rope_tabs is not None else "proj",
    )(*args)
    return out[0] if n_out == 1 else out


def _compress_kernel(x_ref, pos_ref, w1_ref, w2_ref, o_ref):
    n_cmp = o_ref.shape[0]
    hidden = w1_ref.shape[3]
    ha = [jnp.zeros((n_cmp, hidden), F32) for _ in range(NSA_GROUPS)]
    hb = [jnp.zeros((n_cmp, hidden), F32) for _ in range(NSA_GROUPS)]
    tail = [jnp.zeros((8, hidden), F32) for _ in range(NSA_GROUPS)]
    for j in range(CMP_STRIDE):
        xj = x_ref[pl.ds(j, n_cmp, stride=CMP_STRIDE), :]
        pa = pos_ref[j:j + 1, :]
        pb = pos_ref[CMP_STRIDE + j:CMP_STRIDE + j + 1, :]
        xa = (xj + pa).astype(BF16)
        xb = (xj + pb).astype(BF16)
        pb8 = jnp.broadcast_to(pb, (8, pb.shape[1])).astype(BF16)
        for g in range(NSA_GROUPS):
            ha[g] = ha[g] + jnp.dot(xa, w1_ref[g, j], preferred_element_type=F32)
            hb[g] = hb[g] + jnp.dot(xb, w1_ref[g, CMP_STRIDE + j], preferred_element_type=F32)
            tail[g] = tail[g] + jnp.dot(pb8, w1_ref[g, CMP_STRIDE + j], preferred_element_type=F32)
    last = _row_iota((n_cmp, 1)) == n_cmp - 1
    outs = []
    for g in range(NSA_GROUPS):
        hb_next = jnp.where(last, tail[g][0:1, :], pltpu.roll(hb[g], n_cmp - 1, 0))
        h = jax.nn.gelu(ha[g] + hb_next)
        outs.append(jnp.dot(h.astype(BF16), w2_ref[...], preferred_element_type=F32))
    lo = _lane_iota(outs[0].shape) < HEAD_DIM
    o_ref[...] = jnp.where(lo, outs[0], outs[1]).astype(o_ref.dtype)


def _compress(x, col, pos, w1, w2):
    s = x.shape[0]
    n_cmp = s // CMP_STRIDE
    hidden = w1.shape[1]
    pos2 = jnp.concatenate([pos] * NSA_GROUPS, axis=1)
    w1r = w1.reshape(CMP_LEN, HEAD_DIM, hidden).astype(BF16)
    zero = jnp.zeros_like(w1r)
    w1g = jnp.stack([jnp.concatenate([w1r if gg == g else zero for gg in range(NSA_GROUPS)], axis=1)
                     for g in range(NSA_GROUPS)])
    w2d = jnp.concatenate([w2, w2], axis=1).astype(BF16)
    vmem = 2 * (_nbytes((s, NSA_KV_W), F32) + _nbytes(w1g.shape, BF16)) + 16 * _nbytes((n_cmp, hidden), F32)
    return pl.pallas_call(
        _compress_kernel,
        grid=(1,),
        in_specs=[pl.BlockSpec((s, NSA_KV_W), lambda i: (0, col)), pl.BlockSpec(pos2.shape, lambda i: (0, 0)),
                  pl.BlockSpec(w1g.shape, lambda i: (0, 0, 0, 0)), pl.BlockSpec(w2d.shape, lambda i: (0, 0))],
        out_specs=pl.BlockSpec((n_cmp, NSA_KV_W), lambda i: (0, 0)),
        out_shape=jax.ShapeDtypeStruct((n_cmp, NSA_KV_W), BF16),
        compiler_params=_cparams(("arbitrary",), vmem),
        name="nsa_compress",
    )(x, pos2, w1g, w2d)


def _stack_group_queries(q_ref, col0, kv_lane_group):
    parts = []
    for h in range(NSA_HPG):
        c = col0 + (h // 2) * LANES
        qc = q_ref[:, c:c + LANES].astype(F32)
        e = h % 2
        lane = _lane_iota(qc.shape)
        mine = (lane >= e * HEAD_DIM) & (lane < (e + 1) * HEAD_DIM)
        qh = jnp.where(mine, qc, 0.0)
        if e != kv_lane_group:
            qh = pltpu.roll(qh, HEAD_DIM, 1)
        parts.append(qh.astype(BF16))
    return jnp.concatenate(parts, axis=0)


def _unstack_group_outputs(o, kv_lane_group):
    chunks = []
    tq = o.shape[0] // NSA_HPG
    for c in range(NSA_HPG // 2):
        halves = []
        for e in range(2):
            h = 2 * c + e
            oh = o[h * tq:(h + 1) * tq, :]
            if e != kv_lane_group:
                oh = pltpu.roll(oh, HEAD_DIM, 1)
            halves.append(oh)
        lo = _lane_iota(halves[0].shape) < HEAD_DIM
        chunks.append(jnp.where(lo, halves[0], halves[1]))
    return jnp.concatenate(chunks, axis=1)


def _qk(q, k):
    return lax.dot_general(q, k, (((1,), (1,)), ((), ())), preferred_element_type=F32)


def _cmp_select_kernel(q_ref, kc_ref, vc_ref, mmap_ref, oc_ref, nsel_ref, *, top_k, n_cls):
    b = pl.program_id(0)
    n_cmp = kc_ref.shape[0]
    n_slc = mmap_ref.shape[0]
    tq = q_ref.shape[0]
    rows = NSA_HPG * tq
    gw = NSA_HPG * HEAD_DIM
    t_row = b * tq + _row_iota((tq, 1))
    t = b * tq + _lane_iota((1, tq))

    def variant(nc, nb):
        valid = _lane_iota((1, nc)) * CMP_STRIDE + (CMP_LEN - 1) <= t_row
        bias = jnp.concatenate([jnp.where(valid, 0.0, MASKED)] * NSA_HPG, axis=0)
        blk = _row_iota((nb, 1))
        cur = t // SLC_BLOCK
        forced = (blk == 0) | (blk == cur) | (blk == cur - 1)
        future = blk * SLC_BLOCK > t
        blk_f = blk.astype(F32)
        kc = kc_ref[0:nc, :]
        vc = vc_ref[0:nc, :]
        mmap_t = mmap_ref[0:nb, 0:nc]
        for g in range(NSA_GROUPS):
            qst = _stack_group_queries(q_ref, g * gw, g)
            s = _qk(qst, kc) + bias
            m = jnp.maximum(jnp.max(s, axis=-1, keepdims=True), M_INIT)
            e = jnp.exp2(s - m)
            den = jnp.sum(e, axis=-1, keepdims=True)
            p = e / jnp.maximum(den, 1e-30)
            o = jnp.dot(p.astype(BF16), vc, preferred_element_type=F32)
            oc_ref[:, g * gw:(g + 1) * gw] = _unstack_group_outputs(o, g).astype(oc_ref.dtype)
            psum = p[0:tq]
            for h in range(1, NSA_HPG):
                psum = psum + p[h * tq:(h + 1) * tq]
            hi = psum.astype(BF16)
            r1 = psum - hi.astype(F32)
            mid = r1.astype(BF16)
            low = (r1 - mid.astype(F32)).astype(BF16)
            p_slc = _qk(mmap_t, hi) + _qk(mmap_t, mid) + _qk(mmap_t, low)
            score = jnp.where(forced, FORCE_SCORE, jnp.where(future, -1.0, p_slc))
            for _ in range(top_k):
                mx = jnp.max(score, axis=0, keepdims=True)
                first = jnp.min(jnp.where(score == mx, blk_f, float(nb)), axis=0, keepdims=True)
                score = jnp.where(blk_f == first, -2.0, score)
            not_picked = jnp.where(score == -2.0, 0.0, 1.0)
            nsel_ref[:, g * n_slc:g * n_slc + nb] = not_picked.T.astype(nsel_ref.dtype)
            if nb < n_slc:
                nsel_ref[:, g * n_slc + nb:(g + 1) * n_slc] = jnp.ones((tq, n_slc - nb), nsel_ref.dtype)

    for c in range(n_cls):
        nc = (c + 1) * (n_cmp // n_cls)
        nb = min(n_slc, -(-(nc // SLC_RATIO) // LANES) * LANES)
        lo_b = c * (nc // (c + 1)) // (tq // CMP_STRIDE)
        hi_b = nc // (tq // CMP_STRIDE)
        pl.when((b >= lo_b) & (b < hi_b))(functools.partial(variant, nc, nb))


def _cmp_select(q_u, kc, vc):
    s = q_u.shape[0]
    n_cmp, n_slc = s // CMP_STRIDE, s // SLC_BLOCK
    top_k = min(SLC_TOPK, n_slc)
    n_cls = max(n_cmp // (2 * LANES), 1)
    tq = min(CMP_Q_BLOCK, s)
    assert s % tq == 0 and n_cmp % n_cls == 0 and (n_cmp // n_cls) % (tq // CMP_STRIDE) == 0
    m = np.arange(n_cmp)[:, None]
    j = np.arange(n_slc)[None, :]
    mmap = jnp.asarray(((m >= SLC_RATIO * j - 1) & (m <= SLC_RATIO * j + SLC_RATIO - 1)).astype(np.float32).T, BF16)
    const = lambda b: (0, 0)
    rows = NSA_HPG * tq
    vmem = 2 * (2 * _nbytes((n_cmp, NSA_KV_W), BF16) + _nbytes((n_cmp, n_slc), BF16)
                + _nbytes((tq, NSA_Q_W), F32) * 2) + 6 * _nbytes((rows, n_cmp), F32)
    return pl.pallas_call(
        functools.partial(_cmp_select_kernel, top_k=top_k, n_cls=n_cls),
        grid=(s // tq,),
        in_specs=[pl.BlockSpec((tq, NSA_Q_W), lambda b: (b, 0)), pl.BlockSpec((n_cmp, NSA_KV_W), const),
                  pl.BlockSpec((n_cmp, NSA_KV_W), const), pl.BlockSpec((n_slc, n_cmp), const)],
        out_specs=[pl.BlockSpec((tq, NSA_Q_W), lambda b: (b, 0)),
                   pl.BlockSpec((tq, NSA_GROUPS * n_slc), lambda b: (b, 0))],
        out_shape=[jax.ShapeDtypeStruct((s, NSA_Q_W), BF16), jax.ShapeDtypeStruct((s, NSA_GROUPS * n_slc), BF16)],
        compiler_params=_cparams(("parallel",), vmem),
        name="nsa_compressed_select",
    )(q_u, kc, vc, mmap)


def _sel_attn_kernel(q_ref, k_ref, vt_ref, nsel_ref, kw_ref, vwt_ref, o_ref, ow_ref, qa_ref, m_ref, acc_ref, s_ref,
                     *, win, w_span):
    b = pl.program_id(0)
    n_slc = nsel_ref.shape[1] // NSA_GROUPS
    n_win = n_slc // win
    tiles_per_win = win * SLC_BLOCK // SEL_TILE
    gw = NSA_HPG * HEAD_DIM
    last = (b * Q_BLOCK) // SEL_TILE
    t_lane = b * Q_BLOCK + _lane_iota((1, Q_BLOCK))
    lane = _lane_iota((Q_BLOCK, LANES))
    head_lanes = lane < HEAD_DIM
    flag_lanes = (lane >= HEAD_DIM) & (lane < HEAD_DIM + win)

    nsel_all = nsel_ref[...].astype(F32)
    if nsel_all.shape[1] < LANES:
        nsel_all = jnp.concatenate([nsel_all, jnp.zeros((Q_BLOCK, LANES - nsel_all.shape[1]), F32)], axis=1)
    for g in range(NSA_GROUPS):
        heads = []
        for h in range(NSA_HPG):
            c = g * gw + (h // 2) * LANES
            qc = q_ref[:, c:c + LANES].astype(F32)
            heads.append(qc if h % 2 == 0 else pltpu.roll(qc, HEAD_DIM, 1))
        for w in range(n_win):
            a = g * n_slc + w * win
            fl = nsel_all[:, (a // LANES) * LANES:(a // LANES + 1) * LANES]
            shift = (HEAD_DIM - a % LANES) % LANES
            if shift:
                fl = pltpu.roll(fl, shift, 1)
            fl = jnp.where(flag_lanes, fl, 0.0)
            for h in range(NSA_HPG):
                qa_ref[g, w, h * Q_BLOCK:(h + 1) * Q_BLOCK, :] = jnp.where(head_lanes, heads[h], fl).astype(BF16)
    m_ref[...] = jnp.full(m_ref.shape, M_INIT, F32)
    acc_ref[...] = jnp.zeros(acc_ref.shape, F32)

    def heads_to_lanes(o_t, out_ref, g):
        o_t = jnp.concatenate([o_t, jnp.zeros((LANES - o_t.shape[0], o_t.shape[1]), F32)], axis=0)
        heads = [o_t[:, h * Q_BLOCK:(h + 1) * Q_BLOCK].T for h in range(NSA_HPG)]
        for c in range(NSA_HPG // 2):
            even, odd = heads[2 * c], pltpu.roll(heads[2 * c + 1], HEAD_DIM, 1)
            out_ref[:, g * gw + c * LANES:g * gw + (c + 1) * LANES] = jnp.where(head_lanes, even, odd).astype(out_ref.dtype)

    n_back = NSA_WINDOW // Q_BLOCK
    blk0 = jnp.maximum(b - n_back, 0)
    start = pl.multiple_of(blk0 * Q_BLOCK, Q_BLOCK)
    diff = t_lane - (start + _row_iota((w_span, 1)))
    w_bias = jnp.where((diff >= 0) & (diff < NSA_WINDOW), 0.0, MASKED)
    w_bias = jnp.concatenate([w_bias] * NSA_HPG, axis=1)
    for g in range(NSA_GROUPS):
        s = _qk(kw_ref[g, pl.ds(start, w_span), :], qa_ref[g, 0]) + w_bias
        m = jnp.maximum(jnp.max(s, axis=0, keepdims=True), M_INIT)
        p = jnp.exp2(s - m).astype(BF16)
        vt = jnp.concatenate([vwt_ref[g, blk0 + i] for i in range(w_span // Q_BLOCK)], axis=1)
        acc = jnp.dot(vt, p, preferred_element_type=F32)
        heads_to_lanes(acc / jnp.maximum(acc[HEAD_DIM:HEAD_DIM + 1, :], 1e-30), ow_ref, g)

    def scores(kt, slot):
        off = pl.multiple_of(kt * SEL_TILE, SEL_TILE)
        for g in range(NSA_GROUPS):
            s_ref[g, slot] = _qk(k_ref[g, pl.ds(off, SEL_TILE), :], qa_ref[g, kt // tiles_per_win])

    def update(kt, slot, causal):
        off = pl.multiple_of(kt * SEL_TILE, SEL_TILE)
        for g in range(NSA_GROUPS):
            s = s_ref[g, slot]
            if causal:
                keep = off + _row_iota((SEL_TILE, 1)) <= t_lane
                s = jnp.where(jnp.concatenate([keep] * NSA_HPG, axis=1), s, MASKED)
            m_old = m_ref[g]
            m_new = jnp.maximum(m_old, jnp.max(s, axis=0, keepdims=True))
            alpha = jnp.exp2(m_old - m_new)
            p = jnp.exp2(s - m_new[0:1, :]).astype(BF16)
            pv = jnp.dot(vt_ref[g, kt], p, preferred_element_type=F32)
            acc_ref[g] = alpha[0:1, :] * acc_ref[g] + pv
            m_ref[g] = m_new

    def finish():
        for g in range(NSA_GROUPS):
            acc = acc_ref[g]
            heads_to_lanes(acc / jnp.maximum(acc[HEAD_DIM:HEAD_DIM + 1, :], 1e-30), o_ref, g)

    scores(0, 0)

    def pair(j, carry):
        scores(2 * j + 1, 1)
        update(2 * j, 0, False)
        scores(2 * j + 2, 0)
        update(2 * j + 1, 1, False)
        return carry

    lax.fori_loop(0, last // 2, pair, 0)

    @pl.when(last % 2 == 1)
    def _():
        scores(last, 1)
        update(last - 1, 0, False)
        update(last, 1, True)
        finish()

    @pl.when(last % 2 == 0)
    def _():
        update(last, 0, True)
        finish()


def _sel_attn(rot_a, k_s, v_s, nsel, k_w, v_w):
    s = rot_a.shape[0]
    n_slc = s // SLC_BLOCK
    win = min(n_slc, LANES - HEAD_DIM)
    assert s % SEL_TILE == 0 and n_slc % win == 0 and (win * SLC_BLOCK) % SEL_TILE == 0
    n_win = n_slc // win
    spare = LANES - HEAD_DIM
    w_span = min(NSA_WINDOW + Q_BLOCK, s)
    blk = (np.arange(s) // SLC_BLOCK) % win
    marks = jnp.asarray(np.where(np.arange(spare)[None, :] == blk[:, None], MASKED, 0.0).astype(np.float32), BF16)
    grp = lambda a, g: a[:, g * HEAD_DIM:(g + 1) * HEAD_DIM]
    k_aug = jnp.stack([jnp.concatenate([grp(k_s, g), marks], axis=1) for g in range(NSA_GROUPS)])
    kw_aug = jnp.stack([jnp.concatenate([grp(k_w, g), jnp.zeros((s, spare), BF16)], axis=1) for g in range(NSA_GROUPS)])

    def transposed_tiles(v, tile):
        vt = _ones_augmented_values(v).reshape(NSA_GROUPS, s // tile, tile, LANES).transpose(0, 1, 3, 2)
        return vt[:, :, :SEL_VT_ROWS, :]

    vt_aug = transposed_tiles(v_s, SEL_TILE)
    vwt_aug = transposed_tiles(v_w, Q_BLOCK)
    rows = NSA_HPG * Q_BLOCK
    once = pl.Buffered(1)
    vmem = (2 * _nbytes(k_aug.shape, BF16) + _nbytes(vt_aug.shape, BF16) + _nbytes(vwt_aug.shape, BF16)
            + 2 * (2 * _nbytes((Q_BLOCK, NSA_Q_W), BF16) + _nbytes((Q_BLOCK, NSA_GROUPS * n_slc), BF16)
                   + _nbytes((Q_BLOCK, NSA_Q_W), BF16))
            + NSA_GROUPS * (n_win * _nbytes((rows, LANES), BF16) + _nbytes((8 + SEL_VT_ROWS, rows), F32)
                            + 2 * _nbytes((SEL_TILE, rows), F32))
            + 4 * _nbytes((SEL_TILE, rows), F32))
    return pl.pallas_call(
        functools.partial(_sel_attn_kernel, win=win, w_span=w_span),
        grid=(s // Q_BLOCK,),
        in_specs=[pl.BlockSpec((Q_BLOCK, NSA_Q_W), lambda b: (b, 0)),
                  pl.BlockSpec(k_aug.shape, lambda b: (0, 0, 0), pipeline_mode=once),
                  pl.BlockSpec(vt_aug.shape, lambda b: (0, 0, 0, 0), pipeline_mode=once),
                  pl.BlockSpec((Q_BLOCK, NSA_GROUPS * n_slc), lambda b: (b, 0)),
                  pl.BlockSpec(kw_aug.shape, lambda b: (0, 0, 0), pipeline_mode=once),
                  pl.BlockSpec(vwt_aug.shape, lambda b: (0, 0, 0, 0), pipeline_mode=once)],
        out_specs=[pl.BlockSpec((Q_BLOCK, NSA_Q_W), lambda b: (b, 0))] * 2,
        out_shape=[jax.ShapeDtypeStruct((s, NSA_Q_W), BF16)] * 2,
        scratch_shapes=[pltpu.VMEM((NSA_GROUPS, n_win, rows, LANES), BF16),
                        pltpu.VMEM((NSA_GROUPS, 8, rows), F32), pltpu.VMEM((NSA_GROUPS, SEL_VT_ROWS, rows), F32),
                        pltpu.VMEM((NSA_GROUPS, 2, SEL_TILE, rows), F32)],
        compiler_params=_cparams(("parallel",), vmem),
        name="nsa_selected_window",
    )(rot_a, k_aug, vt_aug, nsel, kw_aug, vwt_aug)


def _ones_augmented_values(v):
    s = v.shape[0]
    spare = LANES - HEAD_DIM
    ones = jnp.asarray((np.arange(spare)[None, :] == 0).astype(np.float32) * np.ones((s, 1), np.float32), BF16)
    return jnp.stack([jnp.concatenate([v[:, g * HEAD_DIM:(g + 1) * HEAD_DIM], ones], axis=1)
                      for g in range(NSA_GROUPS)])


def _dil_attn_kernel(q_ref, kp_ref, kc_ref, vp_ref, vc_ref, o_ref, lse_ref, kbuf, vbuf, *, span, dil, nb):
    n = pl.program_id(0)
    unit = dil * Q_BLOCK
    kbuf[0:unit, :] = kp_ref[...]
    kbuf[unit:, :] = kc_ref[...]
    vbuf[0:unit, :] = vp_ref[...]
    vbuf[unit:, :] = vc_ref[...]
    qi = _row_iota((Q_BLOCK, 1)) + Q_BLOCK
    ki = _lane_iota((1, 2 * Q_BLOCK))
    delta = qi - ki
    band_bias = jnp.where((delta >= 0) & (delta <= span), 0.0, MASKED)
    lo = _lane_iota((Q_BLOCK, LANES)) < HEAD_DIM

    def item(idx, carry):
        j = idx // dil
        base = j * unit + idx % dil
        if dil > 1:
            q_rows, kv_rows = pl.ds(base, Q_BLOCK, stride=dil), pl.ds(base, 2 * Q_BLOCK, stride=dil)
        else:
            base = pl.multiple_of(base, Q_BLOCK)
            q_rows, kv_rows = pl.ds(base, Q_BLOCK), pl.ds(base, 2 * Q_BLOCK)
        bias = jnp.where((n * nb + j == 0) & (ki < Q_BLOCK), MASKED, band_bias)
        q = q_ref[q_rows, :]
        k = kbuf[kv_rows, :].astype(BF16)
        v = vbuf[kv_rows, :].astype(BF16)
        outs, lses = [], []
        for e in range(2):
            qh = jnp.where(lo if e == 0 else ~lo, q, 0.0).astype(BF16)
            s = _qk(qh, k) + bias
            m = jnp.maximum(jnp.max(s, axis=-1, keepdims=True), M_INIT)
            ex = jnp.exp(s - m)
            den = jnp.maximum(jnp.sum(ex, axis=-1, keepdims=True), 1e-30)
            outs.append(jnp.dot(ex.astype(BF16), v, preferred_element_type=F32) / den)
            lses.append(jnp.broadcast_to(m + jnp.log(den), (Q_BLOCK, LANES)))
        o_ref[q_rows, :] = jnp.where(lo, outs[0], outs[1])
        lse_ref[q_rows, :] = jnp.where(lo, lses[0], lses[1])
        return carry

    lax.fori_loop(0, nb * dil, item, 0, unroll=8)


def _dil_attn(qb, kb, vb, gidx, win, dil):
    s = qb.shape[0]
    unit = dil * Q_BLOCK
    tb = min(max(DIL_STEP_TOKENS, unit), s)
    assert s % tb == 0 and tb % unit == 0
    nb = tb // unit
    span = win // dil
    n_chunks = DIL_OUT_W // LANES
    cur = lambda n, c: (n, gidx * n_chunks + c)
    prev = lambda n, c: (jnp.maximum(n * nb - 1, 0), gidx * n_chunks + c)
    blk, pblk = (tb, LANES), (unit, LANES)
    vmem = (2 * (5 * _nbytes(blk, F32) + 2 * _nbytes(pblk, F32)) + 2 * _nbytes((tb + unit, LANES), F32)
            + 32 * _nbytes((Q_BLOCK, 2 * Q_BLOCK), F32))
    return pl.pallas_call(
        functools.partial(_dil_attn_kernel, span=span, dil=dil, nb=nb),
        grid=(s // tb, n_chunks),
        in_specs=[pl.BlockSpec(blk, cur), pl.BlockSpec(pblk, prev), pl.BlockSpec(blk, cur),
                  pl.BlockSpec(pblk, prev), pl.BlockSpec(blk, cur)],
        out_specs=[pl.BlockSpec(blk, lambda n, c: (n, c)), pl.BlockSpec(blk, lambda n, c: (n, c))],
        out_shape=[jax.ShapeDtypeStruct((s, DIL_OUT_W), F32)] * 2,
        scratch_shapes=[pltpu.VMEM((tb + unit, LANES), F32)] * 2,
        compiler_params=_cparams(("parallel", "parallel"), vmem),
        name=f"dilated_w{win}_d{dil}",
    )(qb, kb, kb, vb, vb)


def _sgu_kernel(uv_ref, g_ref, b_ref, ws_ref, bs_ref, o_ref):
    uv = jax.nn.gelu(uv_ref[...])
    u = uv[:, :SGU_WIDTH]
    v = _layer_norm(uv[:, SGU_WIDTH:], g_ref[...], b_ref[...]).astype(BF16)
    causal = _lane_iota((SGU_CHUNK, SGU_CHUNK)) <= _row_iota((SGU_CHUNK, SGU_CHUNK))
    bs = bs_ref[...]
    for n in range(uv.shape[0] // SGU_CHUNK):
        rs = slice(n * SGU_CHUNK, (n + 1) * SGU_CHUNK)
        for g in range(SGU_GROUPS):
            cs = slice(g * SGU_GROUP_CH, (g + 1) * SGU_GROUP_CH)
            ws = jnp.where(causal, ws_ref[g], 0.0).astype(BF16)
            sv = jnp.dot(ws, v[rs, cs], preferred_element_type=F32) + bs[:, cs]
            o_ref[rs, cs] = (u[rs, cs] * sv).astype(o_ref.dtype)


def _sgu(uv, ln_g, ln_b, w_s, b_s):
    s = uv.shape[0]
    tm = min(4 * SGU_CHUNK, s)
    assert s % tm == 0
    bs = jnp.repeat(b_s.T, SGU_GROUP_CH, axis=1)
    c2 = lambda i: (0, 0)
    vmem = 2 * (_nbytes((tm, 2 * SGU_WIDTH), F32) + _nbytes((tm, SGU_WIDTH), F32)) + 6 * _nbytes((tm, 2 * SGU_WIDTH), F32)
    return pl.pallas_call(
        _sgu_kernel,
        grid=(s // tm,),
        in_specs=[pl.BlockSpec((tm, 2 * SGU_WIDTH), lambda i: (i, 0)), pl.BlockSpec((1, SGU_WIDTH), c2),
                  pl.BlockSpec((1, SGU_WIDTH), c2), pl.BlockSpec(w_s.shape, lambda i: (0, 0, 0)),
                  pl.BlockSpec((SGU_CHUNK, SGU_WIDTH), c2)],
        out_specs=pl.BlockSpec((tm, SGU_WIDTH), lambda i: (i, 0)),
        out_shape=jax.ShapeDtypeStruct((s, SGU_WIDTH), BF16),
        compiler_params=_cparams(("parallel",), vmem),
        name="sgu",
    )(uv, ln_g.reshape(1, -1), ln_b.reshape(1, -1), w_s, bs)


def _split_dot(x, w2):
    hi = x.astype(BF16)
    lo = (x - hi.astype(F32)).astype(BF16)
    return jnp.dot(jnp.concatenate([hi, lo], axis=1), w2, preferred_element_type=F32)


def _mix_kernel(x_ref, oc_ref, os_ref, ow_ref, ga_ref, eg_ref, d0_ref, d1_ref, d2_ref, l0_ref, l1_ref, l2_ref,
                sg_ref, gm_ref, wa_ref, wb_ref, wc_ref, wo_ref, g_ref, b_ref, o_ref, *, alpha):
    d = x_ref.shape[1]
    gates = _split_dot(jax.nn.sigmoid(ga_ref[...]), eg_ref[...])
    o_a = (gates[:, 0:NSA_Q_W] * oc_ref[...].astype(F32) + gates[:, NSA_Q_W:2 * NSA_Q_W] * os_ref[...].astype(F32)
           + gates[:, 2 * NSA_Q_W:3 * NSA_Q_W] * ow_ref[...].astype(F32))
    y_a = jnp.dot(o_a.astype(BF16), wa_ref[...], preferred_element_type=F32)
    l0, l1, l2 = l0_ref[...], l1_ref[...], l2_ref[...]
    lm = jnp.maximum(jnp.maximum(l0, l1), l2)
    e0, e1, e2 = jnp.exp(l0 - lm), jnp.exp(l1 - lm), jnp.exp(l2 - lm)
    den = e0 + e1 + e2
    o_b = (e0 / den) * d0_ref[...] + (e1 / den) * d1_ref[...] + (e2 / den) * d2_ref[...]
    y_b = jnp.dot(o_b.astype(BF16), wb_ref[...], preferred_element_type=F32)
    y_c = jnp.dot(sg_ref[...].astype(BF16), wc_ref[...], preferred_element_type=F32)
    gm = jax.nn.sigmoid(gm_ref[...].astype(F32))
    merged = gm[:, 0:d] * y_a + gm[:, d:2 * d] * y_b + gm[:, 2 * d:3 * d] * y_c
    mix = jnp.dot(merged.astype(BF16), wo_ref[...], preferred_element_type=F32)
    o_ref[...] = _layer_norm(alpha * x_ref[...] + mix, g_ref[...], b_ref[...])


def _mix(x, o_c, o_s, o_w, g_a, dil_o, dil_lse, sgu, g_m, wa, wb, wc, wo, g, b, alpha):
    s, d = x.shape
    tm = min(256, s)
    assert s % tm == 0
    col = np.arange(N_BRANCH * NSA_Q_W)
    head, br = (col % NSA_Q_W) // HEAD_DIM, col // NSA_Q_W
    eg1 = (np.arange(LANES)[:, None] == (head * N_BRANCH + br)[None, :]).astype(np.float32)
    eg = jnp.asarray(np.concatenate([eg1, eg1], axis=0), BF16)
    row = lambda i: (i, 0)
    const = lambda i: (0, 0)
    rspec = lambda w: pl.BlockSpec((tm, w), row)
    cspec = lambda a: pl.BlockSpec(a.shape, const)
    widths = [d, NSA_Q_W, NSA_Q_W, NSA_Q_W, LANES] + [DIL_OUT_W] * 6 + [SGU_WIDTH, N_BRANCH * d]
    vmem = (2 * sum(_nbytes((tm, w), F32) for w in widths) + 2 * _nbytes((tm, d), F32)
            + 2 * sum(_nbytes(a.shape, BF16) for a in (eg, wa, wb, wc, wo)) + 10 * _nbytes((tm, N_BRANCH * d), F32))
    return pl.pallas_call(
        functools.partial(_mix_kernel, alpha=alpha),
        grid=(s // tm,),
        in_specs=[rspec(d), rspec(NSA_Q_W), rspec(NSA_Q_W), rspec(NSA_Q_W), rspec(LANES), cspec(eg)]
                 + [rspec(DIL_OUT_W)] * 6 + [rspec(SGU_WIDTH), rspec(N_BRANCH * d)]
                 + [cspec(wa), cspec(wb), cspec(wc), cspec(wo), pl.BlockSpec((1, d), const), pl.BlockSpec((1, d), const)],
        out_specs=rspec(d),
        out_shape=jax.ShapeDtypeStruct((s, d), F32),
        compiler_params=_cparams(("parallel",), vmem),
        name="mixer_merge_deepnorm",
    )(x, o_c, o_s, o_w, g_a, eg, *dil_o, *dil_lse, sgu, g_m, wa, wb, wc, wo, g.reshape(1, d), b.reshape(1, d))


def _rope_tables(s):
    inv = 1.0 / (ROPE_THETA ** (np.arange(0, HEAD_DIM, 2, dtype=np.float32) / HEAD_DIM)).astype(np.float32)
    ang = (np.arange(s, dtype=np.float32)[:, None] * inv[None, :]).astype(np.float64)
    cos, sin = jnp.asarray(np.cos(ang), F32), jnp.asarray(np.sin(ang), F32)
    reps = LANES // HEAD_DIM
    return jnp.tile(jnp.concatenate([cos, cos], axis=1), (1, reps)), jnp.tile(jnp.concatenate([-sin, sin], axis=1), (1, reps))


def kernel(x, ln_g, ln_b, ffn1_gate, ffn1_up, ffn1_down, ffn2_gate, ffn2_up, ffn2_down, w_in, phi_k_pos, phi_k_w1, phi_k_w2, phi_v_pos, phi_v_w1, phi_v_w2, sgu_ln_g, sgu_ln_b, sgu_w, sgu_b, w_branch_a, w_branch_b, w_branch_c, w_out):
    bsz, s, d = x.shape
    depth = ln_g.shape[0]
    alpha = float((2 * depth) ** 0.25)
    q_scale = float(HEAD_DIM ** -0.5)
    tabs = _rope_tables(s)
    sizes = [NSA_Q_W] + [NSA_KV_W] * 6 + [N_BRANCH * NSA_HEADS, DIL_W, DIL_W, DIL_W, 2 * SGU_WIDTH, N_BRANCH * d]
    offs = np.concatenate([[0], np.cumsum(sizes)]).tolist()
    outs = []
    for bi in range(bsz):
        h = x[bi]
        for l in range(depth):
            (w_qa, w_kc, w_vc, w_ks, w_vs, w_kw, w_vw, w_ga, w_qb, w_kb, w_vb, w_uv, w_gm) = [
                w_in[l][:, offs[i]:offs[i + 1]] for i in range(len(sizes))]
            w_ga = jnp.pad(w_ga, ((0, 0), (0, LANES - w_ga.shape[1])))
            groups = dict(qb=w_qb, kb=w_kb, vb=w_vb, k_rot=jnp.concatenate([w_ks, w_kw], axis=1),
                          kv_c=jnp.concatenate([w_kc, w_vc], axis=1), v_plain=jnp.concatenate([w_vs, w_vw], axis=1),
                          uv=w_uv, gm=w_gm, qa=w_qa, ga=w_ga)
            wp = jnp.concatenate(list(groups.values()), axis=1).astype(BF16)
            ends = np.cumsum([g.shape[1] for g in groups.values()]).tolist()
            cols = {name: (e - g.shape[1], e) for (name, g), e in zip(groups.items(), ends)}

            h, hb = _ffn(h, ffn1_gate[l].astype(BF16), ffn1_up[l].astype(BF16), ffn1_down[l].astype(BF16),
                         ln_g[l, 0], ln_b[l, 0], alpha)

            q_rot, q_u = _mm(hb, wp, cols["qa"], BF16, rope_tabs=tabs, scale=q_scale * LOG2E, plain_scale=q_scale * LOG2E)
            k_rot = _mm(hb, wp, cols["k_rot"], BF16, rope_tabs=tabs)
            qb = _mm(hb, wp, cols["qb"], F32, rope_tabs=tabs, scale=q_scale)
            kb = _mm(hb, wp, cols["kb"], F32, rope_tabs=tabs)
            vb = _mm(hb, wp, cols["vb"], F32)
            kv_c = _mm(hb, wp, cols["kv_c"], F32)
            v_plain = _mm(hb, wp, cols["v_plain"], BF16)
            uv = _mm(hb, wp, cols["uv"], F32)
            g_m = _mm(hb, wp, cols["gm"], BF16)
            g_a = _mm(hb, wp, cols["ga"], F32)

            kc = _compress(kv_c, 0, phi_k_pos[l], phi_k_w1[l], phi_k_w2[l])
            vc = _compress(kv_c, 1, phi_v_pos[l], phi_v_w1[l], phi_v_w2[l])
            o_c, nsel = _cmp_select(q_u, kc, vc)
            o_s, o_w = _sel_attn(q_rot, k_rot[:, :NSA_KV_W], v_plain[:, :NSA_KV_W], nsel,
                                 k_rot[:, NSA_KV_W:], v_plain[:, NSA_KV_W:])
            dil = [_dil_attn(qb, kb, vb, gi, win, dl) for gi, (win, dl) in enumerate(DIL_PAIRS)]
            sg = _sgu(uv, sgu_ln_g[l], sgu_ln_b[l], sgu_w[l], sgu_b[l])
            h = _mix(h, o_c, o_s, o_w, g_a, [o for o, _ in dil], [e for _, e in dil], sg, g_m,
                     w_branch_a[l].astype(BF16), w_branch_b[l].astype(BF16), w_branch_c[l].astype(BF16),
                     w_out[l].astype(BF16), ln_g[l, 1], ln_b[l, 1], alpha)

            h, _ = _ffn(h, ffn2_gate[l].astype(BF16), ffn2_up[l].astype(BF16), ffn2_down[l].astype(BF16),
                        ln_g[l, 2], ln_b[l, 2], alpha)
        outs.append(h)
    return jnp.stack(outs, axis=0)
```

```python
import functools

import numpy as np
import jax
import jax.numpy as jnp
from jax import lax
from jax.experimental import pallas as pl
from jax.experimental.pallas import tpu as pltpu

F32 = jnp.float32
BF16 = jnp.bfloat16

HEAD_DIM = 64
ROPE_THETA = 10000.0
LN_EPS = 1e-5
Q_BLOCK = 128
NSA_HEADS = 8
NSA_GROUPS = 2
NSA_HPG = NSA_HEADS // NSA_GROUPS
NSA_Q_W = NSA_HEADS * HEAD_DIM
NSA_KV_W = NSA_GROUPS * HEAD_DIM
CMP_LEN = 32
CMP_STRIDE = 16
SLC_BLOCK = 64
SLC_RATIO = SLC_BLOCK // CMP_STRIDE
SLC_TOPK = 16
MM_ROWS = 2048
MM_SUB_ROWS = 512
FFN_ROWS = 1024
CMP_Q_BLOCK = 256
DIL_STEP_TOKENS = 2048
SEL_TILE = 1024
SEL_VT_ROWS = 80
NSA_WINDOW = 512
FORCE_SCORE = 1e6
DIL_PAIRS = ((128, 1), (512, 4), (2048, 16))
DIL_HPG = 4
DIL_HEADS = DIL_HPG * 3
DIL_W = DIL_HEADS * HEAD_DIM
DIL_OUT_W = DIL_HPG * HEAD_DIM
SGU_CHUNK = 128
SGU_GROUPS = 4
SGU_GROUP_CH = 128
SGU_WIDTH = SGU_GROUPS * SGU_GROUP_CH
N_BRANCH = 3

LANES = 128
V7X_VMEM_BYTES = 64 * 1024 * 1024
VMEM_CAP = 56 * 1024 * 1024

M_INIT = -1e30
MASKED = -(2.0 ** 101)
LOG2E = 1.4426950408889634


def _cparams(sem, vmem_bytes):
    return pltpu.CompilerParams(dimension_semantics=sem,
                                vmem_limit_bytes=int(min(max(vmem_bytes, 16 * 1024 * 1024), VMEM_CAP)))


def _nbytes(shape, dtype):
    return int(np.prod(shape)) * jnp.dtype(dtype).itemsize


def _layer_norm(y, g, b):
    mu = jnp.mean(y, axis=-1, keepdims=True)
    d = y - mu
    var = jnp.mean(d * d, axis=-1, keepdims=True)
    return d * lax.rsqrt(var + LN_EPS) * g + b


def _lane_iota(shape):
    return lax.broadcasted_iota(jnp.int32, shape, len(shape) - 1)


def _row_iota(shape):
    return lax.broadcasted_iota(jnp.int32, shape, 0)


def _ffn_kernel(x_ref, wg_ref, wu_ref, wd_ref, g_ref, b_ref, o_ref, ob_ref, *, alpha, fc):
    n_f = wg_ref.shape[1]
    for r0 in range(0, x_ref.shape[0], MM_SUB_ROWS):
        rs = slice(r0, r0 + MM_SUB_ROWS)
        x = x_ref[rs, :]
        xb = x.astype(BF16)
        acc = jnp.zeros(x.shape, F32)
        for c in range(n_f // fc):
            sl = slice(c * fc, (c + 1) * fc)
            gate = jnp.dot(xb, wg_ref[:, sl], preferred_element_type=F32)
            up = jnp.dot(xb, wu_ref[:, sl], preferred_element_type=F32)
            h = (gate * jax.nn.sigmoid(gate)) * up
            acc = acc + jnp.dot(h.astype(BF16), wd_ref[sl, :], preferred_element_type=F32)
        out = _layer_norm(alpha * x + 0.5 * acc, g_ref[...], b_ref[...])
        o_ref[rs, :] = out
        ob_ref[rs, :] = out.astype(BF16)


def _ffn(x, wg, wu, wd, g, b, alpha):
    s, d = x.shape
    n_f = wg.shape[1]
    tm = min(FFN_ROWS, s)
    fc = 256
    assert s % tm == 0 and tm % MM_SUB_ROWS == 0 and n_f % fc == 0
    const = lambda i: (0, 0)
    row = lambda i: (i, 0)
    once = pl.Buffered(1)
    vmem = (2 * 2 * _nbytes((tm, d), F32) + 2 * _nbytes((tm, d), BF16)
            + 3 * _nbytes((d, n_f), BF16) + 8 * _nbytes((MM_SUB_ROWS, d), F32))
    return pl.pallas_call(
        functools.partial(_ffn_kernel, alpha=alpha, fc=fc),
        grid=(s // tm,),
        in_specs=[pl.BlockSpec((tm, d), row), pl.BlockSpec((d, n_f), const, pipeline_mode=once),
                  pl.BlockSpec((d, n_f), const, pipeline_mode=once),
                  pl.BlockSpec((n_f, d), const, pipeline_mode=once),
                  pl.BlockSpec((1, d), const), pl.BlockSpec((1, d), const)],
        out_specs=[pl.BlockSpec((tm, d), row), pl.BlockSpec((tm, d), row)],
        out_shape=[jax.ShapeDtypeStruct((s, d), F32), jax.ShapeDtypeStruct((s, d), BF16)],
        compiler_params=_cparams(("parallel",), vmem),
        name="ffn_deepnorm",
    )(x, wg, wu, wd, g.reshape(1, d), b.reshape(1, d))


def _swap_halves_64(a):
    half = HEAD_DIM // 2
    first = (_lane_iota(a.shape) % HEAD_DIM) < half
    return jnp.where(first, pltpu.roll(a, LANES - half, 1), pltpu.roll(a, half, 1))


def _mm_kernel(*refs, rope, scale, plain_scale):
    if rope:
        x_ref, w_ref, c_ref, s_ref, o_ref = refs[:5]
    else:
        x_ref, w_ref, o_ref = refs
    for r0 in range(0, x_ref.shape[0], MM_SUB_ROWS):
        rs = slice(r0, r0 + MM_SUB_ROWS)
        acc = jnp.dot(x_ref[rs, :], w_ref[...], preferred_element_type=F32)
        if plain_scale is not None:
            refs[5][rs, :] = (acc * plain_scale).astype(refs[5].dtype)
        if rope:
            cos = c_ref[rs, :]
            sin = s_ref[rs, :]
            for c in range(acc.shape[1] // LANES):
                a = acc[:, c * LANES:(c + 1) * LANES]
                r = a * cos + _swap_halves_64(a) * sin
                if scale != 1.0:
                    r = r * scale
                o_ref[rs, c * LANES:(c + 1) * LANES] = r.astype(o_ref.dtype)
        else:
            if scale != 1.0:
                acc = acc * scale
            o_ref[rs, :] = acc.astype(o_ref.dtype)


def _mm(xb, w, cols, out_dtype, *, rope_tabs=None, scale=1.0, plain_scale=None):
    s, k = xb.shape
    col0, n = cols[0], cols[1] - cols[0]
    tm = min(MM_ROWS, s)
    tn = next(c for c in (1024, 768, 512, 256, LANES) if n % c == 0 and col0 % c == 0)
    assert s % tm == 0 and tm % MM_SUB_ROWS == 0 and (plain_scale is None or rope_tabs is not None)
    jb = col0 // tn
    n_out = 1 if plain_scale is None else 2
    in_specs = [pl.BlockSpec((tm, k), lambda i, j: (i, 0)), pl.BlockSpec((k, tn), lambda i, j: (0, jb + j))]
    args = [xb, w]
    if rope_tabs is not None:
        in_specs += [pl.BlockSpec((tm, LANES), lambda i, j: (i, 0))] * 2
        args += list(rope_tabs)
    vmem = 2 * (_nbytes((tm, k), BF16) + _nbytes((k, tn), BF16) + n_out * _nbytes((tm, tn), out_dtype)
                + 2 * _nbytes((tm, LANES), F32)) + 6 * _nbytes((MM_SUB_ROWS, tn), F32)
    out = pl.pallas_call(
        functools.partial(_mm_kernel, rope=rope_tabs is not None, scale=scale, plain_scale=plain_scale),
        grid=(s // tm, n // tn),
        in_specs=in_specs,
        out_specs=[pl.BlockSpec((tm, tn), lambda i, j: (i, j))] * n_out,
        out_shape=[jax.ShapeDtypeStruct((s, n), out_dtype)] * n_out,
        compiler_params=_cparams(("parallel", "parallel"), vmem),
        name="proj_rope" if rope_tabs is not None else "proj",
    )(*args)
    return out[0] if n_out == 1 else out


def _compress_kernel(x_ref, pos_ref, w1_ref, w2_ref, o_ref):
    n_cmp = o_ref.shape[0]
    hidden = w1_ref.shape[3]
    ha = [jnp.zeros((n_cmp, hidden), F32) for _ in range(NSA_GROUPS)]
    hb = [jnp.zeros((n_cmp, hidden), F32) for _ in range(NSA_GROUPS)]
    tail = [jnp.zeros((8, hidden), F32) for _ in range(NSA_GROUPS)]
    for j in range(CMP_STRIDE):
        xj = x_ref[pl.ds(j, n_cmp, stride=CMP_STRIDE), :]
        pa = pos_ref[j:j + 1, :]
        pb = pos_ref[CMP_STRIDE + j:CMP_STRIDE + j + 1, :]
        xa = (xj + pa).astype(BF16)
        xb = (xj + pb).astype(BF16)
        pb8 = jnp.broadcast_to(pb, (8, pb.shape[1])).astype(BF16)
        for g in range(NSA_GROUPS):
            ha[g] = ha[g] + jnp.dot(xa, w1_ref[g, j], preferred_element_type=F32)
            hb[g] = hb[g] + jnp.dot(xb, w1_ref[g, CMP_STRIDE + j], preferred_element_type=F32)
            tail[g] = tail[g] + jnp.dot(pb8, w1_ref[g, CMP_STRIDE + j], preferred_element_type=F32)
    last = _row_iota((n_cmp, 1)) == n_cmp - 1
    outs = []
    for g in range(NSA_GROUPS):
        hb_next = jnp.where(last, tail[g][0:1, :], pltpu.roll(hb[g], n_cmp - 1, 0))
        h = jax.nn.gelu(ha[g] + hb_next)
        outs.append(jnp.dot(h.astype(BF16), w2_ref[...], preferred_element_type=F32))
    lo = _lane_iota(outs[0].shape) < HEAD_DIM
    o_ref[...] = jnp.where(lo, outs[0], outs[1]).astype(o_ref.dtype)


def _compress(x, col, pos, w1, w2):
    s = x.shape[0]
    n_cmp = s // CMP_STRIDE
    hidden = w1.shape[1]
    pos2 = jnp.concatenate([pos] * NSA_GROUPS, axis=1)
    w1r = w1.reshape(CMP_LEN, HEAD_DIM, hidden).astype(BF16)
    zero = jnp.zeros_like(w1r)
    w1g = jnp.stack([jnp.concatenate([w1r if gg == g else zero for gg in range(NSA_GROUPS)], axis=1)
                     for g in range(NSA_GROUPS)])
    w2d = jnp.concatenate([w2, w2], axis=1).astype(BF16)
    vmem = 2 * (_nbytes((s, NSA_KV_W), F32) + _nbytes(w1g.shape, BF16)) + 16 * _nbytes((n_cmp, hidden), F32)
    return pl.pallas_call(
        _compress_kernel,
        grid=(1,),
        in_specs=[pl.BlockSpec((s, NSA_KV_W), lambda i: (0, col)), pl.BlockSpec(pos2.shape, lambda i: (0, 0)),
                  pl.BlockSpec(w1g.shape, lambda i: (0, 0, 0, 0)), pl.BlockSpec(w2d.shape, lambda i: (0, 0))],
        out_specs=pl.BlockSpec((n_cmp, NSA_KV_W), lambda i: (0, 0)),
        out_shape=jax.ShapeDtypeStruct((n_cmp, NSA_KV_W), BF16),
        compiler_params=_cparams(("arbitrary",), vmem),
        name="nsa_compress",
    )(x, pos2, w1g, w2d)


def _stack_group_queries(q_ref, col0, kv_lane_group):
    parts = []
    for h in range(NSA_HPG):
        c = col0 + (h // 2) * LANES
        qc = q_ref[:, c:c + LANES].astype(F32)
        e = h % 2
        lane = _lane_iota(qc.shape)
        mine = (lane >= e * HEAD_DIM) & (lane < (e + 1) * HEAD_DIM)
        qh = jnp.where(mine, qc, 0.0)
        if e != kv_lane_group:
            qh = pltpu.roll(qh, HEAD_DIM, 1)
        parts.append(qh.astype(BF16))
    return jnp.concatenate(parts, axis=0)


def _unstack_group_outputs(o, kv_lane_group):
    chunks = []
    tq = o.shape[0] // NSA_HPG
    for c in range(NSA_HPG // 2):
        halves = []
        for e in range(2):
            h = 2 * c + e
            oh = o[h * tq:(h + 1) * tq, :]
            if e != kv_lane_group:
                oh = pltpu.roll(oh, HEAD_DIM, 1)
            halves.append(oh)
        lo = _lane_iota(halves[0].shape) < HEAD_DIM
        chunks.append(jnp.where(lo, halves[0], halves[1]))
    return jnp.concatenate(chunks, axis=1)


def _qk(q, k):
    return lax.dot_general(q, k, (((1,), (1,)), ((), ())), preferred_element_type=F32)


def _cmp_select_kernel(q_ref, kc_ref, vc_ref, mmap_ref, oc_ref, nsel_ref, *, top_k, n_cls):
    b = pl.program_id(0)
    n_cmp = kc_ref.shape[0]
    n_slc = mmap_ref.shape[0]
    tq = q_ref.shape[0]
    rows = NSA_HPG * tq
    gw = NSA_HPG * HEAD_DIM
    t_row = b * tq + _row_iota((tq, 1))
    t = b * tq + _lane_iota((1, tq))

    def variant(nc, nb):
        valid = _lane_iota((1, nc)) * CMP_STRIDE + (CMP_LEN - 1) <= t_row
        bias = jnp.concatenate([jnp.where(valid, 0.0, MASKED)] * NSA_HPG, axis=0)
        blk = _row_iota((nb, 1))
        cur = t // SLC_BLOCK
        forced = (blk == 0) | (blk == cur) | (blk == cur - 1)
        future = blk * SLC_BLOCK > t
        blk_f = blk.astype(F32)
        kc = kc_ref[0:nc, :]
        vc = vc_ref[0:nc, :]
        mmap_t = mmap_ref[0:nb, 0:nc]
        for g in range(NSA_GROUPS):
            qst = _stack_group_queries(q_ref, g * gw, g)
            s = _qk(qst, kc) + bias
            m = jnp.maximum(jnp.max(s, axis=-1, keepdims=True), M_INIT)
            e = jnp.exp2(s - m)
            den = jnp.sum(e, axis=-1, keepdims=True)
            p = e / jnp.maximum(den, 1e-30)
            o = jnp.dot(p.astype(BF16), vc, preferred_element_type=F32)
            oc_ref[:, g * gw:(g + 1) * gw] = _unstack_group_outputs(o, g).astype(oc_ref.dtype)
            psum = p[0:tq]
            for h in range(1, NSA_HPG):
                psum = psum + p[h * tq:(h + 1) * tq]
            hi = psum.astype(BF16)
            r1 = psum - hi.astype(F32)
            mid = r1.astype(BF16)
            low = (r1 - mid.astype(F32)).astype(BF16)
            p_slc = _qk(mmap_t, hi) + _qk(mmap_t, mid) + _qk(mmap_t, low)
            score = jnp.where(forced, FORCE_SCORE, jnp.where(future, -1.0, p_slc))
            for _ in range(top_k):
                mx = jnp.max(score, axis=0, keepdims=True)
                first = jnp.min(jnp.where(score == mx, blk_f, float(nb)), axis=0, keepdims=True)
                score = jnp.where(blk_f == first, -2.0, score)
            not_picked = jnp.where(score == -2.0, 0.0, 1.0)
            nsel_ref[:, g * n_slc:g * n_slc + nb] = not_picked.T.astype(nsel_ref.dtype)
            if nb < n_slc:
                nsel_ref[:, g * n_slc + nb:(g + 1) * n_slc] = jnp.ones((tq, n_slc - nb), nsel_ref.dtype)

    for c in range(n_cls):
        nc = (c + 1) * (n_cmp // n_cls)
        nb = min(n_slc, -(-(nc // SLC_RATIO) // LANES) * LANES)
        lo_b = c * (nc // (c + 1)) // (tq // CMP_STRIDE)
        hi_b = nc // (tq // CMP_STRIDE)
        pl.when((b >= lo_b) & (b < hi_b))(functools.partial(variant, nc, nb))


def _cmp_select(q_u, kc, vc):
    s = q_u.shape[0]
    n_cmp, n_slc = s // CMP_STRIDE, s // SLC_BLOCK
    top_k = min(SLC_TOPK, n_slc)
    n_cls = max(n_cmp // (2 * LANES), 1)
    tq = min(CMP_Q_BLOCK, s)
    assert s % tq == 0 and n_cmp % n_cls == 0 and (n_cmp // n_cls) % (tq // CMP_STRIDE) == 0
    m = np.arange(n_cmp)[:, None]
    j = np.arange(n_slc)[None, :]
    mmap = jnp.asarray(((m >= SLC_RATIO * j - 1) & (m <= SLC_RATIO * j + SLC_RATIO - 1)).astype(np.float32).T, BF16)
    const = lambda b: (0, 0)
    rows = NSA_HPG * tq
    vmem = 2 * (2 * _nbytes((n_cmp, NSA_KV_W), BF16) + _nbytes((n_cmp, n_slc), BF16)
                + _nbytes((tq, NSA_Q_W), F32) * 2) + 6 * _nbytes((rows, n_cmp), F32)
    return pl.pallas_call(
        functools.partial(_cmp_select_kernel, top_k=top_k, n_cls=n_cls),
        grid=(s // tq,),
        in_specs=[pl.BlockSpec((tq, NSA_Q_W), lambda b: (b, 0)), pl.BlockSpec((n_cmp, NSA_KV_W), const),
                  pl.BlockSpec((n_cmp, NSA_KV_W), const), pl.BlockSpec((n_slc, n_cmp), const)],
        out_specs=[pl.BlockSpec((tq, NSA_Q_W), lambda b: (b, 0)),
                   pl.BlockSpec((tq, NSA_GROUPS * n_slc), lambda b: (b, 0))],
        out_shape=[jax.ShapeDtypeStruct((s, NSA_Q_W), BF16), jax.ShapeDtypeStruct((s, NSA_GROUPS * n_slc), BF16)],
        compiler_params=_cparams(("parallel",), vmem),
        name="nsa_compressed_select",
    )(q_u, kc, vc, mmap)


def _sel_attn_kernel(q_ref, k_ref, vt_ref, nsel_ref, kw_ref, vwt_ref, o_ref, ow_ref, qa_ref, m_ref, acc_ref, s_ref,
                     *, win, w_span):
    b = pl.program_id(0)
    n_slc = nsel_ref.shape[1] // NSA_GROUPS
    n_win = n_slc // win
    tiles_per_win = win * SLC_BLOCK // SEL_TILE
    gw = NSA_HPG * HEAD_DIM
    last = (b * Q_BLOCK) // SEL_TILE
    t_lane = b * Q_BLOCK + _lane_iota((1, Q_BLOCK))
    lane = _lane_iota((Q_BLOCK, LANES))
    head_lanes = lane < HEAD_DIM
    flag_lanes = (lane >= HEAD_DIM) & (lane < HEAD_DIM + win)

    nsel_all = nsel_ref[...].astype(F32)
    if nsel_all.shape[1] < LANES:
        nsel_all = jnp.concatenate([nsel_all, jnp.zeros((Q_BLOCK, LANES - nsel_all.shape[1]), F32)], axis=1)
    for g in range(NSA_GROUPS):
        heads = []
        for h in range(NSA_HPG):
            c = g * gw + (h // 2) * LANES
            qc = q_ref[:, c:c + LANES].astype(F32)
            heads.append(qc if h % 2 == 0 else pltpu.roll(qc, HEAD_DIM, 1))
        for w in range(n_win):
            a = g * n_slc + w * win
            fl = nsel_all[:, (a // LANES) * LANES:(a // LANES + 1) * LANES]
            shift = (HEAD_DIM - a % LANES) % LANES
            if shift:
                fl = pltpu.roll(fl, shift, 1)
            fl = jnp.where(flag_lanes, fl, 0.0)
            for h in range(NSA_HPG):
                qa_ref[g, w, h * Q_BLOCK:(h + 1) * Q_BLOCK, :] = jnp.where(head_lanes, heads[h], fl).astype(BF16)
    m_ref[...] = jnp.full(m_ref.shape, M_INIT, F32)
    acc_ref[...] = jnp.zeros(acc_ref.shape, F32)

    def heads_to_lanes(o_t, out_ref, g):
        o_t = jnp.concatenate([o_t, jnp.zeros((LANES - o_t.shape[0], o_t.shape[1]), F32)], axis=0)
        heads = [o_t[:, h * Q_BLOCK:(h + 1) * Q_BLOCK].T for h in range(NSA_HPG)]
        for c in range(NSA_HPG // 2):
            even, odd = heads[2 * c], pltpu.roll(heads[2 * c + 1], HEAD_DIM, 1)
            out_ref[:, g * gw + c * LANES:g * gw + (c + 1) * LANES] = jnp.where(head_lanes, even, odd).astype(out_ref.dtype)

    n_back = NSA_WINDOW // Q_BLOCK
    blk0 = jnp.maximum(b - n_back, 0)
    start = pl.multiple_of(blk0 * Q_BLOCK, Q_BLOCK)
    diff = t_lane - (start + _row_iota((w_span, 1)))
    w_bias = jnp.where((diff >= 0) & (diff < NSA_WINDOW), 0.0, MASKED)
    w_bias = jnp.concatenate([w_bias] * NSA_HPG, axis=1)
    for g in range(NSA_GROUPS):
        s = _qk(kw_ref[g, pl.ds(start, w_span), :], qa_ref[g, 0]) + w_bias
        m = jnp.maximum(jnp.max(s, axis=0, keepdims=True), M_INIT)
        p = jnp.exp2(s - m).astype(BF16)
        vt = jnp.concatenate([vwt_ref[g, blk0 + i] for i in range(w_span // Q_BLOCK)], axis=1)
        acc = jnp.dot(vt, p, preferred_element_type=F32)
        heads_to_lanes(acc / jnp.maximum(acc[HEAD_DIM:HEAD_DIM + 1, :], 1e-30), ow_ref, g)

    def scores(kt, slot):
        off = pl.multiple_of(kt * SEL_TILE, SEL_TILE)
        for g in range(NSA_GROUPS):
            s_ref[g, slot] = _qk(k_ref[g, pl.ds(off, SEL_TILE), :], qa_ref[g, kt // tiles_per_win])

    def update(kt, slot, causal):
        off = pl.multiple_of(kt * SEL_TILE, SEL_TILE)
        for g in range(NSA_GROUPS):
            s = s_ref[g, slot]
            if causal:
                keep = off + _row_iota((SEL_TILE, 1)) <= t_lane
                s = jnp.where(jnp.concatenate([keep] * NSA_HPG, axis=1), s, MASKED)
            m_old = m_ref[g]
            m_new = jnp.maximum(m_old, jnp.max(s, axis=0, keepdims=True))
            alpha = jnp.exp2(m_old - m_new)
            p = jnp.exp2(s - m_new[0:1, :]).astype(BF16)
            pv = jnp.dot(vt_ref[g, kt], p, preferred_element_type=F32)
            acc_ref[g] = alpha[0:1, :] * acc_ref[g] + pv
            m_ref[g] = m_new

    def finish():
        for g in range(NSA_GROUPS):
            acc = acc_ref[g]
            heads_to_lanes(acc / jnp.maximum(acc[HEAD_DIM:HEAD_DIM + 1, :], 1e-30), o_ref, g)

    scores(0, 0)

    def pair(j, carry):
        scores(2 * j + 1, 1)
        update(2 * j, 0, False)
        scores(2 * j + 2, 0)
        update(2 * j + 1, 1, False)
        return carry

    lax.fori_loop(0, last // 2, pair, 0)

    @pl.when(last % 2 == 1)
    def _():
        scores(last, 1)
        update(last - 1, 0, False)
        update(last, 1, True)
        finish()

    @pl.when(last % 2 == 0)
    def _():
        update(last, 0, True)
        finish()


def _sel_attn(rot_a, k_s, v_s, nsel, k_w, v_w):
    s = rot_a.shape[0]
    n_slc = s // SLC_BLOCK
    win = min(n_slc, LANES - HEAD_DIM)
    assert s % SEL_TILE == 0 and n_slc % win == 0 and (win * SLC_BLOCK) % SEL_TILE == 0
    n_win = n_slc // win
    spare = LANES - HEAD_DIM
    w_span = min(NSA_WINDOW + Q_BLOCK, s)
    blk = (np.arange(s) // SLC_BLOCK) % win
    marks = jnp.asarray(np.where(np.arange(spare)[None, :] == blk[:, None], MASKED, 0.0).astype(np.float32), BF16)
    grp = lambda a, g: a[:, g * HEAD_DIM:(g + 1) * HEAD_DIM]
    k_aug = jnp.stack([jnp.concatenate([grp(k_s, g), marks], axis=1) for g in range(NSA_GROUPS)])
    kw_aug = jnp.stack([jnp.concatenate([grp(k_w, g), jnp.zeros((s, spare), BF16)], axis=1) for g in range(NSA_GROUPS)])

    def transposed_tiles(v, tile):
        vt = _ones_augmented_values(v).reshape(NSA_GROUPS, s // tile, tile, LANES).transpose(0, 1, 3, 2)
        return vt[:, :, :SEL_VT_ROWS, :]

    vt_aug = transposed_tiles(v_s, SEL_TILE)
    vwt_aug = transposed_tiles(v_w, Q_BLOCK)
    rows = NSA_HPG * Q_BLOCK
    once = pl.Buffered(1)
    vmem = (2 * _nbytes(k_aug.shape, BF16) + _nbytes(vt_aug.shape, BF16) + _nbytes(vwt_aug.shape, BF16)
            + 2 * (2 * _nbytes((Q_BLOCK, NSA_Q_W), BF16) + _nbytes((Q_BLOCK, NSA_GROUPS * n_slc), BF16)
                   + _nbytes((Q_BLOCK, NSA_Q_W), BF16))
            + NSA_GROUPS * (n_win * _nbytes((rows, LANES), BF16) + _nbytes((8 + SEL_VT_ROWS, rows), F32)
                            + 2 * _nbytes((SEL_TILE, rows), F32))
            + 4 * _nbytes((SEL_TILE, rows), F32))
    return pl.pallas_call(
        functools.partial(_sel_attn_kernel, win=win, w_span=w_span),
        grid=(s // Q_BLOCK,),
        in_specs=[pl.BlockSpec((Q_BLOCK, NSA_Q_W), lambda b: (b, 0)),
                  pl.BlockSpec(k_aug.shape, lambda b: (0, 0, 0), pipeline_mode=once),
                  pl.BlockSpec(vt_aug.shape, lambda b: (0, 0, 0, 0), pipeline_mode=once),
                  pl.BlockSpec((Q_BLOCK, NSA_GROUPS * n_slc), lambda b: (b, 0)),
                  pl.BlockSpec(kw_aug.shape, lambda b: (0, 0, 0), pipeline_mode=once),
                  pl.BlockSpec(vwt_aug.shape, lambda b: (0, 0, 0, 0), pipeline_mode=once)],
        out_specs=[pl.BlockSpec((Q_BLOCK, NSA_Q_W), lambda b: (b, 0))] * 2,
        out_shape=[jax.ShapeDtypeStruct((s, NSA_Q_W), BF16)] * 2,
        scratch_shapes=[pltpu.VMEM((NSA_GROUPS, n_win, rows, LANES), BF16),
                        pltpu.VMEM((NSA_GROUPS, 8, rows), F32), pltpu.VMEM((NSA_GROUPS, SEL_VT_ROWS, rows), F32),
                        pltpu.VMEM((NSA_GROUPS, 2, SEL_TILE, rows), F32)],
        compiler_params=_cparams(("parallel",), vmem),
        name="nsa_selected_window",
    )(rot_a, k_aug, vt_aug, nsel, kw_aug, vwt_aug)


def _ones_augmented_values(v):
    s = v.shape[0]
    spare = LANES - HEAD_DIM
    ones = jnp.asarray((np.arange(spare)[None, :] == 0).astype(np.float32) * np.ones((s, 1), np.float32), BF16)
    return jnp.stack([jnp.concatenate([v[:, g * HEAD_DIM:(g + 1) * HEAD_DIM], ones], axis=1)
                      for g in range(NSA_GROUPS)])


def _dil_attn_kernel(q_ref, kp_ref, kc_ref, vp_ref, vc_ref, o_ref, lse_ref, kbuf, vbuf, *, span, dil, nb):
    n = pl.program_id(0)
    unit = dil * Q_BLOCK
    kbuf[0:unit, :] = kp_ref[...]
    kbuf[unit:, :] = kc_ref[...]
    vbuf[0:unit, :] = vp_ref[...]
    vbuf[unit:, :] = vc_ref[...]
    qi = _row_iota((Q_BLOCK, 1)) + Q_BLOCK
    ki = _lane_iota((1, 2 * Q_BLOCK))
    delta = qi - ki
    band_bias = jnp.where((delta >= 0) & (delta <= span), 0.0, MASKED)
    lo = _lane_iota((Q_BLOCK, LANES)) < HEAD_DIM

    def item(idx, carry):
        j = idx // dil
        base = j * unit + idx % dil
        if dil > 1:
            q_rows, kv_rows = pl.ds(base, Q_BLOCK, stride=dil), pl.ds(base, 2 * Q_BLOCK, stride=dil)
        else:
            base = pl.multiple_of(base, Q_BLOCK)
            q_rows, kv_rows = pl.ds(base, Q_BLOCK), pl.ds(base, 2 * Q_BLOCK)
        bias = jnp.where((n * nb + j == 0) & (ki < Q_BLOCK), MASKED, band_bias)
        q = q_ref[q_rows, :]
        k = kbuf[kv_rows, :].astype(BF16)
        v = vbuf[kv_rows, :].astype(BF16)
        outs, lses = [], []
        for e in range(2):
            qh = jnp.where(lo if e == 0 else ~lo, q, 0.0).astype(BF16)
            s = _qk(qh, k) + bias
            m = jnp.maximum(jnp.max(s, axis=-1, keepdims=True), M_INIT)
            ex = jnp.exp(s - m)
            den = jnp.maximum(jnp.sum(ex, axis=-1, keepdims=True), 1e-30)
            outs.append(jnp.dot(ex.astype(BF16), v, preferred_element_type=F32) / den)
            lses.append(jnp.broadcast_to(m + jnp.log(den), (Q_BLOCK, LANES)))
        o_ref[q_rows, :] = jnp.where(lo, outs[0], outs[1])
        lse_ref[q_rows, :] = jnp.where(lo, lses[0], lses[1])
        return carry

    lax.fori_loop(0, nb * dil, item, 0, unroll=8)


def _dil_attn(qb, kb, vb, gidx, win, dil):
    s = qb.shape[0]
    unit = dil * Q_BLOCK
    tb = min(max(DIL_STEP_TOKENS, unit), s)
    assert s % tb == 0 and tb % unit == 0
    nb = tb // unit
    span = win // dil
    n_chunks = DIL_OUT_W // LANES
    cur = lambda n, c: (n, gidx * n_chunks + c)
    prev = lambda n, c: (jnp.maximum(n * nb - 1, 0), gidx * n_chunks + c)
    blk, pblk = (tb, LANES), (unit, LANES)
    vmem = (2 * (5 * _nbytes(blk, F32) + 2 * _nbytes(pblk, F32)) + 2 * _nbytes((tb + unit, LANES), F32)
            + 32 * _nbytes((Q_BLOCK, 2 * Q_BLOCK), F32))
    return pl.pallas_call(
        functools.partial(_dil_attn_kernel, span=span, dil=dil, nb=nb),
        grid=(s // tb, n_chunks),
        in_specs=[pl.BlockSpec(blk, cur), pl.BlockSpec(pblk, prev), pl.BlockSpec(blk, cur),
                  pl.BlockSpec(pblk, prev), pl.BlockSpec(blk, cur)],
        out_specs=[pl.BlockSpec(blk, lambda n, c: (n, c)), pl.BlockSpec(blk, lambda n, c: (n, c))],
        out_shape=[jax.ShapeDtypeStruct((s, DIL_OUT_W), F32)] * 2,
        scratch_shapes=[pltpu.VMEM((tb + unit, LANES), F32)] * 2,
        compiler_params=_cparams(("parallel", "parallel"), vmem),
        name=f"dilated_w{win}_d{dil}",
    )(qb, kb, kb, vb, vb)


def _sgu_kernel(uv_ref, g_ref, b_ref, ws_ref, bs_ref, o_ref):
    uv = jax.nn.gelu(uv_ref[...])
    u = uv[:, :SGU_WIDTH]
    v = _layer_norm(uv[:, SGU_WIDTH:], g_ref[...], b_ref[...]).astype(BF16)
    causal = _lane_iota((SGU_CHUNK, SGU_CHUNK)) <= _row_iota((SGU_CHUNK, SGU_CHUNK))
    bs = bs_ref[...]
    for n in range(uv.shape[0] // SGU_CHUNK):
        rs = slice(n * SGU_CHUNK, (n + 1) * SGU_CHUNK)
        for g in range(SGU_GROUPS):
            cs = slice(g * SGU_GROUP_CH, (g + 1) * SGU_GROUP_CH)
            ws = jnp.where(causal, ws_ref[g], 0.0).astype(BF16)
            sv = jnp.dot(ws, v[rs, cs], preferred_element_type=F32) + bs[:, cs]
            o_ref[rs, cs] = (u[rs, cs] * sv).astype(o_ref.dtype)


def _sgu(uv, ln_g, ln_b, w_s, b_s):
    s = uv.shape[0]
    tm = min(4 * SGU_CHUNK, s)
    assert s % tm == 0
    bs = jnp.repeat(b_s.T, SGU_GROUP_CH, axis=1)
    c2 = lambda i: (0, 0)
    vmem = 2 * (_nbytes((tm, 2 * SGU_WIDTH), F32) + _nbytes((tm, SGU_WIDTH), F32)) + 6 * _nbytes((tm, 2 * SGU_WIDTH), F32)
    return pl.pallas_call(
        _sgu_kernel,
        grid=(s // tm,),
        in_specs=[pl.BlockSpec((tm, 2 * SGU_WIDTH), lambda i: (i, 0)), pl.BlockSpec((1, SGU_WIDTH), c2),
                  pl.BlockSpec((1, SGU_WIDTH), c2), pl.BlockSpec(w_s.shape, lambda i: (0, 0, 0)),
                  pl.BlockSpec((SGU_CHUNK, SGU_WIDTH), c2)],
        out_specs=pl.BlockSpec((tm, SGU_WIDTH), lambda i: (i, 0)),
        out_shape=jax.ShapeDtypeStruct((s, SGU_WIDTH), BF16),
        compiler_params=_cparams(("parallel",), vmem),
        name="sgu",
    )(uv, ln_g.reshape(1, -1), ln_b.reshape(1, -1), w_s, bs)


def _split_dot(x, w2):
    hi = x.astype(BF16)
    lo = (x - hi.astype(F32)).astype(BF16)
    return jnp.dot(jnp.concatenate([hi, lo], axis=1), w2, preferred_element_type=F32)


def _mix_kernel(x_ref, oc_ref, os_ref, ow_ref, ga_ref, eg_ref, d0_ref, d1_ref, d2_ref, l0_ref, l1_ref, l2_ref,
                sg_ref, gm_ref, wa_ref, wb_ref, wc_ref, wo_ref, g_ref, b_ref, o_ref, *, alpha):
    d = x_ref.shape[1]
    gates = _split_dot(jax.nn.sigmoid(ga_ref[...]), eg_ref[...])
    o_a = (gates[:, 0:NSA_Q_W] * oc_ref[...].astype(F32) + gates[:, NSA_Q_W:2 * NSA_Q_W] * os_ref[...].astype(F32)
           + gates[:, 2 * NSA_Q_W:3 * NSA_Q_W] * ow_ref[...].astype(F32))
    y_a = jnp.dot(o_a.astype(BF16), wa_ref[...], preferred_element_type=F32)
    l0, l1, l2 = l0_ref[...], l1_ref[...], l2_ref[...]
    lm = jnp.maximum(jnp.maximum(l0, l1), l2)
    e0, e1, e2 = jnp.exp(l0 - lm), jnp.exp(l1 - lm), jnp.exp(l2 - lm)
    den = e0 + e1 + e2
    o_b = (e0 / den) * d0_ref[...] + (e1 / den) * d1_ref[...] + (e2 / den) * d2_ref[...]
    y_b = jnp.dot(o_b.astype(BF16), wb_ref[...], preferred_element_type=F32)
    y_c = jnp.dot(sg_ref[...].astype(BF16), wc_ref[...], preferred_element_type=F32)
    gm = jax.nn.sigmoid(gm_ref[...].astype(F32))
    merged = gm[:, 0:d] * y_a + gm[:, d:2 * d] * y_b + gm[:, 2 * d:3 * d] * y_c
    mix = jnp.dot(merged.astype(BF16), wo_ref[...], preferred_element_type=F32)
    o_ref[...] = _layer_norm(alpha * x_ref[...] + mix, g_ref[...], b_ref[...])


def _mix(x, o_c, o_s, o_w, g_a, dil_o, dil_lse, sgu, g_m, wa, wb, wc, wo, g, b, alpha):
    s, d = x.shape
    tm = min(512, s)
    assert s % tm == 0
    col = np.arange(N_BRANCH * NSA_Q_W)
    head, br = (col % NSA_Q_W) // HEAD_DIM, col // NSA_Q_W
    eg1 = (np.arange(LANES)[:, None] == (head * N_BRANCH + br)[None, :]).astype(np.float32)
    eg = jnp.asarray(np.concatenate([eg1, eg1], axis=0), BF16)
    row = lambda i: (i, 0)
    const = lambda i: (0, 0)
    rspec = lambda w: pl.BlockSpec((tm, w), row)
    cspec = lambda a: pl.BlockSpec(a.shape, const)
    widths = [d, NSA_Q_W, NSA_Q_W, NSA_Q_W, LANES] + [DIL_OUT_W] * 6 + [SGU_WIDTH, N_BRANCH * d]
    vmem = (2 * sum(_nbytes((tm, w), F32) for w in widths) + 2 * _nbytes((tm, d), F32)
            + 2 * sum(_nbytes(a.shape, BF16) for a in (eg, wa, wb, wc, wo)) + 10 * _nbytes((tm, N_BRANCH * d), F32))
    return pl.pallas_call(
        functools.partial(_mix_kernel, alpha=alpha),
        grid=(s // tm,),
        in_specs=[rspec(d), rspec(NSA_Q_W), rspec(NSA_Q_W), rspec(NSA_Q_W), rspec(LANES), cspec(eg)]
                 + [rspec(DIL_OUT_W)] * 6 + [rspec(SGU_WIDTH), rspec(N_BRANCH * d)]
                 + [cspec(wa), cspec(wb), cspec(wc), cspec(wo), pl.BlockSpec((1, d), const), pl.BlockSpec((1, d), const)],
        out_specs=rspec(d),
        out_shape=jax.ShapeDtypeStruct((s, d), F32),
        compiler_params=_cparams(("parallel",), vmem),
        name="mixer_merge_deepnorm",
    )(x, o_c, o_s, o_w, g_a, eg, *dil_o, *dil_lse, sgu, g_m, wa, wb, wc, wo, g.reshape(1, d), b.reshape(1, d))


def _rope_tables(s):
    inv = 1.0 / (ROPE_THETA ** (np.arange(0, HEAD_DIM, 2, dtype=np.float32) / HEAD_DIM)).astype(np.float32)
    ang = (np.arange(s, dtype=np.float32)[:, None] * inv[None, :]).astype(np.float64)
    cos, sin = jnp.asarray(np.cos(ang), F32), jnp.asarray(np.sin(ang), F32)
    reps = LANES // HEAD_DIM
    return jnp.tile(jnp.concatenate([cos, cos], axis=1), (1, reps)), jnp.tile(jnp.concatenate([-sin, sin], axis=1), (1, reps))


def kernel(x, ln_g, ln_b, ffn1_gate, ffn1_up, ffn1_down, ffn2_gate, ffn2_up, ffn2_down, w_in, phi_k_pos, phi_k_w1, phi_k_w2, phi_v_pos, phi_v_w1, phi_v_w2, sgu_ln_g, sgu_ln_b, sgu_w, sgu_b, w_branch_a, w_branch_b, w_branch_c, w_out):
    bsz, s, d = x.shape
    depth = ln_g.shape[0]
    alpha = float((2 * depth) ** 0.25)
    q_scale = float(HEAD_DIM ** -0.5)
    tabs = _rope_tables(s)
    sizes = [NSA_Q_W] + [NSA_KV_W] * 6 + [N_BRANCH * NSA_HEADS, DIL_W, DIL_W, DIL_W, 2 * SGU_WIDTH, N_BRANCH * d]
    offs = np.concatenate([[0], np.cumsum(sizes)]).tolist()
    outs = []
    for bi in range(bsz):
        h = x[bi]
        for l in range(depth):
            (w_qa, w_kc, w_vc, w_ks, w_vs, w_kw, w_vw, w_ga, w_qb, w_kb, w_vb, w_uv, w_gm) = [
                w_in[l][:, offs[i]:offs[i + 1]] for i in range(len(sizes))]
            w_ga = jnp.pad(w_ga, ((0, 0), (0, LANES - w_ga.shape[1])))
            groups = dict(qb=w_qb, kb=w_kb, vb=w_vb, k_rot=jnp.concatenate([w_ks, w_kw], axis=1),
                          kv_c=jnp.concatenate([w_kc, w_vc], axis=1), v_plain=jnp.concatenate([w_vs, w_vw], axis=1),
                          uv=w_uv, gm=w_gm, qa=w_qa, ga=w_ga)
            wp = jnp.concatenate(list(groups.values()), axis=1).astype(BF16)
            ends = np.cumsum([g.shape[1] for g in groups.values()]).tolist()
            cols = {name: (e - g.shape[1], e) for (name, g), e in zip(groups.items(), ends)}

            h, hb = _ffn(h, ffn1_gate[l].astype(BF16), ffn1_up[l].astype(BF16), ffn1_down[l].astype(BF16),
                         ln_g[l, 0], ln_b[l, 0], alpha)

            q_rot, q_u = _mm(hb, wp, cols["qa"], BF16, rope_tabs=tabs, scale=q_scale * LOG2E, plain_scale=q_scale * LOG2E)
            k_rot = _mm(hb, wp, cols["k_rot"], BF16, rope_tabs=tabs)
            qb = _mm(hb, wp, cols["qb"], F32, rope_tabs=tabs, scale=q_scale)
            kb = _mm(hb, wp, cols["kb"], F32, rope_tabs=tabs)
            vb = _mm(hb, wp, cols["vb"], F32)
            kv_c = _mm(hb, wp, cols["kv_c"], F32)
            v_plain = _mm(hb, wp, cols["v_plain"], BF16)
            uv = _mm(hb, wp, cols["uv"], F32)
            g_m = _mm(hb, wp, cols["gm"], BF16)
            g_a = _mm(hb, wp, cols["ga"], F32)

            kc = _compress(kv_c, 0, phi_k_pos[l], phi_k_w1[l], phi_k_w2[l])
            vc = _compress(kv_c, 1, phi_v_pos[l], phi_v_w1[l], phi_v_w2[l])
            o_c, nsel = _cmp_select(q_u, kc, vc)
            o_s, o_w = _sel_attn(q_rot, k_rot[:, :NSA_KV_W], v_plain[:, :NSA_KV_W], nsel,
                                 k_rot[:, NSA_KV_W:], v_plain[:, NSA_KV_W:])
            dil = [_dil_attn(qb, kb, vb, gi, win, dl) for gi, (win, dl) in enumerate(DIL_PAIRS)]
            sg = _sgu(uv, sgu_ln_g[l], sgu_ln_b[l], sgu_w[l], sgu_b[l])
            h = _mix(h, o_c, o_s, o_w, g_a, [o for o, _ in dil], [e for _, e in dil], sg, g_m,
                     w_branch_a[l].astype(BF16), w_branch_b[l].astype(BF16), w_branch_c[l].astype(BF16),
                     w_out[l].astype(BF16), ln_g[l, 1], ln_b[l, 1], alpha)

            h, _ = _ffn(h, ffn2_gate[l].astype(BF16), ffn2_up[l].astype(BF16), ffn2_down[l].astype(BF16),
                        ln_g[l, 2], ln_b[l, 2], alpha)
        outs.append(h)
    return jnp.stack(outs, axis=0)
```

```python
import functools

import numpy as np
import jax
import jax.numpy as jnp
from jax import lax
from jax.experimental import pallas as pl
from jax.experimental.pallas import tpu as pltpu

F32 = jnp.float32
BF16 = jnp.bfloat16

HEAD_DIM = 64
ROPE_THETA = 10000.0
LN_EPS = 1e-5
Q_BLOCK = 128
NSA_HEADS = 8
NSA_GROUPS = 2
NSA_HPG = NSA_HEADS // NSA_GROUPS
NSA_Q_W = NSA_HEADS * HEAD_DIM
NSA_KV_W = NSA_GROUPS * HEAD_DIM
CMP_LEN = 32
CMP_STRIDE = 16
SLC_BLOCK = 64
SLC_RATIO = SLC_BLOCK // CMP_STRIDE
SLC_TOPK = 16
NSA_WINDOW = 512
FORCE_SCORE = 1e6
DIL_PAIRS = ((128, 1), (512, 4), (2048, 16))
DIL_HPG = 4
DIL_HEADS = DIL_HPG * 3
DIL_W = DIL_HEADS * HEAD_DIM
DIL_OUT_W = DIL_HPG * HEAD_DIM
SGU_CHUNK = 128
SGU_GROUPS = 4
SGU_GROUP_CH = 128
SGU_WIDTH = SGU_GROUPS * SGU_GROUP_CH
N_BRANCH = 3

LANES = 128
BF16_SUBLANES = 16
V7X_VMEM_BYTES = 64 * 1024 * 1024
VMEM_CAP = V7X_VMEM_BYTES * 7 // 8
VMEM_FLOOR = V7X_VMEM_BYTES // 4

MM_ROWS = 2048
MM_SUB_ROWS = 512
FFN_ROWS = 1024
MIX_ROWS = 512
SGU_ROWS = 1024
CMP_Q_BLOCK = 256
DIL_STEP_TOKENS = 2048
SEL_TILE = 1024
SEL_VT_ROWS = -(-(HEAD_DIM + 1) // BF16_SUBLANES) * BF16_SUBLANES

M_INIT = -1e30
MASKED = -(2.0 ** 101)
LOG2E = 1.4426950408889634


def _cparams(sem, vmem_bytes):
    return pltpu.CompilerParams(dimension_semantics=sem,
                                vmem_limit_bytes=int(min(max(vmem_bytes, VMEM_FLOOR), VMEM_CAP)))


def _nbytes(shape, dtype):
    return int(np.prod(shape)) * jnp.dtype(dtype).itemsize


def _layer_norm(y, g, b):
    mu = jnp.mean(y, axis=-1, keepdims=True)
    d = y - mu
    var = jnp.mean(d * d, axis=-1, keepdims=True)
    return d * lax.rsqrt(var + LN_EPS) * g + b


def _lane_iota(shape):
    return lax.broadcasted_iota(jnp.int32, shape, len(shape) - 1)


def _row_iota(shape):
    return lax.broadcasted_iota(jnp.int32, shape, 0)


def _ffn_kernel(x_ref, wg_ref, wu_ref, wd_ref, g_ref, b_ref, o_ref, ob_ref, *, alpha, fc):
    n_f = wg_ref.shape[1]
    for r0 in range(0, x_ref.shape[0], MM_SUB_ROWS):
        rs = slice(r0, r0 + MM_SUB_ROWS)
        x = x_ref[rs, :]
        xb = x.astype(BF16)
        acc = jnp.zeros(x.shape, F32)
        for c in range(n_f // fc):
            sl = slice(c * fc, (c + 1) * fc)
            gate = jnp.dot(xb, wg_ref[:, sl], preferred_element_type=F32)
            up = jnp.dot(xb, wu_ref[:, sl], preferred_element_type=F32)
            h = (gate * jax.nn.sigmoid(gate)) * up
            acc = acc + jnp.dot(h.astype(BF16), wd_ref[sl, :], preferred_element_type=F32)
        out = _layer_norm(alpha * x + 0.5 * acc, g_ref[...], b_ref[...])
        o_ref[rs, :] = out
        ob_ref[rs, :] = out.astype(BF16)


def _ffn(x, wg, wu, wd, g, b, alpha):
    s, d = x.shape
    n_f = wg.shape[1]
    tm = min(FFN_ROWS, s)
    fc = 256
    assert s % tm == 0 and tm % MM_SUB_ROWS == 0 and n_f % fc == 0
    const = lambda i: (0, 0)
    row = lambda i: (i, 0)
    once = pl.Buffered(1)
    vmem = (2 * 2 * _nbytes((tm, d), F32) + 2 * _nbytes((tm, d), BF16)
            + 3 * _nbytes((d, n_f), BF16) + 8 * _nbytes((MM_SUB_ROWS, d), F32))
    return pl.pallas_call(
        functools.partial(_ffn_kernel, alpha=alpha, fc=fc),
        grid=(s // tm,),
        in_specs=[pl.BlockSpec((tm, d), row), pl.BlockSpec((d, n_f), const, pipeline_mode=once),
                  pl.BlockSpec((d, n_f), const, pipeline_mode=once),
                  pl.BlockSpec((n_f, d), const, pipeline_mode=once),
                  pl.BlockSpec((1, d), const), pl.BlockSpec((1, d), const)],
        out_specs=[pl.BlockSpec((tm, d), row), pl.BlockSpec((tm, d), row)],
        out_shape=[jax.ShapeDtypeStruct((s, d), F32), jax.ShapeDtypeStruct((s, d), BF16)],
        compiler_params=_cparams(("parallel",), vmem),
        name="ffn_deepnorm",
    )(x, wg, wu, wd, g.reshape(1, d), b.reshape(1, d))


def _swap_halves_64(a):
    half = HEAD_DIM // 2
    first = (_lane_iota(a.shape) % HEAD_DIM) < half
    return jnp.where(first, pltpu.roll(a, LANES - half, 1), pltpu.roll(a, half, 1))


def _mm_kernel(*refs, rope, scale, plain_scale):
    if rope:
        x_ref, w_ref, c_ref, s_ref, o_ref = refs[:5]
    else:
        x_ref, w_ref, o_ref = refs
    for r0 in range(0, x_ref.shape[0], MM_SUB_ROWS):
        rs = slice(r0, r0 + MM_SUB_ROWS)
        acc = jnp.dot(x_ref[rs, :], w_ref[...], preferred_element_type=F32)
        if plain_scale is not None:
            refs[5][rs, :] = (acc * plain_scale).astype(refs[5].dtype)
        if rope:
            cos = c_ref[rs, :]
            sin = s_ref[rs, :]
            for c in range(acc.shape[1] // LANES):
                a = acc[:, c * LANES:(c + 1) * LANES]
                r = a * cos + _swap_halves_64(a) * sin
                if scale != 1.0:
                    r = r * scale
                o_ref[rs, c * LANES:(c + 1) * LANES] = r.astype(o_ref.dtype)
        else:
            if scale != 1.0:
                acc = acc * scale
            o_ref[rs, :] = acc.astype(o_ref.dtype)


def _mm(xb, w, cols, out_dtype, *, rope_tabs=None, scale=1.0, plain_scale=None):
    s, k = xb.shape
    col0, n = cols[0], cols[1] - cols[0]
    tm = min(MM_ROWS, s)
    tn = next(c for c in (1024, 768, 512, 256, LANES) if n % c == 0 and col0 % c == 0)
    assert s % tm == 0 and tm % MM_SUB_ROWS == 0 and (plain_scale is None or rope_tabs is not None)
    jb = col0 // tn
    n_out = 1 if plain_scale is None else 2
    in_specs = [pl.BlockSpec((tm, k), lambda i, j: (i, 0)), pl.BlockSpec((k, tn), lambda i, j: (0, jb + j))]
    args = [xb, w]
    if rope_tabs is not None:
        in_specs += [pl.BlockSpec((tm, LANES), lambda i, j: (i, 0))] * 2
        args += list(rope_tabs)
    vmem = 2 * (_nbytes((tm, k), BF16) + _nbytes((k, tn), BF16) + n_out * _nbytes((tm, tn), out_dtype)
                + 2 * _nbytes((tm, LANES), F32)) + 6 * _nbytes((MM_SUB_ROWS, tn), F32)
    out = pl.pallas_call(
        functools.partial(_mm_kernel, rope=rope_tabs is not None, scale=scale, plain_scale=plain_scale),
        grid=(s // tm, n // tn),
        in_specs=in_specs,
        out_specs=[pl.BlockSpec((tm, tn), lambda i, j: (i, j))] * n_out,
        out_shape=[jax.ShapeDtypeStruct((s, n), out_dtype)] * n_out,
        compiler_params=_cparams(("parallel", "parallel"), vmem),
        name="proj_rope" if rope_tabs is not None else "proj",
    )(*args)
    return out[0] if n_out == 1 else out


def _compress_kernel(x_ref, pos_ref, w1_ref, w2_ref, o_ref):
    n_cmp = o_ref.shape[0]
    hidden = w1_ref.shape[3]
    ha = [jnp.zeros((n_cmp, hidden), F32) for _ in range(NSA_GROUPS)]
    hb = [jnp.zeros((n_cmp, hidden), F32) for _ in range(NSA_GROUPS)]
    tail = [jnp.zeros((8, hidden), F32) for _ in range(NSA_GROUPS)]
    for j in range(CMP_STRIDE):
        xj = x_ref[pl.ds(j, n_cmp, stride=CMP_STRIDE), :]
        pa = pos_ref[j:j + 1, :]
        pb = pos_ref[CMP_STRIDE + j:CMP_STRIDE + j + 1, :]
        xa = (xj + pa).astype(BF16)
        xb = (xj + pb).astype(BF16)
        pb8 = jnp.broadcast_to(pb, (8, pb.shape[1])).astype(BF16)
        for g in range(NSA_GROUPS):
            ha[g] = ha[g] + jnp.dot(xa, w1_ref[g, j], preferred_element_type=F32)
            hb[g] = hb[g] + jnp.dot(xb, w1_ref[g, CMP_STRIDE + j], preferred_element_type=F32)
            tail[g] = tail[g] + jnp.dot(pb8, w1_ref[g, CMP_STRIDE + j], preferred_element_type=F32)
    last = _row_iota((n_cmp, 1)) == n_cmp - 1
    outs = []
    for g in range(NSA_GROUPS):
        hb_next = jnp.where(last, tail[g][0:1, :], pltpu.roll(hb[g], n_cmp - 1, 0))
        h = jax.nn.gelu(ha[g] + hb_next)
        outs.append(jnp.dot(h.astype(BF16), w2_ref[...], preferred_element_type=F32))
    lo = _lane_iota(outs[0].shape) < HEAD_DIM
    o_ref[...] = jnp.where(lo, outs[0], outs[1]).astype(o_ref.dtype)


def _compress(x, col, pos, w1, w2):
    s = x.shape[0]
    n_cmp = s // CMP_STRIDE
    hidden = w1.shape[1]
    pos2 = jnp.concatenate([pos] * NSA_GROUPS, axis=1)
    w1r = w1.reshape(CMP_LEN, HEAD_DIM, hidden).astype(BF16)
    zero = jnp.zeros_like(w1r)
    w1g = jnp.stack([jnp.concatenate([w1r if gg == g else zero for gg in range(NSA_GROUPS)], axis=1)
                     for g in range(NSA_GROUPS)])
    w2d = jnp.concatenate([w2, w2], axis=1).astype(BF16)
    vmem = 2 * (_nbytes((s, NSA_KV_W), F32) + _nbytes(w1g.shape, BF16)) + 16 * _nbytes((n_cmp, hidden), F32)
    return pl.pallas_call(
        _compress_kernel,
        grid=(1,),
        in_specs=[pl.BlockSpec((s, NSA_KV_W), lambda i: (0, col)), pl.BlockSpec(pos2.shape, lambda i: (0, 0)),
                  pl.BlockSpec(w1g.shape, lambda i: (0, 0, 0, 0)), pl.BlockSpec(w2d.shape, lambda i: (0, 0))],
        out_specs=pl.BlockSpec((n_cmp, NSA_KV_W), lambda i: (0, 0)),
        out_shape=jax.ShapeDtypeStruct((n_cmp, NSA_KV_W), BF16),
        compiler_params=_cparams(("arbitrary",), vmem),
        name="nsa_compress",
    )(x, pos2, w1g, w2d)


def _stack_group_queries(q_ref, col0, kv_lane_group):
    parts = []
    for h in range(NSA_HPG):
        c = col0 + (h // 2) * LANES
        qc = q_ref[:, c:c + LANES].astype(F32)
        e = h % 2
        lane = _lane_iota(qc.shape)
        mine = (lane >= e * HEAD_DIM) & (lane < (e + 1) * HEAD_DIM)
        qh = jnp.where(mine, qc, 0.0)
        if e != kv_lane_group:
            qh = pltpu.roll(qh, HEAD_DIM, 1)
        parts.append(qh.astype(BF16))
    return jnp.concatenate(parts, axis=0)


def _unstack_group_outputs(o, kv_lane_group):
    chunks = []
    tq = o.shape[0] // NSA_HPG
    for c in range(NSA_HPG // 2):
        halves = []
        for e in range(2):
            h = 2 * c + e
            oh = o[h * tq:(h + 1) * tq, :]
            if e != kv_lane_group:
                oh = pltpu.roll(oh, HEAD_DIM, 1)
            halves.append(oh)
        lo = _lane_iota(halves[0].shape) < HEAD_DIM
        chunks.append(jnp.where(lo, halves[0], halves[1]))
    return jnp.concatenate(chunks, axis=1)


def _qk(q, k):
    return lax.dot_general(q, k, (((1,), (1,)), ((), ())), preferred_element_type=F32)


def _cmp_select_kernel(q_ref, kc_ref, vc_ref, mmap_ref, oc_ref, nsel_ref, *, top_k, n_cls):
    b = pl.program_id(0)
    n_cmp = kc_ref.shape[0]
    n_slc = mmap_ref.shape[0]
    tq = q_ref.shape[0]
    gw = NSA_HPG * HEAD_DIM
    t_row = b * tq + _row_iota((tq, 1))
    t = b * tq + _lane_iota((1, tq))

    def variant(nc, nb):
        valid = _lane_iota((1, nc)) * CMP_STRIDE + (CMP_LEN - 1) <= t_row
        bias = jnp.concatenate([jnp.where(valid, 0.0, MASKED)] * NSA_HPG, axis=0)
        nbe = min(nb, nc // SLC_RATIO)
        blk = _row_iota((nbe, 1))
        cur = t // SLC_BLOCK
        forced = (blk == 0) | (blk == cur) | (blk == cur - 1)
        future = blk * SLC_BLOCK > t
        blk_f = blk.astype(F32)
        kc = kc_ref[0:nc, :]
        vc = vc_ref[0:nc, :]
        mmap_t = mmap_ref[0:nbe, 0:nc]
        for g in range(NSA_GROUPS):
            qst = _stack_group_queries(q_ref, g * gw, g)
            s = _qk(qst, kc) + bias
            m = jnp.maximum(jnp.max(s, axis=-1, keepdims=True), M_INIT)
            e = jnp.exp2(s - m)
            den = jnp.sum(e, axis=-1, keepdims=True)
            p = e / jnp.maximum(den, 1e-30)
            o = jnp.dot(p.astype(BF16), vc, preferred_element_type=F32)
            oc_ref[:, g * gw:(g + 1) * gw] = _unstack_group_outputs(o, g).astype(oc_ref.dtype)
            psum = p[0:tq]
            for h in range(1, NSA_HPG):
                psum = psum + p[h * tq:(h + 1) * tq]
            hi = psum.astype(BF16)
            r1 = psum - hi.astype(F32)
            mid = r1.astype(BF16)
            low = (r1 - mid.astype(F32)).astype(BF16)
            p_slc = _qk(mmap_t, hi) + _qk(mmap_t, mid) + _qk(mmap_t, low)
            score = jnp.where(forced, FORCE_SCORE, jnp.where(future, -1.0, p_slc))
            for _ in range(top_k):
                mx = jnp.max(score, axis=0, keepdims=True)
                first = jnp.min(jnp.where(score == mx, blk_f, float(nbe)), axis=0, keepdims=True)
                score = jnp.where(blk_f == first, -2.0, score)
            not_picked = jnp.where(score == -2.0, 0.0, 1.0)
            if nbe < nb:
                not_picked = jnp.concatenate([not_picked, jnp.ones((nb - nbe, tq), F32)], axis=0)
            nsel_ref[:, g * n_slc:g * n_slc + nb] = not_picked.T.astype(nsel_ref.dtype)
            if nb < n_slc:
                nsel_ref[:, g * n_slc + nb:(g + 1) * n_slc] = jnp.ones((tq, n_slc - nb), nsel_ref.dtype)

    for c in range(n_cls):
        nc = (c + 1) * (n_cmp // n_cls)
        nb = min(n_slc, -(-(nc // SLC_RATIO) // LANES) * LANES)
        lo_b = c * (nc // (c + 1)) // (tq // CMP_STRIDE)
        hi_b = nc // (tq // CMP_STRIDE)
        pl.when((b >= lo_b) & (b < hi_b))(functools.partial(variant, nc, nb))


def _cmp_select(q_u, kc, vc):
    s = q_u.shape[0]
    n_cmp, n_slc = s // CMP_STRIDE, s // SLC_BLOCK
    top_k = min(SLC_TOPK, n_slc)
    n_cls = max(n_cmp // (2 * LANES), 1)
    tq = min(CMP_Q_BLOCK, s)
    assert s % tq == 0 and n_cmp % n_cls == 0 and (n_cmp // n_cls) % (tq // CMP_STRIDE) == 0
    m = np.arange(n_cmp)[:, None]
    j = np.arange(n_slc)[None, :]
    mmap = jnp.asarray(((m >= SLC_RATIO * j - 1) & (m <= SLC_RATIO * j + SLC_RATIO - 1)).astype(np.float32).T, BF16)
    const = lambda b: (0, 0)
    rows = NSA_HPG * tq
    vmem = 2 * (2 * _nbytes((n_cmp, NSA_KV_W), BF16) + _nbytes((n_cmp, n_slc), BF16)
                + _nbytes((tq, NSA_Q_W), F32) * 2) + 6 * _nbytes((rows, n_cmp), F32)
    return pl.pallas_call(
        functools.partial(_cmp_select_kernel, top_k=top_k, n_cls=n_cls),
        grid=(s // tq,),
        in_specs=[pl.BlockSpec((tq, NSA_Q_W), lambda b: (b, 0)), pl.BlockSpec((n_cmp, NSA_KV_W), const),
                  pl.BlockSpec((n_cmp, NSA_KV_W), const), pl.BlockSpec((n_slc, n_cmp), const)],
        out_specs=[pl.BlockSpec((tq, NSA_Q_W), lambda b: (b, 0)),
                   pl.BlockSpec((tq, NSA_GROUPS * n_slc), lambda b: (b, 0))],
        out_shape=[jax.ShapeDtypeStruct((s, NSA_Q_W), BF16), jax.ShapeDtypeStruct((s, NSA_GROUPS * n_slc), BF16)],
        compiler_params=_cparams(("parallel",), vmem),
        name="nsa_compressed_select",
    )(q_u, kc, vc, mmap)


def _sel_attn_kernel(q_ref, k_ref, vt_ref, nsel_ref, kw_ref, vwt_ref, o_ref, ow_ref, qa_ref, m_ref, acc_ref, s_ref,
                     *, win, w_span):
    b = pl.program_id(0)
    n_slc = nsel_ref.shape[1] // NSA_GROUPS
    n_win = n_slc // win
    tiles_per_win = win * SLC_BLOCK // SEL_TILE
    gw = NSA_HPG * HEAD_DIM
    last = (b * Q_BLOCK) // SEL_TILE
    t_lane = b * Q_BLOCK + _lane_iota((1, Q_BLOCK))
    lane = _lane_iota((Q_BLOCK, LANES))
    head_lanes = lane < HEAD_DIM
    flag_lanes = (lane >= HEAD_DIM) & (lane < HEAD_DIM + win)

    nsel_all = nsel_ref[...].astype(F32)
    if nsel_all.shape[1] < LANES:
        nsel_all = jnp.concatenate([nsel_all, jnp.zeros((Q_BLOCK, LANES - nsel_all.shape[1]), F32)], axis=1)
    for g in range(NSA_GROUPS):
        heads = []
        for h in range(NSA_HPG):
            c = g * gw + (h // 2) * LANES
            qc = q_ref[:, c:c + LANES].astype(F32)
            heads.append(qc if h % 2 == 0 else pltpu.roll(qc, HEAD_DIM, 1))
        for w in range(n_win):
            a = g * n_slc + w * win
            fl = nsel_all[:, (a // LANES) * LANES:(a // LANES + 1) * LANES]
            shift = (HEAD_DIM - a % LANES) % LANES
            if shift:
                fl = pltpu.roll(fl, shift, 1)
            fl = jnp.where(flag_lanes, fl, 0.0)
            for h in range(NSA_HPG):
                qa_ref[g, w, h * Q_BLOCK:(h + 1) * Q_BLOCK, :] = jnp.where(head_lanes, heads[h], fl).astype(BF16)
    m_ref[...] = jnp.full(m_ref.shape, M_INIT, F32)
    acc_ref[...] = jnp.zeros(acc_ref.shape, F32)

    def heads_to_lanes(o_t, out_ref, g):
        o_t = jnp.concatenate([o_t, jnp.zeros((LANES - o_t.shape[0], o_t.shape[1]), F32)], axis=0)
        heads = [o_t[:, h * Q_BLOCK:(h + 1) * Q_BLOCK].T for h in range(NSA_HPG)]
        for c in range(NSA_HPG // 2):
            even, odd = heads[2 * c], pltpu.roll(heads[2 * c + 1], HEAD_DIM, 1)
            out_ref[:, g * gw + c * LANES:g * gw + (c + 1) * LANES] = jnp.where(head_lanes, even, odd).astype(out_ref.dtype)

    n_back = NSA_WINDOW // Q_BLOCK
    blk0 = jnp.maximum(b - n_back, 0)
    start = pl.multiple_of(blk0 * Q_BLOCK, Q_BLOCK)
    diff = t_lane - (start + _row_iota((w_span, 1)))
    w_bias = jnp.where((diff >= 0) & (diff < NSA_WINDOW), 0.0, MASKED)
    w_bias = jnp.concatenate([w_bias] * NSA_HPG, axis=1)
    for g in range(NSA_GROUPS):
        s = _qk(kw_ref[g, pl.ds(start, w_span), :], qa_ref[g, 0]) + w_bias
        m = jnp.maximum(jnp.max(s, axis=0, keepdims=True), M_INIT)
        p = jnp.exp2(s - m).astype(BF16)
        vt = jnp.concatenate([vwt_ref[g, blk0 + i] for i in range(w_span // Q_BLOCK)], axis=1)
        acc = jnp.dot(vt, p, preferred_element_type=F32)
        heads_to_lanes(acc / jnp.maximum(acc[HEAD_DIM:HEAD_DIM + 1, :], 1e-30), ow_ref, g)

    def scores(kt, slot):
        off = pl.multiple_of(kt * SEL_TILE, SEL_TILE)
        for g in range(NSA_GROUPS):
            s_ref[g, slot] = _qk(k_ref[g, pl.ds(off, SEL_TILE), :], qa_ref[g, kt // tiles_per_win])

    def update(kt, slot, causal):
        off = pl.multiple_of(kt * SEL_TILE, SEL_TILE)
        for g in range(NSA_GROUPS):
            s = s_ref[g, slot]
            if causal:
                keep = off + _row_iota((SEL_TILE, 1)) <= t_lane
                s = jnp.where(jnp.concatenate([keep] * NSA_HPG, axis=1), s, MASKED)
            m_old = m_ref[g]
            m_new = jnp.maximum(m_old, jnp.max(s, axis=0, keepdims=True))
            alpha = jnp.exp2(m_old - m_new)
            p = jnp.exp2(s - m_new[0:1, :]).astype(BF16)
            pv = jnp.dot(vt_ref[g, kt], p, preferred_element_type=F32)
            acc_ref[g] = alpha[0:1, :] * acc_ref[g] + pv
            m_ref[g] = m_new

    def finish():
        for g in range(NSA_GROUPS):
            acc = acc_ref[g]
            heads_to_lanes(acc / jnp.maximum(acc[HEAD_DIM:HEAD_DIM + 1, :], 1e-30), o_ref, g)

    scores(0, 0)

    def pair(j, carry):
        scores(2 * j + 1, 1)
        update(2 * j, 0, False)
        scores(2 * j + 2, 0)
        update(2 * j + 1, 1, False)
        return carry

    lax.fori_loop(0, last // 2, pair, 0)

    @pl.when(last % 2 == 1)
    def _():
        scores(last, 1)
        update(last - 1, 0, False)
        update(last, 1, True)
        finish()

    @pl.when(last % 2 == 0)
    def _():
        update(last, 0, True)
        finish()


def _sel_attn(rot_a, k_s, v_s, nsel, k_w, v_w):
    s = rot_a.shape[0]
    n_slc = s // SLC_BLOCK
    win = min(n_slc, LANES - HEAD_DIM)
    assert s % SEL_TILE == 0 and n_slc % win == 0 and (win * SLC_BLOCK) % SEL_TILE == 0
    n_win = n_slc // win
    spare = LANES - HEAD_DIM
    w_span = min(NSA_WINDOW + Q_BLOCK, s)
    blk = (np.arange(s) // SLC_BLOCK) % win
    marks = jnp.asarray(np.where(np.arange(spare)[None, :] == blk[:, None], MASKED, 0.0).astype(np.float32), BF16)
    grp = lambda a, g: a[:, g * HEAD_DIM:(g + 1) * HEAD_DIM]
    k_aug = jnp.stack([jnp.concatenate([grp(k_s, g), marks], axis=1) for g in range(NSA_GROUPS)])
    kw_aug = jnp.stack([jnp.concatenate([grp(k_w, g), jnp.zeros((s, spare), BF16)], axis=1) for g in range(NSA_GROUPS)])

    def transposed_tiles(v, tile):
        vt = _ones_augmented_values(v).reshape(NSA_GROUPS, s // tile, tile, LANES).transpose(0, 1, 3, 2)
        return vt[:, :, :SEL_VT_ROWS, :]

    vt_aug = transposed_tiles(v_s, SEL_TILE)
    vwt_aug = transposed_tiles(v_w, Q_BLOCK)
    rows = NSA_HPG * Q_BLOCK
    once = pl.Buffered(1)
    vmem = (2 * _nbytes(k_aug.shape, BF16) + _nbytes(vt_aug.shape, BF16) + _nbytes(vwt_aug.shape, BF16)
            + 2 * (2 * _nbytes((Q_BLOCK, NSA_Q_W), BF16) + _nbytes((Q_BLOCK, NSA_GROUPS * n_slc), BF16)
                   + _nbytes((Q_BLOCK, NSA_Q_W), BF16))
            + NSA_GROUPS * (n_win * _nbytes((rows, LANES), BF16) + _nbytes((8 + SEL_VT_ROWS, rows), F32)
                            + 2 * _nbytes((SEL_TILE, rows), F32))
            + 4 * _nbytes((SEL_TILE, rows), F32))
    return pl.pallas_call(
        functools.partial(_sel_attn_kernel, win=win, w_span=w_span),
        grid=(s // Q_BLOCK,),
        in_specs=[pl.BlockSpec((Q_BLOCK, NSA_Q_W), lambda b: (b, 0)),
                  pl.BlockSpec(k_aug.shape, lambda b: (0, 0, 0), pipeline_mode=once),
                  pl.BlockSpec(vt_aug.shape, lambda b: (0, 0, 0, 0), pipeline_mode=once),
                  pl.BlockSpec((Q_BLOCK, NSA_GROUPS * n_slc), lambda b: (b, 0)),
                  pl.BlockSpec(kw_aug.shape, lambda b: (0, 0, 0), pipeline_mode=once),
                  pl.BlockSpec(vwt_aug.shape, lambda b: (0, 0, 0, 0), pipeline_mode=once)],
        out_specs=[pl.BlockSpec((Q_BLOCK, NSA_Q_W), lambda b: (b, 0))] * 2,
        out_shape=[jax.ShapeDtypeStruct((s, NSA_Q_W), BF16)] * 2,
        scratch_shapes=[pltpu.VMEM((NSA_GROUPS, n_win, rows, LANES), BF16),
                        pltpu.VMEM((NSA_GROUPS, 8, rows), F32), pltpu.VMEM((NSA_GROUPS, SEL_VT_ROWS, rows), F32),
                        pltpu.VMEM((NSA_GROUPS, 2, SEL_TILE, rows), F32)],
        compiler_params=_cparams(("parallel",), vmem),
        name="nsa_selected_window",
    )(rot_a, k_aug, vt_aug, nsel, kw_aug, vwt_aug)


def _ones_augmented_values(v):
    s = v.shape[0]
    spare = LANES - HEAD_DIM
    ones = jnp.asarray((np.arange(spare)[None, :] == 0).astype(np.float32) * np.ones((s, 1), np.float32), BF16)
    return jnp.stack([jnp.concatenate([v[:, g * HEAD_DIM:(g + 1) * HEAD_DIM], ones], axis=1)
                      for g in range(NSA_GROUPS)])


def _dil_attn_kernel(q_ref, kp_ref, kc_ref, vp_ref, vc_ref, o_ref, lse_ref, kbuf, vbuf, *, span, dil, nb):
    n = pl.program_id(0)
    unit = dil * Q_BLOCK
    kbuf[0:unit, :] = kp_ref[...]
    kbuf[unit:, :] = kc_ref[...]
    vbuf[0:unit, :] = vp_ref[...]
    vbuf[unit:, :] = vc_ref[...]
    qi = _row_iota((Q_BLOCK, 1)) + Q_BLOCK
    ki = _lane_iota((1, 2 * Q_BLOCK))
    delta = qi - ki
    band_bias = jnp.where((delta >= 0) & (delta <= span), 0.0, MASKED)
    lo = _lane_iota((Q_BLOCK, LANES)) < HEAD_DIM

    def item(idx, carry):
        j = idx // dil
        base = j * unit + idx % dil
        if dil > 1:
            q_rows, kv_rows = pl.ds(base, Q_BLOCK, stride=dil), pl.ds(base, 2 * Q_BLOCK, stride=dil)
        else:
            base = pl.multiple_of(base, Q_BLOCK)
            q_rows, kv_rows = pl.ds(base, Q_BLOCK), pl.ds(base, 2 * Q_BLOCK)
        bias = jnp.where((n * nb + j == 0) & (ki < Q_BLOCK), MASKED, band_bias)
        q = q_ref[q_rows, :]
        k = kbuf[kv_rows, :].astype(BF16)
        v = vbuf[kv_rows, :].astype(BF16)
        outs, lses = [], []
        for e in range(2):
            qh = jnp.where(lo if e == 0 else ~lo, q, 0.0).astype(BF16)
            s = _qk(qh, k) + bias
            m = jnp.maximum(jnp.max(s, axis=-1, keepdims=True), M_INIT)
            ex = jnp.exp(s - m)
            den = jnp.maximum(jnp.sum(ex, axis=-1, keepdims=True), 1e-30)
            outs.append(jnp.dot(ex.astype(BF16), v, preferred_element_type=F32) / den)
            lses.append(jnp.broadcast_to(m + jnp.log(den), (Q_BLOCK, LANES)))
        o_ref[q_rows, :] = jnp.where(lo, outs[0], outs[1])
        lse_ref[q_rows, :] = jnp.where(lo, lses[0], lses[1])
        return carry

    lax.fori_loop(0, nb * dil, item, 0, unroll=8)


def _dil_attn(qb, kb, vb, gidx, win, dil):
    s = qb.shape[0]
    unit = dil * Q_BLOCK
    tb = min(max(DIL_STEP_TOKENS, unit), s)
    assert s % tb == 0 and tb % unit == 0
    nb = tb // unit
    span = win // dil
    n_chunks = DIL_OUT_W // LANES
    cur = lambda n, c: (n, gidx * n_chunks + c)
    prev = lambda n, c: (jnp.maximum(n * nb - 1, 0), gidx * n_chunks + c)
    blk, pblk = (tb, LANES), (unit, LANES)
    vmem = (2 * (5 * _nbytes(blk, F32) + 2 * _nbytes(pblk, F32)) + 2 * _nbytes((tb + unit, LANES), F32)
            + 32 * _nbytes((Q_BLOCK, 2 * Q_BLOCK), F32))
    return pl.pallas_call(
        functools.partial(_dil_attn_kernel, span=span, dil=dil, nb=nb),
        grid=(s // tb, n_chunks),
        in_specs=[pl.BlockSpec(blk, cur), pl.BlockSpec(pblk, prev), pl.BlockSpec(blk, cur),
                  pl.BlockSpec(pblk, prev), pl.BlockSpec(blk, cur)],
        out_specs=[pl.BlockSpec(blk, lambda n, c: (n, c)), pl.BlockSpec(blk, lambda n, c: (n, c))],
        out_shape=[jax.ShapeDtypeStruct((s, DIL_OUT_W), F32)] * 2,
        scratch_shapes=[pltpu.VMEM((tb + unit, LANES), F32)] * 2,
        compiler_params=_cparams(("parallel", "parallel"), vmem),
        name=f"dilated_w{win}_d{dil}",
    )(qb, kb, kb, vb, vb)


def _sgu_kernel(uv_ref, g_ref, b_ref, ws_ref, bs_ref, o_ref):
    uv = jax.nn.gelu(uv_ref[...])
    u = uv[:, :SGU_WIDTH]
    v = _layer_norm(uv[:, SGU_WIDTH:], g_ref[...], b_ref[...]).astype(BF16)
    causal = _lane_iota((SGU_CHUNK, SGU_CHUNK)) <= _row_iota((SGU_CHUNK, SGU_CHUNK))
    bs = bs_ref[...]
    for n in range(uv.shape[0] // SGU_CHUNK):
        rs = slice(n * SGU_CHUNK, (n + 1) * SGU_CHUNK)
        for g in range(SGU_GROUPS):
            cs = slice(g * SGU_GROUP_CH, (g + 1) * SGU_GROUP_CH)
            ws = jnp.where(causal, ws_ref[g], 0.0).astype(BF16)
            sv = jnp.dot(ws, v[rs, cs], preferred_element_type=F32) + bs[:, cs]
            o_ref[rs, cs] = (u[rs, cs] * sv).astype(o_ref.dtype)


def _sgu(uv, ln_g, ln_b, w_s, b_s):
    s = uv.shape[0]
    tm = min(SGU_ROWS, s)
    assert s % tm == 0
    bs = jnp.repeat(b_s.T, SGU_GROUP_CH, axis=1)
    c2 = lambda i: (0, 0)
    vmem = 2 * (_nbytes((tm, 2 * SGU_WIDTH), F32) + _nbytes((tm, SGU_WIDTH), F32)) + 6 * _nbytes((tm, 2 * SGU_WIDTH), F32)
    return pl.pallas_call(
        _sgu_kernel,
        grid=(s // tm,),
        in_specs=[pl.BlockSpec((tm, 2 * SGU_WIDTH), lambda i: (i, 0)), pl.BlockSpec((1, SGU_WIDTH), c2),
                  pl.BlockSpec((1, SGU_WIDTH), c2), pl.BlockSpec(w_s.shape, lambda i: (0, 0, 0)),
                  pl.BlockSpec((SGU_CHUNK, SGU_WIDTH), c2)],
        out_specs=pl.BlockSpec((tm, SGU_WIDTH), lambda i: (i, 0)),
        out_shape=jax.ShapeDtypeStruct((s, SGU_WIDTH), BF16),
        compiler_params=_cparams(("parallel",), vmem),
        name="sgu",
    )(uv, ln_g.reshape(1, -1), ln_b.reshape(1, -1), w_s, bs)


def _split_dot(x, w2):
    hi = x.astype(BF16)
    lo = (x - hi.astype(F32)).astype(BF16)
    return jnp.dot(jnp.concatenate([hi, lo], axis=1), w2, preferred_element_type=F32)


def _mix_kernel(x_ref, oc_ref, os_ref, ow_ref, ga_ref, eg_ref, d0_ref, d1_ref, d2_ref, l0_ref, l1_ref, l2_ref,
                sg_ref, gm_ref, wa_ref, wb_ref, wc_ref, wo_ref, g_ref, b_ref, o_ref, *, alpha):
    d = x_ref.shape[1]
    gates = _split_dot(jax.nn.sigmoid(ga_ref[...]), eg_ref[...])
    o_a = (gates[:, 0:NSA_Q_W] * oc_ref[...].astype(F32) + gates[:, NSA_Q_W:2 * NSA_Q_W] * os_ref[...].astype(F32)
           + gates[:, 2 * NSA_Q_W:3 * NSA_Q_W] * ow_ref[...].astype(F32))
    y_a = jnp.dot(o_a.astype(BF16), wa_ref[...], preferred_element_type=F32)
    l0, l1, l2 = l0_ref[...], l1_ref[...], l2_ref[...]
    lm = jnp.maximum(jnp.maximum(l0, l1), l2)
    e0, e1, e2 = jnp.exp(l0 - lm), jnp.exp(l1 - lm), jnp.exp(l2 - lm)
    den = e0 + e1 + e2
    o_b = (e0 / den) * d0_ref[...] + (e1 / den) * d1_ref[...] + (e2 / den) * d2_ref[...]
    y_b = jnp.dot(o_b.astype(BF16), wb_ref[...], preferred_element_type=F32)
    y_c = jnp.dot(sg_ref[...].astype(BF16), wc_ref[...], preferred_element_type=F32)
    gm = jax.nn.sigmoid(gm_ref[...].astype(F32))
    merged = gm[:, 0:d] * y_a + gm[:, d:2 * d] * y_b + gm[:, 2 * d:3 * d] * y_c
    mix = jnp.dot(merged.astype(BF16), wo_ref[...], preferred_element_type=F32)
    o_ref[...] = _layer_norm(alpha * x_ref[...] + mix, g_ref[...], b_ref[...])


def _mix(x, o_c, o_s, o_w, g_a, dil_o, dil_lse, sgu, g_m, wa, wb, wc, wo, g, b, alpha):
    s, d = x.shape
    tm = min(MIX_ROWS, s)
    assert s % tm == 0
    col = np.arange(N_BRANCH * NSA_Q_W)
    head, br = (col % NSA_Q_W) // HEAD_DIM, col // NSA_Q_W
    eg1 = (np.arange(LANES)[:, None] == (head * N_BRANCH + br)[None, :]).astype(np.float32)
    eg = jnp.asarray(np.concatenate([eg1, eg1], axis=0), BF16)
    row = lambda i: (i, 0)
    const = lambda i: (0, 0)
    rspec = lambda w: pl.BlockSpec((tm, w), row)
    cspec = lambda a: pl.BlockSpec(a.shape, const)
    widths = [d, NSA_Q_W, NSA_Q_W, NSA_Q_W, LANES] + [DIL_OUT_W] * 6 + [SGU_WIDTH, N_BRANCH * d]
    vmem = (2 * sum(_nbytes((tm, w), F32) for w in widths) + 2 * _nbytes((tm, d), F32)
            + 2 * sum(_nbytes(a.shape, BF16) for a in (eg, wa, wb, wc, wo)) + 10 * _nbytes((tm, N_BRANCH * d), F32))
    return pl.pallas_call(
        functools.partial(_mix_kernel, alpha=alpha),
        grid=(s // tm,),
        in_specs=[rspec(d), rspec(NSA_Q_W), rspec(NSA_Q_W), rspec(NSA_Q_W), rspec(LANES), cspec(eg)]
                 + [rspec(DIL_OUT_W)] * 6 + [rspec(SGU_WIDTH), rspec(N_BRANCH * d)]
                 + [cspec(wa), cspec(wb), cspec(wc), cspec(wo), pl.BlockSpec((1, d), const), pl.BlockSpec((1, d), const)],
        out_specs=rspec(d),
        out_shape=jax.ShapeDtypeStruct((s, d), F32),
        compiler_params=_cparams(("parallel",), vmem),
        name="mixer_merge_deepnorm",
    )(x, o_c, o_s, o_w, g_a, eg, *dil_o, *dil_lse, sgu, g_m, wa, wb, wc, wo, g.reshape(1, d), b.reshape(1, d))


def _rope_tables(s):
    inv = 1.0 / (ROPE_THETA ** (np.arange(0, HEAD_DIM, 2, dtype=np.float32) / HEAD_DIM)).astype(np.float32)
    ang = (np.arange(s, dtype=np.float32)[:, None] * inv[None, :]).astype(np.float64)
    cos, sin = jnp.asarray(np.cos(ang), F32), jnp.asarray(np.sin(ang), F32)
    reps = LANES // HEAD_DIM
    return jnp.tile(jnp.concatenate([cos, cos], axis=1), (1, reps)), jnp.tile(jnp.concatenate([-sin, sin], axis=1), (1, reps))


def kernel(x, ln_g, ln_b, ffn1_gate, ffn1_up, ffn1_down, ffn2_gate, ffn2_up, ffn2_down, w_in, phi_k_pos, phi_k_w1, phi_k_w2, phi_v_pos, phi_v_w1, phi_v_w2, sgu_ln_g, sgu_ln_b, sgu_w, sgu_b, w_branch_a, w_branch_b, w_branch_c, w_out):
    bsz, s, d = x.shape
    depth = ln_g.shape[0]
    alpha = float((2 * depth) ** 0.25)
    q_scale = float(HEAD_DIM ** -0.5)
    tabs = _rope_tables(s)
    sizes = [NSA_Q_W] + [NSA_KV_W] * 6 + [N_BRANCH * NSA_HEADS, DIL_W, DIL_W, DIL_W, 2 * SGU_WIDTH, N_BRANCH * d]
    offs = np.concatenate([[0], np.cumsum(sizes)]).tolist()
    outs = []
    for bi in range(bsz):
        h = x[bi]
        for l in range(depth):
            (w_qa, w_kc, w_vc, w_ks, w_vs, w_kw, w_vw, w_ga, w_qb, w_kb, w_vb, w_uv, w_gm) = [
                w_in[l][:, offs[i]:offs[i + 1]] for i in range(len(sizes))]
            w_ga = jnp.pad(w_ga, ((0, 0), (0, LANES - w_ga.shape[1])))
            groups = dict(qb=w_qb, kb=w_kb, vb=w_vb, k_rot=jnp.concatenate([w_ks, w_kw], axis=1),
                          kv_c=jnp.concatenate([w_kc, w_vc], axis=1), v_plain=jnp.concatenate([w_vs, w_vw], axis=1),
                          uv=w_uv, gm=w_gm, qa=w_qa, ga=w_ga)
            wp = jnp.concatenate(list(groups.values()), axis=1).astype(BF16)
            ends = np.cumsum([g.shape[1] for g in groups.values()]).tolist()
            cols = {name: (e - g.shape[1], e) for (name, g), e in zip(groups.items(), ends)}

            h, hb = _ffn(h, ffn1_gate[l].astype(BF16), ffn1_up[l].astype(BF16), ffn1_down[l].astype(BF16),
                         ln_g[l, 0], ln_b[l, 0], alpha)

            q_rot, q_u = _mm(hb, wp, cols["qa"], BF16, rope_tabs=tabs, scale=q_scale * LOG2E, plain_scale=q_scale * LOG2E)
            k_rot = _mm(hb, wp, cols["k_rot"], BF16, rope_tabs=tabs)
            qb = _mm(hb, wp, cols["qb"], F32, rope_tabs=tabs, scale=q_scale)
            kb = _mm(hb, wp, cols["kb"], F32, rope_tabs=tabs)
            vb = _mm(hb, wp, cols["vb"], F32)
            kv_c = _mm(hb, wp, cols["kv_c"], F32)
            v_plain = _mm(hb, wp, cols["v_plain"], BF16)
            uv = _mm(hb, wp, cols["uv"], F32)
            g_m = _mm(hb, wp, cols["gm"], BF16)
            g_a = _mm(hb, wp, cols["ga"], F32)

            kc = _compress(kv_c, 0, phi_k_pos[l], phi_k_w1[l], phi_k_w2[l])
            vc = _compress(kv_c, 1, phi_v_pos[l], phi_v_w1[l], phi_v_w2[l])
            o_c, nsel = _cmp_select(q_u, kc, vc)
            o_s, o_w = _sel_attn(q_rot, k_rot[:, :NSA_KV_W], v_plain[:, :NSA_KV_W], nsel,
                                 k_rot[:, NSA_KV_W:], v_plain[:, NSA_KV_W:])
            dil = [_dil_attn(qb, kb, vb, gi, win, dl) for gi, (win, dl) in enumerate(DIL_PAIRS)]
            sg = _sgu(uv, sgu_ln_g[l], sgu_ln_b[l], sgu_w[l], sgu_b[l])
            h = _mix(h, o_c, o_s, o_w, g_a, [o for o, _ in dil], [e for _, e in dil], sg, g_m,
                     w_branch_a[l].astype(BF16), w_branch_b[l].astype(BF16), w_branch_c[l].astype(BF16),
                     w_out[l].astype(BF16), ln_g[l, 1], ln_b[l, 1], alpha)

            h, _ = _ffn(h, ffn2_gate[l].astype(BF16), ffn2_up[l].astype(BF16), ffn2_down[l].astype(BF16),
                        ln_g[l, 2], ln_b[l, 2], alpha)
        outs.append(h)
    return jnp.stack(outs, axis=0)
```

```python
import functools

import numpy as np
import jax
import jax.numpy as jnp
from jax import lax
from jax.experimental import pallas as pl
from jax.experimental.pallas import tpu as pltpu

F32 = jnp.float32
BF16 = jnp.bfloat16

HEAD_DIM = 64
ROPE_THETA = 10000.0
LN_EPS = 1e-5
Q_BLOCK = 128
NSA_HEADS = 8
NSA_GROUPS = 2
NSA_HPG = NSA_HEADS // NSA_GROUPS
NSA_Q_W = NSA_HEADS * HEAD_DIM
NSA_KV_W = NSA_GROUPS * HEAD_DIM
CMP_LEN = 32
CMP_STRIDE = 16
SLC_BLOCK = 64
SLC_RATIO = SLC_BLOCK // CMP_STRIDE
SLC_TOPK = 16
NSA_WINDOW = 512
N_FORCED = 3
DIL_PAIRS = ((128, 1), (512, 4), (2048, 16))
DIL_HPG = 4
DIL_HEADS = DIL_HPG * 3
DIL_W = DIL_HEADS * HEAD_DIM
DIL_OUT_W = DIL_HPG * HEAD_DIM
SGU_CHUNK = 128
SGU_GROUPS = 4
SGU_GROUP_CH = 128
SGU_WIDTH = SGU_GROUPS * SGU_GROUP_CH
N_BRANCH = 3

LANES = 128
BF16_SUBLANES = 16
V7X_VMEM_BYTES = 64 * 1024 * 1024
VMEM_CAP = V7X_VMEM_BYTES * 7 // 8
VMEM_FLOOR = V7X_VMEM_BYTES // 4

MM_ROWS = 2048
MM_SUB_ROWS = 512
FFN_ROWS = 1024
MIX_ROWS = 512
SGU_ROWS = 1024
CMP_Q_BLOCK = 256
DIL_STEP_TOKENS = 4096
SEL_TILE = 1024
SEL_VT_ROWS = -(-(HEAD_DIM + 1) // BF16_SUBLANES) * BF16_SUBLANES

M_INIT = -1e30
MASKED = -(2.0 ** 101)
LOG2E = 1.4426950408889634


def _cparams(sem, vmem_bytes):
    return pltpu.CompilerParams(dimension_semantics=sem,
                                vmem_limit_bytes=int(min(max(vmem_bytes, VMEM_FLOOR), VMEM_CAP)))


def _nbytes(shape, dtype):
    return int(np.prod(shape)) * jnp.dtype(dtype).itemsize


def _layer_norm(y, g, b):
    mu = jnp.mean(y, axis=-1, keepdims=True)
    d = y - mu
    var = jnp.mean(d * d, axis=-1, keepdims=True)
    return d * lax.rsqrt(var + LN_EPS) * g + b


def _lane_iota(shape):
    return lax.broadcasted_iota(jnp.int32, shape, len(shape) - 1)


def _row_iota(shape):
    return lax.broadcasted_iota(jnp.int32, shape, 0)


def _ffn_kernel(x_ref, wg_ref, wu_ref, wd_ref, g_ref, b_ref, o_ref, ob_ref, *, alpha, fc):
    n_f = wg_ref.shape[1]
    for r0 in range(0, x_ref.shape[0], MM_SUB_ROWS):
        rs = slice(r0, r0 + MM_SUB_ROWS)
        x = x_ref[rs, :]
        xb = x.astype(BF16)
        acc = jnp.zeros(x.shape, F32)
        for c in range(n_f // fc):
            sl = slice(c * fc, (c + 1) * fc)
            gate = jnp.dot(xb, wg_ref[:, sl], preferred_element_type=F32)
            up = jnp.dot(xb, wu_ref[:, sl], preferred_element_type=F32)
            h = (gate * jax.nn.sigmoid(gate)) * up
            acc = acc + jnp.dot(h.astype(BF16), wd_ref[sl, :], preferred_element_type=F32)
        out = _layer_norm(alpha * x + 0.5 * acc, g_ref[...], b_ref[...])
        o_ref[rs, :] = out
        ob_ref[rs, :] = out.astype(BF16)


def _ffn(x, wg, wu, wd, g, b, alpha):
    s, d = x.shape
    n_f = wg.shape[1]
    tm = min(FFN_ROWS, s)
    fc = 256
    assert s % tm == 0 and tm % MM_SUB_ROWS == 0 and n_f % fc == 0
    const = lambda i: (0, 0)
    row = lambda i: (i, 0)
    once = pl.Buffered(1)
    vmem = (2 * 2 * _nbytes((tm, d), F32) + 2 * _nbytes((tm, d), BF16)
            + 3 * _nbytes((d, n_f), BF16) + 8 * _nbytes((MM_SUB_ROWS, d), F32))
    return pl.pallas_call(
        functools.partial(_ffn_kernel, alpha=alpha, fc=fc),
        grid=(s // tm,),
        in_specs=[pl.BlockSpec((tm, d), row), pl.BlockSpec((d, n_f), const, pipeline_mode=once),
                  pl.BlockSpec((d, n_f), const, pipeline_mode=once),
                  pl.BlockSpec((n_f, d), const, pipeline_mode=once),
                  pl.BlockSpec((1, d), const), pl.BlockSpec((1, d), const)],
        out_specs=[pl.BlockSpec((tm, d), row), pl.BlockSpec((tm, d), row)],
        out_shape=[jax.ShapeDtypeStruct((s, d), F32), jax.ShapeDtypeStruct((s, d), BF16)],
        compiler_params=_cparams(("parallel",), vmem),
        name="ffn_deepnorm",
    )(x, wg, wu, wd, g.reshape(1, d), b.reshape(1, d))


def _swap_halves_64(a):
    half = HEAD_DIM // 2
    first = (_lane_iota(a.shape) % HEAD_DIM) < half
    return jnp.where(first, pltpu.roll(a, LANES - half, 1), pltpu.roll(a, half, 1))


def _mm_kernel(*refs, rope, scale, plain_scale):
    if rope:
        x_ref, w_ref, c_ref, s_ref, o_ref = refs[:5]
    else:
        x_ref, w_ref, o_ref = refs
    for r0 in range(0, x_ref.shape[0], MM_SUB_ROWS):
        rs = slice(r0, r0 + MM_SUB_ROWS)
        acc = jnp.dot(x_ref[rs, :], w_ref[...], preferred_element_type=F32)
        if plain_scale is not None:
            refs[5][rs, :] = (acc * plain_scale).astype(refs[5].dtype)
        if rope:
            cos = c_ref[rs, :]
            sin = s_ref[rs, :]
            for c in range(acc.shape[1] // LANES):
                a = acc[:, c * LANES:(c + 1) * LANES]
                r = a * cos + _swap_halves_64(a) * sin
                if scale != 1.0:
                    r = r * scale
                o_ref[rs, c * LANES:(c + 1) * LANES] = r.astype(o_ref.dtype)
        else:
            if scale != 1.0:
                acc = acc * scale
            o_ref[rs, :] = acc.astype(o_ref.dtype)


def _mm(xb, w, cols, out_dtype, *, rope_tabs=None, scale=1.0, plain_scale=None):
    s, k = xb.shape
    col0, n = cols[0], cols[1] - cols[0]
    tm = min(MM_ROWS, s)
    tn = next(c for c in (1024, 768, 512, 256, LANES) if n % c == 0 and col0 % c == 0)
    assert s % tm == 0 and tm % MM_SUB_ROWS == 0 and (plain_scale is None or rope_tabs is not None)
    jb = col0 // tn
    n_out = 1 if plain_scale is None else 2
    in_specs = [pl.BlockSpec((tm, k), lambda i, j: (i, 0)), pl.BlockSpec((k, tn), lambda i, j: (0, jb + j))]
    args = [xb, w]
    if rope_tabs is not None:
        in_specs += [pl.BlockSpec((tm, LANES), lambda i, j: (i, 0))] * 2
        args += list(rope_tabs)
    vmem = 2 * (_nbytes((tm, k), BF16) + _nbytes((k, tn), BF16) + n_out * _nbytes((tm, tn), out_dtype)
                + 2 * _nbytes((tm, LANES), F32)) + 6 * _nbytes((MM_SUB_ROWS, tn), F32)
    out = pl.pallas_call(
        functools.partial(_mm_kernel, rope=rope_tabs is not None, scale=scale, plain_scale=plain_scale),
        grid=(s // tm, n // tn),
        in_specs=in_specs,
        out_specs=[pl.BlockSpec((tm, tn), lambda i, j: (i, j))] * n_out,
        out_shape=[jax.ShapeDtypeStruct((s, n), out_dtype)] * n_out,
        compiler_params=_cparams(("parallel", "parallel"), vmem),
        name="proj_rope" if rope_tabs is not None else "proj",
    )(*args)
    return out[0] if n_out == 1 else out


def _compress_kernel(x_ref, pos_ref, w1_ref, w2_ref, o_ref):
    n_cmp = o_ref.shape[0]
    hidden = w1_ref.shape[3]
    ha = [jnp.zeros((n_cmp, hidden), F32) for _ in range(NSA_GROUPS)]
    hb = [jnp.zeros((n_cmp, hidden), F32) for _ in range(NSA_GROUPS)]
    tail = [jnp.zeros((8, hidden), F32) for _ in range(NSA_GROUPS)]
    for j in range(CMP_STRIDE):
        xj = x_ref[pl.ds(j, n_cmp, stride=CMP_STRIDE), :]
        pa = pos_ref[j:j + 1, :]
        pb = pos_ref[CMP_STRIDE + j:CMP_STRIDE + j + 1, :]
        xa = (xj + pa).astype(BF16)
        xb = (xj + pb).astype(BF16)
        pb8 = jnp.broadcast_to(pb, (8, pb.shape[1])).astype(BF16)
        for g in range(NSA_GROUPS):
            ha[g] = ha[g] + jnp.dot(xa, w1_ref[g, j], preferred_element_type=F32)
            hb[g] = hb[g] + jnp.dot(xb, w1_ref[g, CMP_STRIDE + j], preferred_element_type=F32)
            tail[g] = tail[g] + jnp.dot(pb8, w1_ref[g, CMP_STRIDE + j], preferred_element_type=F32)
    last = _row_iota((n_cmp, 1)) == n_cmp - 1
    outs = []
    for g in range(NSA_GROUPS):
        hb_next = jnp.where(last, tail[g][0:1, :], pltpu.roll(hb[g], n_cmp - 1, 0))
        h = jax.nn.gelu(ha[g] + hb_next)
        outs.append(jnp.dot(h.astype(BF16), w2_ref[...], preferred_element_type=F32))
    lo = _lane_iota(outs[0].shape) < HEAD_DIM
    o_ref[...] = jnp.where(lo, outs[0], outs[1]).astype(o_ref.dtype)


def _compress(x, col, pos, w1, w2):
    s = x.shape[0]
    n_cmp = s // CMP_STRIDE
    hidden = w1.shape[1]
    pos2 = jnp.concatenate([pos] * NSA_GROUPS, axis=1)
    w1r = w1.reshape(CMP_LEN, HEAD_DIM, hidden).astype(BF16)
    zero = jnp.zeros_like(w1r)
    w1g = jnp.stack([jnp.concatenate([w1r if gg == g else zero for gg in range(NSA_GROUPS)], axis=1)
                     for g in range(NSA_GROUPS)])
    w2d = jnp.concatenate([w2, w2], axis=1).astype(BF16)
    vmem = 2 * (_nbytes((s, NSA_KV_W), F32) + _nbytes(w1g.shape, BF16)) + 16 * _nbytes((n_cmp, hidden), F32)
    return pl.pallas_call(
        _compress_kernel,
        grid=(1,),
        in_specs=[pl.BlockSpec((s, NSA_KV_W), lambda i: (0, col)), pl.BlockSpec(pos2.shape, lambda i: (0, 0)),
                  pl.BlockSpec(w1g.shape, lambda i: (0, 0, 0, 0)), pl.BlockSpec(w2d.shape, lambda i: (0, 0))],
        out_specs=pl.BlockSpec((n_cmp, NSA_KV_W), lambda i: (0, 0)),
        out_shape=jax.ShapeDtypeStruct((n_cmp, NSA_KV_W), BF16),
        compiler_params=_cparams(("arbitrary",), vmem),
        name="nsa_compress",
    )(x, pos2, w1g, w2d)


def _stack_group_queries(q_ref, col0, kv_lane_group):
    parts = []
    for h in range(NSA_HPG):
        c = col0 + (h // 2) * LANES
        qc = q_ref[:, c:c + LANES].astype(F32)
        e = h % 2
        lane = _lane_iota(qc.shape)
        mine = (lane >= e * HEAD_DIM) & (lane < (e + 1) * HEAD_DIM)
        qh = jnp.where(mine, qc, 0.0)
        if e != kv_lane_group:
            qh = pltpu.roll(qh, HEAD_DIM, 1)
        parts.append(qh.astype(BF16))
    return jnp.concatenate(parts, axis=0)


def _unstack_group_outputs(o, kv_lane_group):
    chunks = []
    tq = o.shape[0] // NSA_HPG
    for c in range(NSA_HPG // 2):
        halves = []
        for e in range(2):
            h = 2 * c + e
            oh = o[h * tq:(h + 1) * tq, :]
            if e != kv_lane_group:
                oh = pltpu.roll(oh, HEAD_DIM, 1)
            halves.append(oh)
        lo = _lane_iota(halves[0].shape) < HEAD_DIM
        chunks.append(jnp.where(lo, halves[0], halves[1]))
    return jnp.concatenate(chunks, axis=1)


def _qk(q, k):
    return lax.dot_general(q, k, (((1,), (1,)), ((), ())), preferred_element_type=F32)


def _cmp_select_kernel(q_ref, kc_ref, vc_ref, mmap_ref, oc_ref, nsel_ref, *, top_k, n_cls):
    b = pl.program_id(0)
    n_cmp = kc_ref.shape[0]
    n_slc = mmap_ref.shape[0]
    tq = q_ref.shape[0]
    gw = NSA_HPG * HEAD_DIM
    t_row = b * tq + _row_iota((tq, 1))
    t = b * tq + _lane_iota((1, tq))

    def variant(nc, nb, first_class):
        valid = _lane_iota((1, nc)) * CMP_STRIDE + (CMP_LEN - 1) <= t_row
        bias = jnp.concatenate([jnp.where(valid, 0.0, MASKED)] * NSA_HPG, axis=0)
        nbe = min(nb, nc // SLC_RATIO)
        blk = _row_iota((nbe, 1))
        cur = t // SLC_BLOCK
        forced = (blk == 0) | (blk == cur) | (blk == cur - 1)
        future = blk * SLC_BLOCK > t
        blk_f = blk.astype(F32)
        kc = kc_ref[0:nc, :]
        vc = vc_ref[0:nc, :]
        mmap_t = mmap_ref[0:nbe, 0:nc]
        for g in range(NSA_GROUPS):
            qst = _stack_group_queries(q_ref, g * gw, g)
            s = _qk(qst, kc) + bias
            m = jnp.maximum(jnp.max(s, axis=-1, keepdims=True), M_INIT)
            e = jnp.exp2(s - m)
            den = jnp.sum(e, axis=-1, keepdims=True)
            p = e / jnp.maximum(den, 1e-30)
            o = jnp.dot(p.astype(BF16), vc, preferred_element_type=F32)
            oc_ref[:, g * gw:(g + 1) * gw] = _unstack_group_outputs(o, g).astype(oc_ref.dtype)
            psum = p[0:tq]
            for h in range(1, NSA_HPG):
                psum = psum + p[h * tq:(h + 1) * tq]
            hi = psum.astype(BF16)
            r1 = psum - hi.astype(F32)
            mid = r1.astype(BF16)
            low = (r1 - mid.astype(F32)).astype(BF16)
            p_slc = _qk(mmap_t, hi) + _qk(mmap_t, mid) + _qk(mmap_t, low)
            score = jnp.where(forced, -2.0, jnp.where(future, -1.0, p_slc))

            def pick(score, who):
                mx = jnp.max(score, axis=0, keepdims=True)
                first = jnp.min(jnp.where(score == mx, blk_f, float(nbe)), axis=0, keepdims=True)
                hit = blk_f == first
                return jnp.where(hit if who is None else hit & who, -2.0, score)

            for _ in range(top_k - N_FORCED):
                score = pick(score, None)
            if first_class:
                for n_distinct in range(N_FORCED - 1, 0, -1):
                    score = pick(score, t < n_distinct * SLC_BLOCK)
            not_picked = jnp.where(score == -2.0, 0.0, 1.0)
            if nbe < nb:
                not_picked = jnp.concatenate([not_picked, jnp.ones((nb - nbe, tq), F32)], axis=0)
            nsel_ref[:, g * n_slc:g * n_slc + nb] = not_picked.T.astype(nsel_ref.dtype)
            if nb < n_slc:
                nsel_ref[:, g * n_slc + nb:(g + 1) * n_slc] = jnp.ones((tq, n_slc - nb), nsel_ref.dtype)

    for c in range(n_cls):
        nc = (c + 1) * (n_cmp // n_cls)
        nb = min(n_slc, -(-(nc // SLC_RATIO) // LANES) * LANES)
        lo_b = c * (nc // (c + 1)) // (tq // CMP_STRIDE)
        hi_b = nc // (tq // CMP_STRIDE)
        pl.when((b >= lo_b) & (b < hi_b))(functools.partial(variant, nc, nb, c == 0))


def _cmp_select(q_u, kc, vc):
    s = q_u.shape[0]
    n_cmp, n_slc = s // CMP_STRIDE, s // SLC_BLOCK
    top_k = min(SLC_TOPK, n_slc)
    assert top_k >= N_FORCED
    n_cls = max(n_cmp // (2 * LANES), 1)
    tq = min(CMP_Q_BLOCK, s)
    assert s % tq == 0 and n_cmp % n_cls == 0 and (n_cmp // n_cls) % (tq // CMP_STRIDE) == 0
    m = np.arange(n_cmp)[:, None]
    j = np.arange(n_slc)[None, :]
    mmap = jnp.asarray(((m >= SLC_RATIO * j - 1) & (m <= SLC_RATIO * j + SLC_RATIO - 1)).astype(np.float32).T, BF16)
    const = lambda b: (0, 0)
    rows = NSA_HPG * tq
    vmem = 2 * (2 * _nbytes((n_cmp, NSA_KV_W), BF16) + _nbytes((n_cmp, n_slc), BF16)
                + _nbytes((tq, NSA_Q_W), F32) * 2) + 6 * _nbytes((rows, n_cmp), F32)
    return pl.pallas_call(
        functools.partial(_cmp_select_kernel, top_k=top_k, n_cls=n_cls),
        grid=(s // tq,),
        in_specs=[pl.BlockSpec((tq, NSA_Q_W), lambda b: (b, 0)), pl.BlockSpec((n_cmp, NSA_KV_W), const),
                  pl.BlockSpec((n_cmp, NSA_KV_W), const), pl.BlockSpec((n_slc, n_cmp), const)],
        out_specs=[pl.BlockSpec((tq, NSA_Q_W), lambda b: (b, 0)),
                   pl.BlockSpec((tq, NSA_GROUPS * n_slc), lambda b: (b, 0))],
        out_shape=[jax.ShapeDtypeStruct((s, NSA_Q_W), BF16), jax.ShapeDtypeStruct((s, NSA_GROUPS * n_slc), BF16)],
        compiler_params=_cparams(("parallel",), vmem),
        name="nsa_compressed_select",
    )(q_u, kc, vc, mmap)


def _sel_attn_kernel(q_ref, k_ref, vt_ref, nsel_ref, kw_ref, vwt_ref, o_ref, ow_ref, qa_ref, m_ref, acc_ref, s_ref,
                     *, win, w_span):
    b = pl.program_id(0)
    n_slc = nsel_ref.shape[1] // NSA_GROUPS
    n_win = n_slc // win
    tiles_per_win = win * SLC_BLOCK // SEL_TILE
    gw = NSA_HPG * HEAD_DIM
    last = (b * Q_BLOCK) // SEL_TILE
    t_lane = b * Q_BLOCK + _lane_iota((1, Q_BLOCK))
    lane = _lane_iota((Q_BLOCK, LANES))
    head_lanes = lane < HEAD_DIM
    flag_lanes = (lane >= HEAD_DIM) & (lane < HEAD_DIM + win)

    nsel_all = nsel_ref[...].astype(F32)
    if nsel_all.shape[1] < LANES:
        nsel_all = jnp.concatenate([nsel_all, jnp.zeros((Q_BLOCK, LANES - nsel_all.shape[1]), F32)], axis=1)
    for g in range(NSA_GROUPS):
        heads = []
        for h in range(NSA_HPG):
            c = g * gw + (h // 2) * LANES
            qc = q_ref[:, c:c + LANES].astype(F32)
            heads.append(qc if h % 2 == 0 else pltpu.roll(qc, HEAD_DIM, 1))
        for w in range(n_win):
            a = g * n_slc + w * win
            fl = nsel_all[:, (a // LANES) * LANES:(a // LANES + 1) * LANES]
            shift = (HEAD_DIM - a % LANES) % LANES
            if shift:
                fl = pltpu.roll(fl, shift, 1)
            fl = jnp.where(flag_lanes, fl, 0.0)
            for h in range(NSA_HPG):
                qa_ref[g, w, h * Q_BLOCK:(h + 1) * Q_BLOCK, :] = jnp.where(head_lanes, heads[h], fl).astype(BF16)
    m_ref[...] = jnp.full(m_ref.shape, M_INIT, F32)
    acc_ref[...] = jnp.zeros(acc_ref.shape, F32)

    def heads_to_lanes(o_t, out_ref, g):
        o_t = jnp.concatenate([o_t, jnp.zeros((LANES - o_t.shape[0], o_t.shape[1]), F32)], axis=0)
        heads = [o_t[:, h * Q_BLOCK:(h + 1) * Q_BLOCK].T for h in range(NSA_HPG)]
        for c in range(NSA_HPG // 2):
            even, odd = heads[2 * c], pltpu.roll(heads[2 * c + 1], HEAD_DIM, 1)
            out_ref[:, g * gw + c * LANES:g * gw + (c + 1) * LANES] = jnp.where(head_lanes, even, odd).astype(out_ref.dtype)

    n_back = NSA_WINDOW // Q_BLOCK
    blk0 = jnp.maximum(b - n_back, 0)
    start = pl.multiple_of(blk0 * Q_BLOCK, Q_BLOCK)
    diff = t_lane - (start + _row_iota((w_span, 1)))
    w_bias = jnp.where((diff >= 0) & (diff < NSA_WINDOW), 0.0, MASKED)
    w_bias = jnp.concatenate([w_bias] * NSA_HPG, axis=1)
    for g in range(NSA_GROUPS):
        s = _qk(kw_ref[g, pl.ds(start, w_span), :], qa_ref[g, 0]) + w_bias
        m = jnp.maximum(jnp.max(s, axis=0, keepdims=True), M_INIT)
        p = jnp.exp2(s - m).astype(BF16)
        vt = jnp.concatenate([vwt_ref[g, blk0 + i] for i in range(w_span // Q_BLOCK)], axis=1)
        acc = jnp.dot(vt, p, preferred_element_type=F32)
        heads_to_lanes(acc / jnp.maximum(acc[HEAD_DIM:HEAD_DIM + 1, :], 1e-30), ow_ref, g)

    def scores(kt, slot):
        off = pl.multiple_of(kt * SEL_TILE, SEL_TILE)
        for g in range(NSA_GROUPS):
            s_ref[g, slot] = _qk(k_ref[g, pl.ds(off, SEL_TILE), :], qa_ref[g, kt // tiles_per_win])

    def update(kt, slot, causal):
        off = pl.multiple_of(kt * SEL_TILE, SEL_TILE)
        for g in range(NSA_GROUPS):
            s = s_ref[g, slot]
            if causal:
                keep = off + _row_iota((SEL_TILE, 1)) <= t_lane
                s = jnp.where(jnp.concatenate([keep] * NSA_HPG, axis=1), s, MASKED)
            m_old = m_ref[g]
            m_new = jnp.maximum(m_old, jnp.max(s, axis=0, keepdims=True))
            alpha = jnp.exp2(m_old - m_new)
            p = jnp.exp2(s - m_new[0:1, :]).astype(BF16)
            pv = jnp.dot(vt_ref[g, kt], p, preferred_element_type=F32)
            acc_ref[g] = alpha[0:1, :] * acc_ref[g] + pv
            m_ref[g] = m_new

    def finish():
        for g in range(NSA_GROUPS):
            acc = acc_ref[g]
            heads_to_lanes(acc / jnp.maximum(acc[HEAD_DIM:HEAD_DIM + 1, :], 1e-30), o_ref, g)

    scores(0, 0)

    def pair(j, carry):
        scores(2 * j + 1, 1)
        update(2 * j, 0, False)
        scores(2 * j + 2, 0)
        update(2 * j + 1, 1, False)
        return carry

    lax.fori_loop(0, last // 2, pair, 0)

    @pl.when(last % 2 == 1)
    def _():
        scores(last, 1)
        update(last - 1, 0, False)
        update(last, 1, True)
        finish()

    @pl.when(last % 2 == 0)
    def _():
        update(last, 0, True)
        finish()


def _sel_attn(rot_a, k_s, v_s, nsel, k_w, v_w):
    s = rot_a.shape[0]
    n_slc = s // SLC_BLOCK
    win = min(n_slc, LANES - HEAD_DIM)
    assert s % SEL_TILE == 0 and n_slc % win == 0 and (win * SLC_BLOCK) % SEL_TILE == 0
    n_win = n_slc // win
    spare = LANES - HEAD_DIM
    w_span = min(NSA_WINDOW + Q_BLOCK, s)
    blk = (np.arange(s) // SLC_BLOCK) % win
    marks = jnp.asarray(np.where(np.arange(spare)[None, :] == blk[:, None], MASKED, 0.0).astype(np.float32), BF16)
    grp = lambda a, g: a[:, g * HEAD_DIM:(g + 1) * HEAD_DIM]
    k_aug = jnp.stack([jnp.concatenate([grp(k_s, g), marks], axis=1) for g in range(NSA_GROUPS)])
    kw_aug = jnp.stack([jnp.concatenate([grp(k_w, g), jnp.zeros((s, spare), BF16)], axis=1) for g in range(NSA_GROUPS)])

    def transposed_tiles(v, tile):
        vt = _ones_augmented_values(v).reshape(NSA_GROUPS, s // tile, tile, LANES).transpose(0, 1, 3, 2)
        return vt[:, :, :SEL_VT_ROWS, :]

    vt_aug = transposed_tiles(v_s, SEL_TILE)
    vwt_aug = transposed_tiles(v_w, Q_BLOCK)
    rows = NSA_HPG * Q_BLOCK
    once = pl.Buffered(1)
    vmem = (2 * _nbytes(k_aug.shape, BF16) + _nbytes(vt_aug.shape, BF16) + _nbytes(vwt_aug.shape, BF16)
            + 2 * (2 * _nbytes((Q_BLOCK, NSA_Q_W), BF16) + _nbytes((Q_BLOCK, NSA_GROUPS * n_slc), BF16)
                   + _nbytes((Q_BLOCK, NSA_Q_W), BF16))
            + NSA_GROUPS * (n_win * _nbytes((rows, LANES), BF16) + _nbytes((8 + SEL_VT_ROWS, rows), F32)
                            + 2 * _nbytes((SEL_TILE, rows), F32))
            + 4 * _nbytes((SEL_TILE, rows), F32))
    return pl.pallas_call(
        functools.partial(_sel_attn_kernel, win=win, w_span=w_span),
        grid=(s // Q_BLOCK,),
        in_specs=[pl.BlockSpec((Q_BLOCK, NSA_Q_W), lambda b: (b, 0)),
                  pl.BlockSpec(k_aug.shape, lambda b: (0, 0, 0), pipeline_mode=once),
                  pl.BlockSpec(vt_aug.shape, lambda b: (0, 0, 0, 0), pipeline_mode=once),
                  pl.BlockSpec((Q_BLOCK, NSA_GROUPS * n_slc), lambda b: (b, 0)),
                  pl.BlockSpec(kw_aug.shape, lambda b: (0, 0, 0), pipeline_mode=once),
                  pl.BlockSpec(vwt_aug.shape, lambda b: (0, 0, 0, 0), pipeline_mode=once)],
        out_specs=[pl.BlockSpec((Q_BLOCK, NSA_Q_W), lambda b: (b, 0))] * 2,
        out_shape=[jax.ShapeDtypeStruct((s, NSA_Q_W), BF16)] * 2,
        scratch_shapes=[pltpu.VMEM((NSA_GROUPS, n_win, rows, LANES), BF16),
                        pltpu.VMEM((NSA_GROUPS, 8, rows), F32), pltpu.VMEM((NSA_GROUPS, SEL_VT_ROWS, rows), F32),
                        pltpu.VMEM((NSA_GROUPS, 2, SEL_TILE, rows), F32)],
        compiler_params=_cparams(("parallel",), vmem),
        name="nsa_selected_window",
    )(rot_a, k_aug, vt_aug, nsel, kw_aug, vwt_aug)


def _ones_augmented_values(v):
    s = v.shape[0]
    spare = LANES - HEAD_DIM
    ones = jnp.asarray((np.arange(spare)[None, :] == 0).astype(np.float32) * np.ones((s, 1), np.float32), BF16)
    return jnp.stack([jnp.concatenate([v[:, g * HEAD_DIM:(g + 1) * HEAD_DIM], ones], axis=1)
                      for g in range(NSA_GROUPS)])


def _dil_attn_kernel(q_ref, kp_ref, kc_ref, vp_ref, vc_ref, o_ref, lse_ref, kbuf, vbuf, *, span, dil, nb):
    n = pl.program_id(0)
    unit = dil * Q_BLOCK
    kbuf[0:unit, :] = kp_ref[...]
    kbuf[unit:, :] = kc_ref[...]
    vbuf[0:unit, :] = vp_ref[...]
    vbuf[unit:, :] = vc_ref[...]
    qi = _row_iota((Q_BLOCK, 1)) + Q_BLOCK
    ki = _lane_iota((1, 2 * Q_BLOCK))
    delta = qi - ki
    band_bias = jnp.where((delta >= 0) & (delta <= span), 0.0, MASKED)
    lo = _lane_iota((Q_BLOCK, LANES)) < HEAD_DIM

    def item(idx, carry):
        j = idx // dil
        base = j * unit + idx % dil
        if dil > 1:
            q_rows, kv_rows = pl.ds(base, Q_BLOCK, stride=dil), pl.ds(base, 2 * Q_BLOCK, stride=dil)
        else:
            base = pl.multiple_of(base, Q_BLOCK)
            q_rows, kv_rows = pl.ds(base, Q_BLOCK), pl.ds(base, 2 * Q_BLOCK)
        bias = jnp.where((n * nb + j == 0) & (ki < Q_BLOCK), MASKED, band_bias)
        q = q_ref[q_rows, :]
        k = kbuf[kv_rows, :].astype(BF16)
        v = vbuf[kv_rows, :].astype(BF16)
        outs, lses = [], []
        for e in range(2):
            qh = jnp.where(lo if e == 0 else ~lo, q, 0.0).astype(BF16)
            s = _qk(qh, k) + bias
            m = jnp.maximum(jnp.max(s, axis=-1, keepdims=True), M_INIT)
            ex = jnp.exp(s - m)
            den = jnp.maximum(jnp.sum(ex, axis=-1, keepdims=True), 1e-30)
            outs.append(jnp.dot(ex.astype(BF16), v, preferred_element_type=F32) / den)
            lses.append(jnp.broadcast_to(m + jnp.log(den), (Q_BLOCK, LANES)))
        o_ref[q_rows, :] = jnp.where(lo, outs[0], outs[1])
        lse_ref[q_rows, :] = jnp.where(lo, lses[0], lses[1])
        return carry

    lax.fori_loop(0, nb * dil, item, 0, unroll=8)


def _dil_attn(qb, kb, vb, gidx, win, dil):
    s = qb.shape[0]
    unit = dil * Q_BLOCK
    tb = min(max(DIL_STEP_TOKENS, unit), s)
    assert s % tb == 0 and tb % unit == 0
    nb = tb // unit
    span = win // dil
    n_chunks = DIL_OUT_W // LANES
    cur = lambda n, c: (n, gidx * n_chunks + c)
    prev = lambda n, c: (jnp.maximum(n * nb - 1, 0), gidx * n_chunks + c)
    blk, pblk = (tb, LANES), (unit, LANES)
    vmem = (2 * (5 * _nbytes(blk, F32) + 2 * _nbytes(pblk, F32)) + 2 * _nbytes((tb + unit, LANES), F32)
            + 32 * _nbytes((Q_BLOCK, 2 * Q_BLOCK), F32))
    return pl.pallas_call(
        functools.partial(_dil_attn_kernel, span=span, dil=dil, nb=nb),
        grid=(s // tb, n_chunks),
        in_specs=[pl.BlockSpec(blk, cur), pl.BlockSpec(pblk, prev), pl.BlockSpec(blk, cur),
                  pl.BlockSpec(pblk, prev), pl.BlockSpec(blk, cur)],
        out_specs=[pl.BlockSpec(blk, lambda n, c: (n, c)), pl.BlockSpec(blk, lambda n, c: (n, c))],
        out_shape=[jax.ShapeDtypeStruct((s, DIL_OUT_W), F32)] * 2,
        scratch_shapes=[pltpu.VMEM((tb + unit, LANES), F32)] * 2,
        compiler_params=_cparams(("parallel", "parallel"), vmem),
        name=f"dilated_w{win}_d{dil}",
    )(qb, kb, kb, vb, vb)


def _sgu_kernel(uv_ref, g_ref, b_ref, ws_ref, bs_ref, o_ref):
    uv = jax.nn.gelu(uv_ref[...])
    u = uv[:, :SGU_WIDTH]
    v = _layer_norm(uv[:, SGU_WIDTH:], g_ref[...], b_ref[...]).astype(BF16)
    causal = _lane_iota((SGU_CHUNK, SGU_CHUNK)) <= _row_iota((SGU_CHUNK, SGU_CHUNK))
    bs = bs_ref[...]
    for n in range(uv.shape[0] // SGU_CHUNK):
        rs = slice(n * SGU_CHUNK, (n + 1) * SGU_CHUNK)
        for g in range(SGU_GROUPS):
            cs = slice(g * SGU_GROUP_CH, (g + 1) * SGU_GROUP_CH)
            ws = jnp.where(causal, ws_ref[g], 0.0).astype(BF16)
            sv = jnp.dot(ws, v[rs, cs], preferred_element_type=F32) + bs[:, cs]
            o_ref[rs, cs] = (u[rs, cs] * sv).astype(o_ref.dtype)


def _sgu(uv, ln_g, ln_b, w_s, b_s):
    s = uv.shape[0]
    tm = min(SGU_ROWS, s)
    assert s % tm == 0
    bs = jnp.repeat(b_s.T, SGU_GROUP_CH, axis=1)
    c2 = lambda i: (0, 0)
    vmem = 2 * (_nbytes((tm, 2 * SGU_WIDTH), F32) + _nbytes((tm, SGU_WIDTH), F32)) + 6 * _nbytes((tm, 2 * SGU_WIDTH), F32)
    return pl.pallas_call(
        _sgu_kernel,
        grid=(s // tm,),
        in_specs=[pl.BlockSpec((tm, 2 * SGU_WIDTH), lambda i: (i, 0)), pl.BlockSpec((1, SGU_WIDTH), c2),
                  pl.BlockSpec((1, SGU_WIDTH), c2), pl.BlockSpec(w_s.shape, lambda i: (0, 0, 0)),
                  pl.BlockSpec((SGU_CHUNK, SGU_WIDTH), c2)],
        out_specs=pl.BlockSpec((tm, SGU_WIDTH), lambda i: (i, 0)),
        out_shape=jax.ShapeDtypeStruct((s, SGU_WIDTH), BF16),
        compiler_params=_cparams(("parallel",), vmem),
        name="sgu",
    )(uv, ln_g.reshape(1, -1), ln_b.reshape(1, -1), w_s, bs)


def _split_dot(x, w2):
    hi = x.astype(BF16)
    lo = (x - hi.astype(F32)).astype(BF16)
    return jnp.dot(jnp.concatenate([hi, lo], axis=1), w2, preferred_element_type=F32)


def _mix_kernel(x_ref, oc_ref, os_ref, ow_ref, ga_ref, eg_ref, d0_ref, d1_ref, d2_ref, l0_ref, l1_ref, l2_ref,
                sg_ref, gm_ref, wa_ref, wb_ref, wc_ref, wo_ref, g_ref, b_ref, o_ref, *, alpha):
    d = x_ref.shape[1]
    gates = _split_dot(jax.nn.sigmoid(ga_ref[...]), eg_ref[...])
    o_a = (gates[:, 0:NSA_Q_W] * oc_ref[...].astype(F32) + gates[:, NSA_Q_W:2 * NSA_Q_W] * os_ref[...].astype(F32)
           + gates[:, 2 * NSA_Q_W:3 * NSA_Q_W] * ow_ref[...].astype(F32))
    y_a = jnp.dot(o_a.astype(BF16), wa_ref[...], preferred_element_type=F32)
    l0, l1, l2 = l0_ref[...], l1_ref[...], l2_ref[...]
    lm = jnp.maximum(jnp.maximum(l0, l1), l2)
    e0, e1, e2 = jnp.exp(l0 - lm), jnp.exp(l1 - lm), jnp.exp(l2 - lm)
    den = e0 + e1 + e2
    o_b = (e0 / den) * d0_ref[...] + (e1 / den) * d1_ref[...] + (e2 / den) * d2_ref[...]
    y_b = jnp.dot(o_b.astype(BF16), wb_ref[...], preferred_element_type=F32)
    y_c = jnp.dot(sg_ref[...].astype(BF16), wc_ref[...], preferred_element_type=F32)
    gm = jax.nn.sigmoid(gm_ref[...].astype(F32))
    merged = gm[:, 0:d] * y_a + gm[:, d:2 * d] * y_b + gm[:, 2 * d:3 * d] * y_c
    mix = jnp.dot(merged.astype(BF16), wo_ref[...], preferred_element_type=F32)
    o_ref[...] = _layer_norm(alpha * x_ref[...] + mix, g_ref[...], b_ref[...])


def _mix(x, o_c, o_s, o_w, g_a, dil_o, dil_lse, sgu, g_m, wa, wb, wc, wo, g, b, alpha):
    s, d = x.shape
    tm = min(MIX_ROWS, s)
    assert s % tm == 0
    col = np.arange(N_BRANCH * NSA_Q_W)
    head, br = (col % NSA_Q_W) // HEAD_DIM, col // NSA_Q_W
    eg1 = (np.arange(LANES)[:, None] == (head * N_BRANCH + br)[None, :]).astype(np.float32)
    eg = jnp.asarray(np.concatenate([eg1, eg1], axis=0), BF16)
    row = lambda i: (i, 0)
    const = lambda i: (0, 0)
    rspec = lambda w: pl.BlockSpec((tm, w), row)
    cspec = lambda a: pl.BlockSpec(a.shape, const)
    widths = [d, NSA_Q_W, NSA_Q_W, NSA_Q_W, LANES] + [DIL_OUT_W] * 6 + [SGU_WIDTH, N_BRANCH * d]
    vmem = (2 * sum(_nbytes((tm, w), F32) for w in widths) + 2 * _nbytes((tm, d), F32)
            + 2 * sum(_nbytes(a.shape, BF16) for a in (eg, wa, wb, wc, wo)) + 10 * _nbytes((tm, N_BRANCH * d), F32))
    return pl.pallas_call(
        functools.partial(_mix_kernel, alpha=alpha),
        grid=(s // tm,),
        in_specs=[rspec(d), rspec(NSA_Q_W), rspec(NSA_Q_W), rspec(NSA_Q_W), rspec(LANES), cspec(eg)]
                 + [rspec(DIL_OUT_W)] * 6 + [rspec(SGU_WIDTH), rspec(N_BRANCH * d)]
                 + [cspec(wa), cspec(wb), cspec(wc), cspec(wo), pl.BlockSpec((1, d), const), pl.BlockSpec((1, d), const)],
        out_specs=rspec(d),
        out_shape=jax.ShapeDtypeStruct((s, d), F32),
        compiler_params=_cparams(("parallel",), vmem),
        name="mixer_merge_deepnorm",
    )(x, o_c, o_s, o_w, g_a, eg, *dil_o, *dil_lse, sgu, g_m, wa, wb, wc, wo, g.reshape(1, d), b.reshape(1, d))


def _rope_tables(s):
    inv = 1.0 / (ROPE_THETA ** (np.arange(0, HEAD_DIM, 2, dtype=np.float32) / HEAD_DIM)).astype(np.float32)
    ang = (np.arange(s, dtype=np.float32)[:, None] * inv[None, :]).astype(np.float64)
    cos, sin = jnp.asarray(np.cos(ang), F32), jnp.asarray(np.sin(ang), F32)
    reps = LANES // HEAD_DIM
    return jnp.tile(jnp.concatenate([cos, cos], axis=1), (1, reps)), jnp.tile(jnp.concatenate([-sin, sin], axis=1), (1, reps))


def kernel(x, ln_g, ln_b, ffn1_gate, ffn1_up, ffn1_down, ffn2_gate, ffn2_up, ffn2_down, w_in, phi_k_pos, phi_k_w1, phi_k_w2, phi_v_pos, phi_v_w1, phi_v_w2, sgu_ln_g, sgu_ln_b, sgu_w, sgu_b, w_branch_a, w_branch_b, w_branch_c, w_out):
    bsz, s, d = x.shape
    depth = ln_g.shape[0]
    alpha = float((2 * depth) ** 0.25)
    q_scale = float(HEAD_DIM ** -0.5)
    tabs = _rope_tables(s)
    sizes = [NSA_Q_W] + [NSA_KV_W] * 6 + [N_BRANCH * NSA_HEADS, DIL_W, DIL_W, DIL_W, 2 * SGU_WIDTH, N_BRANCH * d]
    offs = np.concatenate([[0], np.cumsum(sizes)]).tolist()
    outs = []
    for bi in range(bsz):
        h = x[bi]
        for l in range(depth):
            (w_qa, w_kc, w_vc, w_ks, w_vs, w_kw, w_vw, w_ga, w_qb, w_kb, w_vb, w_uv, w_gm) = [
                w_in[l][:, offs[i]:offs[i + 1]] for i in range(len(sizes))]
            w_ga = jnp.pad(w_ga, ((0, 0), (0, LANES - w_ga.shape[1])))
            groups = dict(qb=w_qb, kb=w_kb, vb=w_vb, k_rot=jnp.concatenate([w_ks, w_kw], axis=1),
                          kv_c=jnp.concatenate([w_kc, w_vc], axis=1), v_plain=jnp.concatenate([w_vs, w_vw], axis=1),
                          uv=w_uv, gm=w_gm, qa=w_qa, ga=w_ga)
            wp = jnp.concatenate(list(groups.values()), axis=1).astype(BF16)
            ends = np.cumsum([g.shape[1] for g in groups.values()]).tolist()
            cols = {name: (e - g.shape[1], e) for (name, g), e in zip(groups.items(), ends)}

            h, hb = _ffn(h, ffn1_gate[l].astype(BF16), ffn1_up[l].astype(BF16), ffn1_down[l].astype(BF16),
                         ln_g[l, 0], ln_b[l, 0], alpha)

            q_rot, q_u = _mm(hb, wp, cols["qa"], BF16, rope_tabs=tabs, scale=q_scale * LOG2E, plain_scale=q_scale * LOG2E)
            k_rot = _mm(hb, wp, cols["k_rot"], BF16, rope_tabs=tabs)
            qb = _mm(hb, wp, cols["qb"], F32, rope_tabs=tabs, scale=q_scale)
            kb = _mm(hb, wp, cols["kb"], F32, rope_tabs=tabs)
            vb = _mm(hb, wp, cols["vb"], F32)
            kv_c = _mm(hb, wp, cols["kv_c"], F32)
            v_plain = _mm(hb, wp, cols["v_plain"], BF16)
            uv = _mm(hb, wp, cols["uv"], F32)
            g_m = _mm(hb, wp, cols["gm"], BF16)
            g_a = _mm(hb, wp, cols["ga"], F32)

            kc = _compress(kv_c, 0, phi_k_pos[l], phi_k_w1[l], phi_k_w2[l])
            vc = _compress(kv_c, 1, phi_v_pos[l], phi_v_w1[l], phi_v_w2[l])
            o_c, nsel = _cmp_select(q_u, kc, vc)
            o_s, o_w = _sel_attn(q_rot, k_rot[:, :NSA_KV_W], v_plain[:, :NSA_KV_W], nsel,
                                 k_rot[:, NSA_KV_W:], v_plain[:, NSA_KV_W:])
            dil = [_dil_attn(qb, kb, vb, gi, win, dl) for gi, (win, dl) in enumerate(DIL_PAIRS)]
            sg = _sgu(uv, sgu_ln_g[l], sgu_ln_b[l], sgu_w[l], sgu_b[l])
            h = _mix(h, o_c, o_s, o_w, g_a, [o for o, _ in dil], [e for _, e in dil], sg, g_m,
                     w_branch_a[l].astype(BF16), w_branch_b[l].astype(BF16), w_branch_c[l].astype(BF16),
                     w_out[l].astype(BF16), ln_g[l, 1], ln_b[l, 1], alpha)

            h, _ = _ffn(h, ffn2_gate[l].astype(BF16), ffn2_up[l].astype(BF16), ffn2_down[l].astype(BF16),
                        ln_g[l, 2], ln_b[l, 2], alpha)
        outs.append(h)
    return jnp.stack(outs, axis=0)
```
